```python
import math
import jax, jax.numpy as jnp
from jax import lax
import numpy as np

D_MODEL = 1024
BATCH = 8
SEQ = 2048
DEPTH = 2
DEC_BATCH = 128
DEC_SEQ = 4
PAST_LEN = 16384
PAGE_SIZE = 128

N_EVEN = (DEPTH + 1) // 2
N_ODD = DEPTH // 2
CHUNK = 128
EPS = 1e-6

A_HEADS = 4
A_DK = 128
A_DV = 256
A_WIDTH = A_HEADS * A_DV
B_HEADS = 16
B_HEADDIM = 64
B_WIDTH = B_HEADS * B_HEADDIM
B_GROUPS = 2
B_STATE = 128
B_CONV = 4
B_CONV_DIM = B_WIDTH + 2 * B_GROUPS * B_STATE
C_HEADS = 8
C_DK = 128
C_DV = 256
C_WIDTH = C_HEADS * C_DV
ROPE_BASE = 10000.0
D_FF = 2816
FFN_CONV = 3

A_IN = 2 * A_HEADS * A_DK + A_WIDTH + 2 * A_HEADS + A_WIDTH
B_IN = B_WIDTH + B_CONV_DIM + B_HEADS
EVEN_IN = A_IN + B_IN
EVEN_MIX = A_WIDTH + B_WIDTH
ODD_IN = 2 * C_HEADS * C_DK + 2 * C_WIDTH

kernel_name = 'mlstm_ssd_retention_convffn_hybrid_step'


def rmsnorm(x, g):
    xf = x.astype(jnp.float32)
    y = xf * lax.rsqrt(jnp.mean(xf * xf, axis=-1, keepdims=True) + EPS)
    return (y * g.astype(jnp.float32)).astype(x.dtype)


def head_norm(h, g):
    mu = jnp.mean(h, axis=-1, keepdims=True)
    hc = h - mu
    y = hc * lax.rsqrt(jnp.mean(hc * hc, axis=-1, keepdims=True) + EPS)
    return y.reshape(h.shape[0], h.shape[1], -1) * g.astype(jnp.float32)


def causal_dwconv(x, buf, w, b):
    width = w.shape[0]
    T = x.shape[1]
    xx = jnp.concatenate([buf.astype(x.dtype), x], axis=1)
    y = b + xx[:, 0:T] * w[0]
    for j in range(1, width):
        y = y + xx[:, j:j + T] * w[j]
    return y, xx[:, T:]


def rope(x, pos):
    half = x.shape[-1] // 2
    inv = ROPE_BASE ** (-jnp.arange(half, dtype=jnp.float32) / half)
    ang = pos.astype(jnp.float32)[:, None] * inv[None, :]
    cos = jnp.cos(ang)[None, :, None, :]
    sin = jnp.sin(ang)[None, :, None, :]
    x = x.astype(jnp.float32)
    x1, x2 = x[..., :half], x[..., half:]
    return jnp.concatenate([x1 * cos - x2 * sin, x1 * sin + x2 * cos], axis=-1)


def _chunks(a, L):
    b, t = a.shape[0], a.shape[1]
    return jnp.swapaxes(a.reshape((b, t // L, L) + a.shape[2:]), 0, 1)


def _unchunks(a):
    a = jnp.swapaxes(a, 0, 1)
    return a.reshape((a.shape[0], a.shape[1] * a.shape[2]) + a.shape[3:])


def mlstm_chunked(q, k, v, i_pre, f_pre, C0, n0, m0):
    f32 = jnp.float32
    T = q.shape[1]
    L = math.gcd(T, CHUNK)
    q = q.astype(f32)
    k = k.astype(f32) * (A_DK ** -0.5)
    v = v.astype(f32)
    log_i = i_pre.astype(f32)
    log_f = jax.nn.log_sigmoid(f_pre.astype(f32))
    causal = jnp.tril(jnp.ones((L, L), dtype=bool))

    def step(carry, inp):
        C, n, m = carry
        qc, kc, vc, li, lf = inp
        b = jnp.swapaxes(jnp.cumsum(lf, axis=1), 1, 2)
        li = jnp.swapaxes(li, 1, 2)
        logw = jnp.where(causal, b[..., :, None] - b[..., None, :] + li[..., None, :], -jnp.inf)
        log_prev = b + m[..., None]
        m_t = jnp.maximum(log_prev, jnp.max(logw, axis=-1))
        w_in = jnp.exp(logw - m_t[..., None])
        w_prev = jnp.exp(log_prev - m_t)
        s = jnp.einsum('blhd,bshd->bhls', qc, kc) * w_in
        num = jnp.einsum('bhls,bshv->bhlv', s, vc) + jnp.einsum('blhd,bhdv->bhlv', qc, C) * w_prev[..., None]
        den = jnp.sum(s, axis=-1) + jnp.einsum('blhd,bhd->bhl', qc, n) * w_prev
        h = num / jnp.maximum(jnp.abs(den), jnp.exp(-m_t))[..., None]
        b_last = b[..., -1]
        log_g = b_last[..., None] - b + li
        m_new = jnp.maximum(b_last + m, jnp.max(log_g, axis=-1))
        g = jnp.swapaxes(jnp.exp(log_g - m_new[..., None]), 1, 2)[..., None]
        decay = jnp.exp(b_last + m - m_new)
        C_new = C * decay[..., None, None] + jnp.einsum('bshd,bshv->bhdv', kc * g, vc)
        n_new = n * decay[..., None] + jnp.sum(kc * g, axis=1)
        return (C_new, n_new, m_new), jnp.swapaxes(h, 1, 2)

    xs = (_chunks(q, L), _chunks(k, L), _chunks(v, L), _chunks(log_i, L), _chunks(log_f, L))
    (C, n, m), h = lax.scan(step, (C0.astype(f32), n0.astype(f32), m0.astype(f32)), xs)
    return _unchunks(h), C, n, m


def ssd_chunked(x, Bm, Cm, dt, A, h0):
    f32 = jnp.float32
    bsz, T = x.shape[0], x.shape[1]
    L = math.gcd(T, CHUNK)
    E = B_HEADS // B_GROUPS
    x = x.astype(f32).reshape(bsz, T, B_GROUPS, E, B_HEADDIM)
    dt = dt.astype(f32).reshape(bsz, T, B_GROUPS, E)
    a = dt * A.astype(f32).reshape(B_GROUPS, E)
    Bm = Bm.astype(f32)
    Cm = Cm.astype(f32)
    causal = jnp.tril(jnp.ones((L, L), dtype=bool))

    def step(S, inp):
        xc, bc, cc, dtc, ac = inp
        acum = jnp.cumsum(ac, axis=1)
        at = jnp.moveaxis(acum, 1, -1)
        seg = at[..., :, None] - at[..., None, :]
        decay = jnp.where(causal, jnp.exp(jnp.where(causal, seg, 0.0)), 0.0)
        cb = jnp.einsum('blgn,bsgn->bgls', cc, bc)
        M = cb[:, :, None] * decay * jnp.moveaxis(dtc, 1, -1)[..., None, :]
        y = jnp.einsum('bgels,bsgep->blgep', M, xc)
        y = y + jnp.einsum('blgn,bgepn->blgep', cc, S) * jnp.exp(acum)[..., None]
        a_last = acum[:, -1]
        w = jnp.exp(a_last[:, None] - acum) * dtc
        S_new = S * jnp.exp(a_last)[..., None, None] + jnp.einsum('bsgep,bsgn->bgepn', xc * w[..., None], bc)
        return S_new, y

    S0 = h0.astype(f32).reshape(bsz, B_GROUPS, E, B_HEADDIM, B_STATE)
    xs = (_chunks(x, L), _chunks(Bm, L), _chunks(Cm, L), _chunks(dt, L), _chunks(a, L))
    S, y = lax.scan(step, S0, xs)
    return _unchunks(y).reshape(bsz, T, B_HEADS, B_HEADDIM), S.reshape(bsz, B_HEADS, B_HEADDIM, B_STATE)


def retention_chunked(q, k, v, S0):
    f32 = jnp.float32
    T = q.shape[1]
    L = math.gcd(T, CHUNK)
    k = k * (C_DK ** -0.5)
    v = v.astype(f32)
    log_gamma = jnp.log1p(-jnp.exp2(-5.0 - jnp.arange(C_HEADS, dtype=f32)))
    idx = jnp.arange(L, dtype=f32)
    diff = idx[:, None] - idx[None, :]
    intra = jnp.where(diff >= 0, jnp.exp(log_gamma[:, None, None] * jnp.maximum(diff, 0.0)), 0.0)
    cross = jnp.exp(log_gamma[:, None] * (idx + 1.0)).T[:, :, None]
    into = jnp.exp(log_gamma[:, None] * (L - 1.0 - idx)).T[:, :, None]
    chunk_decay = jnp.exp(log_gamma * L)

    def step(S, inp):
        qc, kc, vc = inp
        s = jnp.einsum('blhd,bshd->bhls', qc, kc) * intra
        o = jnp.einsum('bhls,bshv->blhv', s, vc) + jnp.einsum('blhd,bhdv->blhv', qc, S) * cross
        S_new = S * chunk_decay[:, None, None] + jnp.einsum('bshd,bshv->bhdv', kc * into, vc)
        return S_new, o

    S, o = lax.scan(step, S0.astype(f32), (_chunks(q, L), _chunks(k, L), _chunks(v, L)))
    return _unchunks(o), S


def even_mixer(h, C0, n0, m0, conv0, ssm0, w_in, ig_b, fg_b, a_norm_g, conv_w, conv_b,
               dt_bias, A_log, D_skip, b_norm_g, w_out):
    f32 = jnp.float32
    bsz, T, _ = h.shape
    p = h @ w_in
    sizes = [A_HEADS * A_DK, A_HEADS * A_DK, A_WIDTH, A_HEADS, A_HEADS, A_WIDTH, B_WIDTH, B_CONV_DIM, B_HEADS]
    q, k, v, ig, fg, og, z, xbc, dt = jnp.split(p, np.cumsum(sizes)[:-1].tolist(), axis=-1)
    ha, C, n, m = mlstm_chunked(q.reshape(bsz, T, A_HEADS, A_DK), k.reshape(bsz, T, A_HEADS, A_DK),
                                v.reshape(bsz, T, A_HEADS, A_DV), ig + ig_b, fg + fg_b, C0, n0, m0)
    ya = head_norm(ha, a_norm_g) * jax.nn.sigmoid(og.astype(f32))
    xbc, conv_new = causal_dwconv(xbc, conv0, conv_w, conv_b)
    xbc = jax.nn.silu(xbc.astype(f32))
    xs, Bm, Cm = jnp.split(xbc, [B_WIDTH, B_WIDTH + B_GROUPS * B_STATE], axis=-1)
    xs = xs.reshape(bsz, T, B_HEADS, B_HEADDIM)
    dt = jax.nn.softplus(dt.astype(f32) + dt_bias)
    A = -jnp.exp(A_log.astype(f32))
    yb, ssm = ssd_chunked(xs, Bm.reshape(bsz, T, B_GROUPS, B_STATE), Cm.reshape(bsz, T, B_GROUPS, B_STATE), dt, A, ssm0)
    yb = (yb + D_skip.astype(f32)[:, None] * xs).reshape(bsz, T, B_WIDTH) * jax.nn.silu(z.astype(f32))
    yg = yb.reshape(bsz, T, B_GROUPS, -1)
    yg = yg * lax.rsqrt(jnp.mean(yg * yg, axis=-1, keepdims=True) + EPS)
    yb = yg.reshape(bsz, T, B_WIDTH) * b_norm_g.astype(f32)
    out = jnp.concatenate([ya, yb], axis=-1).astype(h.dtype) @ w_out
    return out, C, n, m, conv_new, ssm


def odd_mixer(h, pos, S0, w_in, norm_g, w_out):
    f32 = jnp.float32
    bsz, T, _ = h.shape
    p = h @ w_in
    q, k, v, g = jnp.split(p, [C_HEADS * C_DK, 2 * C_HEADS * C_DK, 2 * C_HEADS * C_DK + C_WIDTH], axis=-1)
    q = rope(q.reshape(bsz, T, C_HEADS, C_DK), pos)
    k = rope(k.reshape(bsz, T, C_HEADS, C_DK), pos)
    o, S = retention_chunked(q, k, v.reshape(bsz, T, C_HEADS, C_DV), S0)
    y = head_norm(o, norm_g) * jax.nn.silu(g.astype(f32))
    return y.astype(h.dtype) @ w_out, S


def conv_ffn(h, buf, w_up, conv_w, conv_b, w_down):
    u = h @ w_up
    u, buf_new = causal_dwconv(u, buf, conv_w, conv_b)
    gate, val = jnp.split(u, 2, axis=-1)
    return (jax.nn.silu(gate) * val) @ w_down, buf_new


def trunk(x, pos, st_C, st_n, st_m, st_conv, st_h, st_S, st_f,
          norm_mix_g, norm_ffn_g, norm_final_g,
          even_w_in, mlstm_igate_b, mlstm_fgate_b, mlstm_norm_g, ssd_conv_w, ssd_conv_b,
          ssd_dt_bias, ssd_A_log, ssd_D, ssd_norm_g, even_w_out,
          odd_w_in, ret_norm_g, odd_w_out,
          ffn_w_up, ffn_conv_w, ffn_conv_b, ffn_w_down):
    nC, nn_, nm, nconv, nh, nS, nf = [], [], [], [], [], [], []
    for layer in range(DEPTH):
        j = layer // 2
        h = rmsnorm(x, norm_mix_g[layer])
        if layer % 2 == 0:
            mix, C, n, m, cv, hs = even_mixer(h, st_C[j], st_n[j], st_m[j], st_conv[j], st_h[j],
                                              even_w_in[j], mlstm_igate_b[j], mlstm_fgate_b[j], mlstm_norm_g[j],
                                              ssd_conv_w[j], ssd_conv_b[j], ssd_dt_bias[j], ssd_A_log[j],
                                              ssd_D[j], ssd_norm_g[j], even_w_out[j])
            nC.append(C); nn_.append(n); nm.append(m); nconv.append(cv); nh.append(hs)
        else:
            mix, S = odd_mixer(h, pos, st_S[j], odd_w_in[j], ret_norm_g[j], odd_w_out[j])
            nS.append(S)
        x = x + mix.astype(x.dtype)
        f, fb = conv_ffn(rmsnorm(x, norm_ffn_g[layer]), st_f[layer], ffn_w_up[layer],
                         ffn_conv_w[layer], ffn_conv_b[layer], ffn_w_down[layer])
        nf.append(fb)
        x = x + f.astype(x.dtype)
    y = rmsnorm(x, norm_final_g)
    new_state = (jnp.stack(nC).astype(st_C.dtype), jnp.stack(nn_).astype(st_n.dtype),
                 jnp.stack(nm).astype(st_m.dtype), jnp.stack(nconv).astype(st_conv.dtype),
                 jnp.stack(nh).astype(st_h.dtype), jnp.stack(nS).astype(st_S.dtype),
                 jnp.stack(nf).astype(st_f.dtype))
    return y, new_state


def _fresh(s, bsz):
    return jnp.zeros((s.shape[0], bsz) + s.shape[2:], s.dtype)


def setup_inputs(seed: int = 0) -> dict:
    key = jax.random.key(seed)
    ks = iter(jax.random.split(key, 48))
    f32 = jnp.float32

    def nrm(shape, scale):
        return jax.random.normal(next(ks), shape, f32) * scale

    def gain(shape):
        return 1.0 + nrm(shape, 0.02)

    x_prompt = nrm((BATCH, SEQ, D_MODEL), 1.0)
    x_sample = nrm((DEC_BATCH, DEC_SEQ, D_MODEL), 1.0)
    state_mlstm_C = nrm((N_EVEN, DEC_BATCH, A_HEADS, A_DK, A_DV), 0.05)
    state_mlstm_n = nrm((N_EVEN, DEC_BATCH, A_HEADS, A_DK), 0.05)
    state_mlstm_m = nrm((N_EVEN, DEC_BATCH, A_HEADS), 0.5)
    state_ssd_conv = nrm((N_EVEN, DEC_BATCH, B_CONV - 1, B_CONV_DIM), 1.0)
    state_ssd_h = nrm((N_EVEN, DEC_BATCH, B_HEADS, B_HEADDIM, B_STATE), 0.1)
    state_ret_S = nrm((N_ODD, DEC_BATCH, C_HEADS, C_DK, C_DV), 0.1)
    state_ffn_conv = nrm((DEPTH, DEC_BATCH, FFN_CONV - 1, 2 * D_FF), 1.0)

    norm_mix_g = gain((DEPTH, D_MODEL))
    norm_ffn_g = gain((DEPTH, D_MODEL))
    norm_final_g = gain((D_MODEL,))

    even_w_in = nrm((N_EVEN, D_MODEL, EVEN_IN), D_MODEL ** -0.5)
    mlstm_igate_b = nrm((N_EVEN, A_HEADS), 0.1)
    mlstm_fgate_b = jnp.linspace(3.0, 6.0, A_HEADS, dtype=f32)[None, :] + nrm((N_EVEN, A_HEADS), 0.1)
    mlstm_norm_g = gain((N_EVEN, A_WIDTH))
    ssd_conv_w = nrm((N_EVEN, B_CONV, B_CONV_DIM), B_CONV ** -0.5)
    ssd_conv_b = nrm((N_EVEN, B_CONV_DIM), 0.02)
    dt0 = jnp.exp(jax.random.uniform(next(ks), (N_EVEN, B_HEADS), f32, math.log(1e-3), math.log(1e-1)))
    ssd_dt_bias = dt0 + jnp.log(-jnp.expm1(-dt0))
    ssd_A_log = jnp.log(jax.random.uniform(next(ks), (N_EVEN, B_HEADS), f32, 1.0, 16.0))
    ssd_D = gain((N_EVEN, B_HEADS))
    ssd_norm_g = gain((N_EVEN, B_WIDTH))
    even_w_out = nrm((N_EVEN, EVEN_MIX, D_MODEL), EVEN_MIX ** -0.5)

    odd_w_in = nrm((N_ODD, D_MODEL, ODD_IN), D_MODEL ** -0.5)
    ret_norm_g = gain((N_ODD, C_WIDTH))
    odd_w_out = nrm((N_ODD, C_WIDTH, D_MODEL), C_WIDTH ** -0.5)

    ffn_w_up = nrm((DEPTH, D_MODEL, 2 * D_FF), D_MODEL ** -0.5)
    ffn_conv_w = nrm((DEPTH, FFN_CONV, 2 * D_FF), FFN_CONV ** -0.5)
    ffn_conv_b = nrm((DEPTH, 2 * D_FF), 0.02)
    ffn_w_down = nrm((DEPTH, D_FF, D_MODEL), D_FF ** -0.5)

    return {'x_prompt': x_prompt, 'x_sample': x_sample,
            'state_mlstm_C': state_mlstm_C, 'state_mlstm_n': state_mlstm_n, 'state_mlstm_m': state_mlstm_m,
            'state_ssd_conv': state_ssd_conv, 'state_ssd_h': state_ssd_h, 'state_ret_S': state_ret_S,
            'state_ffn_conv': state_ffn_conv,
            'norm_mix_g': norm_mix_g, 'norm_ffn_g': norm_ffn_g, 'norm_final_g': norm_final_g,
            'even_w_in': even_w_in, 'mlstm_igate_b': mlstm_igate_b, 'mlstm_fgate_b': mlstm_fgate_b,
            'mlstm_norm_g': mlstm_norm_g, 'ssd_conv_w': ssd_conv_w, 'ssd_conv_b': ssd_conv_b,
            'ssd_dt_bias': ssd_dt_bias, 'ssd_A_log': ssd_A_log, 'ssd_D': ssd_D, 'ssd_norm_g': ssd_norm_g,
            'even_w_out': even_w_out,
            'odd_w_in': odd_w_in, 'ret_norm_g': ret_norm_g, 'odd_w_out': odd_w_out,
            'ffn_w_up': ffn_w_up, 'ffn_conv_w': ffn_conv_w, 'ffn_conv_b': ffn_conv_b, 'ffn_w_down': ffn_w_down}


def reference(x_prompt, x_sample, state_mlstm_C, state_mlstm_n, state_mlstm_m, state_ssd_conv, state_ssd_h,
              state_ret_S, state_ffn_conv, norm_mix_g, norm_ffn_g, norm_final_g,
              even_w_in, mlstm_igate_b, mlstm_fgate_b, mlstm_norm_g, ssd_conv_w, ssd_conv_b,
              ssd_dt_bias, ssd_A_log, ssd_D, ssd_norm_g, even_w_out,
              odd_w_in, ret_norm_g, odd_w_out,
              ffn_w_up, ffn_conv_w, ffn_conv_b, ffn_w_down):
    weights = (norm_mix_g, norm_ffn_g, norm_final_g,
               even_w_in, mlstm_igate_b, mlstm_fgate_b, mlstm_norm_g, ssd_conv_w, ssd_conv_b,
               ssd_dt_bias, ssd_A_log, ssd_D, ssd_norm_g, even_w_out,
               odd_w_in, ret_norm_g, odd_w_out,
               ffn_w_up, ffn_conv_w, ffn_conv_b, ffn_w_down)
    bsz = x_prompt.shape[0]
    pos_prompt = jnp.arange(x_prompt.shape[1], dtype=jnp.int32)
    pos_sample = PAST_LEN + jnp.arange(x_sample.shape[1], dtype=jnp.int32)
    y_prompt, (p_C, p_n, p_m, p_conv, p_h, p_S, p_f) = trunk(
        x_prompt, pos_prompt,
        _fresh(state_mlstm_C, bsz), _fresh(state_mlstm_n, bsz), _fresh(state_mlstm_m, bsz),
        _fresh(state_ssd_conv, bsz), _fresh(state_ssd_h, bsz), _fresh(state_ret_S, bsz),
        _fresh(state_ffn_conv, bsz), *weights)
    y_sample, (s_C, s_n, s_m, s_conv, s_h, s_S, s_f) = trunk(
        x_sample, pos_sample, state_mlstm_C, state_mlstm_n, state_mlstm_m, state_ssd_conv, state_ssd_h,
        state_ret_S, state_ffn_conv, *weights)
    return (y_prompt, y_sample, p_C, p_n, p_m, p_conv, p_h, p_S, p_f, s_C, s_n, s_m, s_conv, s_h, s_S, s_f)
```

```python
import functools
import math

import numpy as np
import jax
import jax.numpy as jnp
from jax import lax
from jax.experimental import pallas as pl
from jax.experimental.pallas import tpu as pltpu

f32 = jnp.float32
bf16 = jnp.bfloat16

EPS = 1e-6
CHUNK = 128
D_MODEL = 1024
A_HEADS, A_DK, A_DV = 4, 128, 256
A_WIDTH = A_HEADS * A_DV
B_HEADS, B_HEADDIM, B_GROUPS, B_STATE, B_CONV = 16, 64, 2, 128, 4
B_WIDTH = B_HEADS * B_HEADDIM
B_CONV_DIM = B_WIDTH + 2 * B_GROUPS * B_STATE
B_HEADS_PER_GROUP = B_HEADS // B_GROUPS
C_HEADS, C_DK, C_DV = 8, 128, 256
C_WIDTH = C_HEADS * C_DV
ROPE_BASE = 10000.0
D_FF = 2816
FFN_CONV = 3
PAST_LEN = 16384

LANES = 128
SUBLANES = 8
SEQ_PAD = SUBLANES
NEG_BIG = -1e30
VMEM_LIMIT = 56 * 1024 * 1024

EVEN_SPLITS = (A_HEADS * A_DK, A_HEADS * A_DK, A_WIDTH, A_WIDTH, B_WIDTH, B_CONV_DIM, 3 * LANES)
ODD_SPLITS = (C_HEADS * C_DK, C_HEADS * C_DK, C_WIDTH, C_WIDTH)


def _params(sem):
    return pltpu.CompilerParams(dimension_semantics=sem, vmem_limit_bytes=VMEM_LIMIT)


def _nt(a, b):
    return lax.dot_general(a, b, (((1,), (1,)), ((), ())), preferred_element_type=f32)


def _tn(a, b):
    return lax.dot_general(a, b, (((0,), (0,)), ((), ())), preferred_element_type=f32)


def _nn(a, b):
    return jnp.dot(a, b, preferred_element_type=f32)


def _rmsnorm(x, g):
    return x * lax.rsqrt(jnp.mean(x * x, axis=-1, keepdims=True) + EPS) * g


def _layernorm_nogain(h):
    mu = jnp.mean(h, axis=-1, keepdims=True)
    hc = h - mu
    return hc * lax.rsqrt(jnp.mean(hc * hc, axis=-1, keepdims=True) + EPS)


def _softplus(x):
    return jnp.maximum(x, 0.0) + jnp.log1p(jnp.exp(-jnp.abs(x)))


def _silu(x):
    return x * jax.nn.sigmoid(x)


def _cumsum_rows(tril, x):
    hi = x.astype(bf16)
    r1 = x - hi.astype(f32)
    mid = r1.astype(bf16)
    lo = (r1 - mid.astype(f32)).astype(bf16)
    return _nn(tril, hi) + _nn(tril, mid) + _nn(tril, lo)


def _norm_proj_body(x_ref, g_ref, w_ref, *o_refs, splits):
    h = _rmsnorm(x_ref[...], g_ref[...]).astype(bf16)
    off = 0
    for o_ref, n in zip(o_refs, splits):
        o_ref[...] = _nn(h, w_ref[:, off:off + n])
        off += n


def _norm_proj(x, g, w, splits, tm):
    rows, d = x.shape
    return pl.pallas_call(
        functools.partial(_norm_proj_body, splits=splits),
        grid=(rows // tm,),
        in_specs=[pl.BlockSpec((tm, d), lambda i: (i, 0)),
                  pl.BlockSpec((1, d), lambda i: (0, 0)),
                  pl.BlockSpec(w.shape, lambda i: (0, 0))],
        out_specs=[pl.BlockSpec((tm, n), lambda i: (i, 0)) for n in splits],
        out_shape=[jax.ShapeDtypeStruct((rows, n), f32) for n in splits],
        compiler_params=_params(("arbitrary",)),
        name="norm_proj",
    )(x, g, w)


def _proj_res_body(x_ref, y_ref, w_ref, o_ref):
    o_ref[...] = x_ref[...] + _nn(y_ref[...].astype(bf16), w_ref[...])


def _proj_res(x, y, w, tm):
    rows, d = x.shape
    k = y.shape[1]
    return pl.pallas_call(
        _proj_res_body,
        grid=(rows // tm,),
        in_specs=[pl.BlockSpec((tm, d), lambda i: (i, 0)),
                  pl.BlockSpec((tm, k), lambda i: (i, 0)),
                  pl.BlockSpec(w.shape, lambda i: (0, 0))],
        out_specs=pl.BlockSpec((tm, d), lambda i: (i, 0)),
        out_shape=jax.ShapeDtypeStruct((rows, d), f32),
        compiler_params=_params(("arbitrary",)),
        name="proj_res",
    )(x, y, w)


def _ffn_body(*refs, tt, cw, sample, final):
    it = iter(refs)
    x_ref = next(it)
    buf_ref = next(it) if sample else None
    g_ref, wup_ref, cw_ref, cb_ref, wdn_ref = next(it), next(it), next(it), next(it), next(it)
    gf_ref = next(it) if final else None
    o_ref, tail_ref = next(it), next(it)
    yn_ref = next(it) if final else None
    scr = next(it)
    carry = None if sample else next(it)

    if not sample:
        @pl.when(pl.program_id(1) == 0)
        def _():
            carry[...] = jnp.zeros(carry.shape, f32)
    else:
        row = lax.broadcasted_iota(jnp.int32, (tt, 1), 0) % SEQ_PAD
        is_hist = (row >= SEQ_PAD // 2 - (FFN_CONV - 1)) & (row < SEQ_PAD // 2)

    x = x_ref[...]
    h = _rmsnorm(x, g_ref[...]).astype(bf16)
    acc = jnp.zeros((tt, D_MODEL), f32)
    for j in range(D_FF // cw):
        conv = []
        for part in range(2):
            c0 = part * D_FF + j * cw
            u = _nn(h, wup_ref[:, c0:c0 + cw])
            if sample:
                u = jnp.where(is_hist, buf_ref[:, c0:c0 + cw], u)
                scr[0:SUBLANES, :] = jnp.zeros((SUBLANES, cw), f32)
                tail_ref[:, c0:c0 + cw] = u
            else:
                scr[0:SUBLANES, :] = carry[:, c0:c0 + cw]
                tail_ref[:, c0:c0 + cw] = u[tt - SUBLANES:tt, :]
                carry[:, c0:c0 + cw] = u[tt - SUBLANES:tt, :]
            scr[SUBLANES:SUBLANES + tt, :] = u
            y = cb_ref[:, c0:c0 + cw] + scr[SUBLANES - 2:SUBLANES - 2 + tt, :] * cw_ref[0:1, c0:c0 + cw]
            y = y + scr[SUBLANES - 1:SUBLANES - 1 + tt, :] * cw_ref[1:2, c0:c0 + cw]
            y = y + u * cw_ref[2:3, c0:c0 + cw]
            conv.append(y)
        act = (_silu(conv[0]) * conv[1]).astype(bf16)
        acc = acc + _nn(act, wdn_ref[j * cw:(j + 1) * cw, :])
    out = x + acc
    o_ref[...] = out
    if final:
        yn_ref[...] = _rmsnorm(out, gf_ref[...])


def _ffn(x, hist, g, wup, conv_w, conv_b, wdn, g_final, *, seq_rows, tt, cw):
    rows, d = x.shape
    sample = hist is not None
    final = g_final is not None
    n_seq = rows // seq_rows
    if sample:
        grid = (rows // tt,)
        rmap = lambda i: (i, 0)
        cmap = lambda i: (0, 0)
        tail_spec = pl.BlockSpec((tt, 2 * D_FF), rmap)
        sem = ("arbitrary",)
    else:
        n_inner = seq_rows // tt
        grid = (n_seq, n_inner)
        rmap = lambda s, t: (s * n_inner + t, 0)
        cmap = lambda s, t: (0, 0)
        tail_spec = pl.BlockSpec((SUBLANES, 2 * D_FF), lambda s, t: (s, 0))
        sem = ("arbitrary", "arbitrary")
    in_specs = [pl.BlockSpec((tt, d), rmap)]
    args = [x]
    if sample:
        in_specs.append(pl.BlockSpec((tt, 2 * D_FF), rmap))
        args.append(hist)
    in_specs += [pl.BlockSpec((1, d), cmap), pl.BlockSpec(wup.shape, cmap), pl.BlockSpec(conv_w.shape, cmap),
                 pl.BlockSpec(conv_b.shape, cmap), pl.BlockSpec(wdn.shape, cmap)]
    args += [g, wup, conv_w, conv_b, wdn]
    out_specs = [pl.BlockSpec((tt, d), rmap), tail_spec]
    out_shape = [jax.ShapeDtypeStruct((rows, d), f32), jax.ShapeDtypeStruct((n_seq * SUBLANES, 2 * D_FF), f32)]
    if final:
        in_specs.append(pl.BlockSpec((1, d), cmap))
        args.append(g_final)
        out_specs.append(pl.BlockSpec((tt, d), rmap))
        out_shape.append(jax.ShapeDtypeStruct((rows, d), f32))
    scratch = [pltpu.VMEM((SUBLANES + tt, cw), f32)]
    if not sample:
        scratch.append(pltpu.VMEM((SUBLANES, 2 * D_FF), f32))
    return pl.pallas_call(
        functools.partial(_ffn_body, tt=tt, cw=cw, sample=sample, final=final),
        grid=grid, in_specs=in_specs, out_specs=out_specs, out_shape=out_shape,
        scratch_shapes=scratch, compiler_params=_params(sem),
        name="ffn_sample" if sample else "ffn_prompt",
    )(*args)


def _even_chunk(r0, slot, L, n_pad, q_ref, k_ref, v_ref, og_ref, z_ref, xbc_ref, gt_ref, hist_ref,
                gb_ref, anorm_ref, convw_ref, convb_ref, alog_ref, dskip_ref, bnorm_ref,
                y_ref, C_ref, n_ref, m_ref, S_ref, tail_ref, conv_scr, yb_scr):
    rows = pl.ds(r0, L)
    ri = lax.broadcasted_iota(jnp.int32, (L, L), 0)
    ci = lax.broadcasted_iota(jnp.int32, (L, L), 1)
    causal = ri >= ci
    tril = causal.astype(bf16)
    valid = None
    if n_pad:
        valid = lax.broadcasted_iota(jnp.int32, (L, 1), 0) >= n_pad

    gates = gt_ref[rows, :]
    li = gates[:, 0:LANES] + gb_ref[0:1, :]
    fpre = gates[:, LANES:2 * LANES] + gb_ref[1:2, :]
    lf = -_softplus(-fpre)
    dt = _softplus(gates[:, 2 * LANES:3 * LANES] + gb_ref[2:3, :])
    if n_pad:
        li = jnp.where(valid, li, NEG_BIG)
        lf = jnp.where(valid, lf, 0.0)
        dt = jnp.where(valid, dt, 0.0)
    a = dt * (-jnp.exp(alog_ref[...]))
    bcum = _cumsum_rows(tril, lf)
    acum = _cumsum_rows(tril, a)
    liT, bT, aT, dtT = li.T, bcum.T, acum.T, dt.T

    m_row = m_ref[slot]
    b_last = bcum[L - 1:L, :]
    log_g = b_last - bcum + li
    m_new = jnp.maximum(b_last + m_row, jnp.max(log_g, axis=0, keepdims=True))
    gfac = jnp.exp(log_g - m_new)
    cdecay = jnp.exp(b_last + m_row - m_new)
    m_ref[slot] = m_new
    for h in range(A_HEADS):
        bcol, brow, lirow = bcum[:, h:h + 1], bT[h:h + 1, :], liT[h:h + 1, :]
        m_h = m_row[:, h:h + 1]
        logw = jnp.where(causal, bcol - brow + lirow, -jnp.inf)
        log_prev = bcol + m_h
        m_t = jnp.maximum(log_prev, jnp.max(logw, axis=-1, keepdims=True))
        w_in = jnp.exp(logw - m_t)
        w_prev = jnp.exp(log_prev - m_t)
        qh = q_ref[rows, h * A_DK:(h + 1) * A_DK]
        kh = k_ref[rows, h * A_DK:(h + 1) * A_DK] * (A_DK ** -0.5)
        vb = v_ref[rows, h * A_DV:(h + 1) * A_DV].astype(bf16)
        qb = qh.astype(bf16)
        s = _nt(qb, kh.astype(bf16)) * w_in
        C = C_ref[slot, h]
        nrow = n_ref[slot, h:h + 1, :]
        num = _nn(s.astype(bf16), vb) + _nn(qb, C.astype(bf16)) * w_prev
        den = jnp.sum(s, axis=-1, keepdims=True) + jnp.sum(qh * nrow, axis=-1, keepdims=True) * w_prev
        hout = num / jnp.maximum(jnp.abs(den), jnp.exp(-m_t))
        kg = kh * gfac[:, h:h + 1]
        dec_h = cdecay[:, h:h + 1]
        C_ref[slot, h] = C * dec_h + _tn(kg.astype(bf16), vb)
        n_ref[slot, h:h + 1, :] = nrow * dec_h + jnp.sum(kg, axis=0, keepdims=True)
        ya = _layernorm_nogain(hout) * anorm_ref[:, h * A_DV:(h + 1) * A_DV]
        ya = ya * jax.nn.sigmoid(og_ref[rows, h * A_DV:(h + 1) * A_DV])
        y_ref[rows, h * A_DV:(h + 1) * A_DV] = ya

    xraw = xbc_ref[rows, :]
    if hist_ref is not None:
        is_hist = (lax.broadcasted_iota(jnp.int32, (L, 1), 0) >= n_pad - (B_CONV - 1)) & jnp.logical_not(valid)
        xraw = jnp.where(is_hist, hist_ref[rows, :], xraw)
    conv_scr[SUBLANES:SUBLANES + L, :] = xraw
    xc = convb_ref[...] + conv_scr[SUBLANES - 3:SUBLANES - 3 + L, :] * convw_ref[0:1, :]
    xc = xc + conv_scr[SUBLANES - 2:SUBLANES - 2 + L, :] * convw_ref[1:2, :]
    xc = xc + conv_scr[SUBLANES - 1:SUBLANES - 1 + L, :] * convw_ref[2:3, :]
    xc = xc + xraw * convw_ref[3:4, :]
    new_tail = conv_scr[L:L + SUBLANES, :]
    tail_ref[pl.ds(slot * SUBLANES, SUBLANES), :] = new_tail
    conv_scr[0:SUBLANES, :] = new_tail
    xc = _silu(xc)

    a_last = acum[L - 1:L, :]
    wtile = jnp.exp(a_last - acum) * dt
    expa = jnp.exp(acum)
    dec_last = jnp.exp(a_last)
    for g in range(B_GROUPS):
        Bg = xc[:, B_WIDTH + g * B_STATE:B_WIDTH + (g + 1) * B_STATE].astype(bf16)
        c0 = B_WIDTH + B_GROUPS * B_STATE + g * B_STATE
        Cg = xc[:, c0:c0 + B_STATE].astype(bf16)
        cb = _nt(Cg, Bg)
        for e in range(B_HEADS_PER_GROUP):
            h = g * B_HEADS_PER_GROUP + e
            acol, arow, dtrow = acum[:, h:h + 1], aT[h:h + 1, :], dtT[h:h + 1, :]
            decay = jnp.where(causal, jnp.exp(jnp.where(causal, acol - arow, 0.0)), 0.0)
            M = cb * decay * dtrow
            xh = xc[:, h * B_HEADDIM:(h + 1) * B_HEADDIM]
            S = S_ref[slot, h]
            yh = _nn(M.astype(bf16), xh.astype(bf16)) + _nt(Cg, S.astype(bf16)) * expa[:, h:h + 1]
            S_ref[slot, h] = S * dec_last[:, h:h + 1] + _tn((xh * wtile[:, h:h + 1]).astype(bf16), Bg)
            yb_scr[:, h * B_HEADDIM:(h + 1) * B_HEADDIM] = yh + dskip_ref[:, h:h + 1] * xh
    yb = yb_scr[...] * _silu(z_ref[rows, :])
    gw = B_WIDTH // B_GROUPS
    for g in range(B_GROUPS):
        yg = yb[:, g * gw:(g + 1) * gw]
        yg = yg * lax.rsqrt(jnp.mean(yg * yg, axis=-1, keepdims=True) + EPS)
        y_ref[rows, A_WIDTH + g * gw:A_WIDTH + (g + 1) * gw] = yg * bnorm_ref[:, g * gw:(g + 1) * gw]


def _even_body(*refs, L, n_sub, n_pad, has_state):
    it = iter(refs)
    q_ref, k_ref, v_ref, og_ref, z_ref, xbc_ref, gt_ref = (next(it) for _ in range(7))
    if has_state:
        hist_ref, C0_ref, n0_ref, m0_ref, S0_ref = (next(it) for _ in range(5))
    else:
        hist_ref = None
    gb_ref, anorm_ref, convw_ref, convb_ref, alog_ref, dskip_ref, bnorm_ref = (next(it) for _ in range(7))
    y_ref, C_ref, n_ref, m_ref, S_ref, tail_ref = (next(it) for _ in range(6))
    conv_scr, yb_scr = next(it), next(it)

    @pl.when(pl.program_id(1) == 0)
    def _():
        conv_scr[0:SUBLANES, :] = jnp.zeros((SUBLANES, B_CONV_DIM), f32)
        if has_state:
            C_ref[...] = C0_ref[...]
            n_ref[...] = n0_ref[...]
            m_ref[...] = m0_ref[...]
            S_ref[...] = S0_ref[...]
        else:
            C_ref[...] = jnp.zeros(C_ref.shape, f32)
            n_ref[...] = jnp.zeros(n_ref.shape, f32)
            m_ref[...] = jnp.zeros(m_ref.shape, f32)
            S_ref[...] = jnp.zeros(S_ref.shape, f32)

    def step(c, carry):
        r0 = pl.multiple_of(c * L, L)
        _even_chunk(r0, c, L, n_pad, q_ref, k_ref, v_ref, og_ref, z_ref, xbc_ref, gt_ref, hist_ref,
                    gb_ref, anorm_ref, convw_ref, convb_ref, alog_ref, dskip_ref, bnorm_ref,
                    y_ref, C_ref, n_ref, m_ref, S_ref, tail_ref, conv_scr, yb_scr)
        return carry

    if n_sub == 1:
        step(0, 0)
    else:
        lax.fori_loop(0, n_sub, step, 0)


def _even_mix(proj, state, consts, *, seq_rows, L, n_sub, n_pad):
    rows = proj[0].shape[0]
    n_seq = rows // seq_rows
    n_inner = seq_rows // (L * n_sub) if state is None else 1
    n_outer = n_seq if state is None else n_seq // n_sub
    tt = L * n_sub
    slots = 1 if state is None else n_sub
    rmap = lambda o, t: (o * n_inner + t, 0)
    cmap = lambda o, t: (0, 0)
    smap4 = lambda o, t: (o, 0, 0, 0)
    smap3 = lambda o, t: (o, 0, 0)
    in_specs = [pl.BlockSpec((tt, a.shape[1]), rmap) for a in proj]
    args = list(proj)
    st_specs = [pl.BlockSpec((slots, A_HEADS, A_DK, A_DV), smap4), pl.BlockSpec((slots, SUBLANES, A_DK), smap3),
                pl.BlockSpec((slots, 1, LANES), smap3), pl.BlockSpec((slots, B_HEADS, B_HEADDIM, B_STATE), smap4)]
    if state is not None:
        in_specs += [pl.BlockSpec((tt, B_CONV_DIM), rmap)] + st_specs
        args += list(state)
    in_specs += [pl.BlockSpec(c.shape, cmap) for c in consts]
    args += list(consts)
    out_specs = [pl.BlockSpec((tt, A_WIDTH + B_WIDTH), rmap)] + st_specs + [
        pl.BlockSpec((slots * SUBLANES, B_CONV_DIM), lambda o, t: (o, 0))]
    out_shape = [jax.ShapeDtypeStruct((rows, A_WIDTH + B_WIDTH), f32),
                 jax.ShapeDtypeStruct((n_seq, A_HEADS, A_DK, A_DV), f32),
                 jax.ShapeDtypeStruct((n_seq, SUBLANES, A_DK), f32),
                 jax.ShapeDtypeStruct((n_seq, 1, LANES), f32),
                 jax.ShapeDtypeStruct((n_seq, B_HEADS, B_HEADDIM, B_STATE), f32),
                 jax.ShapeDtypeStruct((n_seq * SUBLANES, B_CONV_DIM), f32)]
    return pl.pallas_call(
        functools.partial(_even_body, L=L, n_sub=n_sub, n_pad=n_pad, has_state=state is not None),
        grid=(n_outer, n_inner), in_specs=in_specs, out_specs=out_specs, out_shape=out_shape,
        scratch_shapes=[pltpu.VMEM((SUBLANES + L, B_CONV_DIM), f32), pltpu.VMEM((L, B_WIDTH), f32)],
        compiler_params=_params(("arbitrary", "arbitrary")),
        name="even_mix_sample" if state is not None else "even_mix_prompt",
    )(*args)


def _odd_chunk(r0, slot, L, q_ref, k_ref, v_ref, g_ref, cos_ref, sin_ref, intra_ref, cross_ref, into_ref,
               cdec_ref, norm_ref, y_ref, S_ref):
    rows = pl.ds(r0, L)
    cosf = cos_ref[pl.ds(0, L), :] if cos_ref.shape[0] == L else cos_ref[rows, :]
    sinf = sin_ref[pl.ds(0, L), :] if sin_ref.shape[0] == L else sin_ref[rows, :]
    cross = cross_ref[...]
    into = into_ref[...]
    cdec = cdec_ref[...]
    for h in range(C_HEADS):
        qh = q_ref[rows, h * C_DK:(h + 1) * C_DK]
        kh = k_ref[rows, h * C_DK:(h + 1) * C_DK]
        qh = qh * cosf + pltpu.roll(qh, C_DK // 2, 1) * sinf
        kh = (kh * cosf + pltpu.roll(kh, C_DK // 2, 1) * sinf) * (C_DK ** -0.5)
        vb = v_ref[rows, h * C_DV:(h + 1) * C_DV].astype(bf16)
        qb = qh.astype(bf16)
        s = _nt(qb, kh.astype(bf16)) * intra_ref[h]
        S = S_ref[slot, h]
        o = _nn(s.astype(bf16), vb) + _nn(qb, S.astype(bf16)) * cross[:, h:h + 1]
        S_ref[slot, h] = S * cdec[:, h:h + 1] + _tn((kh * into[:, h:h + 1]).astype(bf16), vb)
        y = _layernorm_nogain(o) * norm_ref[:, h * C_DV:(h + 1) * C_DV]
        y = y * _silu(g_ref[rows, h * C_DV:(h + 1) * C_DV])
        y_ref[rows, h * C_DV:(h + 1) * C_DV] = y


def _odd_body(*refs, L, n_sub, has_state):
    it = iter(refs)
    q_ref, k_ref, v_ref, g_ref = (next(it) for _ in range(4))
    S0_ref = next(it) if has_state else None
    cos_ref, sin_ref, intra_ref, cross_ref, into_ref, cdec_ref, norm_ref = (next(it) for _ in range(7))
    y_ref, S_ref = next(it), next(it)

    @pl.when(pl.program_id(1) == 0)
    def _():
        if has_state:
            S_ref[...] = S0_ref[...]
        else:
            S_ref[...] = jnp.zeros(S_ref.shape, f32)

    def step(c, carry):
        r0 = pl.multiple_of(c * L, L)
        _odd_chunk(r0, c, L, q_ref, k_ref, v_ref, g_ref, cos_ref, sin_ref, intra_ref, cross_ref, into_ref,
                   cdec_ref, norm_ref, y_ref, S_ref)
        return carry

    if n_sub == 1:
        step(0, 0)
    else:
        lax.fori_loop(0, n_sub, step, 0)


def _odd_mix(proj, S0, cos, sin, consts, *, seq_rows, L, n_sub):
    rows = proj[0].shape[0]
    n_seq = rows // seq_rows
    has_state = S0 is not None
    n_inner = 1 if has_state else seq_rows // (L * n_sub)
    n_outer = n_seq // n_sub if has_state else n_seq
    tt = L * n_sub
    slots = n_sub if has_state else 1
    rmap = lambda o, t: (o * n_inner + t, 0)
    cmap2 = lambda o, t: (0, 0)
    st_spec = pl.BlockSpec((slots, C_HEADS, C_DK, C_DV), lambda o, t: (o, 0, 0, 0))
    in_specs = [pl.BlockSpec((tt, a.shape[1]), rmap) for a in proj]
    args = list(proj)
    if has_state:
        in_specs.append(st_spec)
        args.append(S0)
        rope_spec = pl.BlockSpec((L, LANES), cmap2)
    else:
        rope_spec = pl.BlockSpec((tt, LANES), lambda o, t: (t, 0))
    in_specs += [rope_spec, rope_spec]
    args += [cos, sin]
    for c in consts:
        in_specs.append(pl.BlockSpec(c.shape, (lambda o, t: (0, 0, 0)) if c.ndim == 3 else cmap2))
        args.append(c)
    return pl.pallas_call(
        functools.partial(_odd_body, L=L, n_sub=n_sub, has_state=has_state),
        grid=(n_outer, n_inner), in_specs=in_specs,
        out_specs=[pl.BlockSpec((tt, C_WIDTH), rmap), st_spec],
        out_shape=[jax.ShapeDtypeStruct((rows, C_WIDTH), f32),
                   jax.ShapeDtypeStruct((n_seq, C_HEADS, C_DK, C_DV), f32)],
        compiler_params=_params(("arbitrary", "arbitrary")),
        name="odd_mix_sample" if has_state else "odd_mix_prompt",
    )(*args)


def _pad_lanes(a, width=LANES):
    return jnp.pad(a, [(0, 0)] * (a.ndim - 1) + [(0, width - a.shape[-1])])


def _even_w_in_cols(w):
    sizes = [A_HEADS * A_DK, A_HEADS * A_DK, A_WIDTH, A_HEADS, A_HEADS, A_WIDTH, B_WIDTH, B_CONV_DIM, B_HEADS]
    q, k, v, ig, fg, og, z, xbc, dt = jnp.split(w, np.cumsum(sizes)[:-1].tolist(), axis=-1)
    return jnp.concatenate([q, k, v, og, z, xbc, _pad_lanes(ig), _pad_lanes(fg), _pad_lanes(dt)], axis=-1).astype(bf16)


def _retention_tables(L, n_pad):
    log_gamma = jnp.log1p(-jnp.exp2(-5.0 - jnp.arange(C_HEADS, dtype=f32)))
    t_real = L - n_pad
    idx = jnp.arange(L, dtype=f32) - n_pad
    real = idx >= 0
    diff = idx[:, None] - idx[None, :]
    intra = jnp.where((diff >= 0) & real[None, :], jnp.exp(log_gamma[:, None, None] * jnp.maximum(diff, 0.0)), 0.0)
    cross = jnp.where(real[:, None], jnp.exp(log_gamma[None, :] * (idx[:, None] + 1.0)), 0.0)
    into = jnp.where(real[:, None], jnp.exp(log_gamma[None, :] * (t_real - 1.0 - idx[:, None])), 0.0)
    cdec = jnp.exp(log_gamma * t_real)[None, :]
    return intra.astype(f32), _pad_lanes(cross), _pad_lanes(into), _pad_lanes(cdec)


def _rope_tables(pos):
    half = C_DK // 2
    inv = ROPE_BASE ** (-jnp.arange(half, dtype=f32) / half)
    ang = pos.astype(f32)[:, None] * inv[None, :]
    cos, sin = jnp.cos(ang), jnp.sin(ang)
    return jnp.concatenate([cos, cos], axis=-1), jnp.concatenate([-sin, sin], axis=-1)


def _trunk(x, mode, states, w):
    seq_rows, L, n_pad = mode["seq_rows"], mode["L"], mode["n_pad"]
    sample = states is not None
    proj = _norm_proj(x, w["g_mix0"], w["even_w_in"], EVEN_SPLITS, mode["tm"])
    st = (states["hist_ssd"], states["C"], states["n"], states["m"], states["S"]) if sample else None
    y, C, n, m, S, conv_tail = _even_mix(proj, st, w["even_consts"], seq_rows=seq_rows, L=L,
                                         n_sub=mode["n_sub"], n_pad=n_pad)
    x = _proj_res(x, y, w["even_w_out"], mode["tm"])
    x, ffn_tail0 = _ffn(x, states["hist_ffn0"] if sample else None, w["g_ffn0"], w["w_up0"], w["ffn_cw0"],
                        w["ffn_cb0"], w["w_dn0"], None, seq_rows=seq_rows, tt=mode["tt"], cw=mode["cw"])
    proj = _norm_proj(x, w["g_mix1"], w["odd_w_in"], ODD_SPLITS, mode["tm"])
    y, Sr = _odd_mix(proj, states["Sr"] if sample else None, mode["cos"], mode["sin"],
                     mode["ret_tables"] + (w["ret_norm_g"],), seq_rows=seq_rows, L=L, n_sub=mode["n_sub"])
    x = _proj_res(x, y, w["odd_w_out"], mode["tm"])
    x, ffn_tail1, yn = _ffn(x, states["hist_ffn1"] if sample else None, w["g_ffn1"], w["w_up1"], w["ffn_cw1"],
                            w["ffn_cb1"], w["w_dn1"], w["g_final"], seq_rows=seq_rows, tt=mode["tt"], cw=mode["cw"])
    return yn, C, n, m, S, conv_tail, Sr, ffn_tail0, ffn_tail1


def kernel(x_prompt, x_sample, state_mlstm_C, state_mlstm_n, state_mlstm_m, state_ssd_conv, state_ssd_h, state_ret_S, state_ffn_conv, norm_mix_g, norm_ffn_g, norm_final_g, even_w_in, mlstm_igate_b, mlstm_fgate_b, mlstm_norm_g, ssd_conv_w, ssd_conv_b, ssd_dt_bias, ssd_A_log, ssd_D, ssd_norm_g, even_w_out, odd_w_in, ret_norm_g, odd_w_out, ffn_w_up, ffn_conv_w, ffn_conv_b, ffn_w_down):
    bsz, seq, d = x_prompt.shape
    dbsz, dseq, _ = x_sample.shape
    n_pad = SEQ_PAD - dseq
    assert norm_mix_g.shape[0] == 2 and dseq == 4 and seq % CHUNK == 0

    row = lambda a: a.reshape(1, -1)
    gate_bias = jnp.concatenate([_pad_lanes(row(mlstm_igate_b[0])), _pad_lanes(row(mlstm_fgate_b[0])),
                                 _pad_lanes(row(ssd_dt_bias[0]))], axis=0)
    w = {
        "g_mix0": row(norm_mix_g[0]), "g_mix1": row(norm_mix_g[1]),
        "g_ffn0": row(norm_ffn_g[0]), "g_ffn1": row(norm_ffn_g[1]), "g_final": row(norm_final_g),
        "even_w_in": _even_w_in_cols(even_w_in[0]), "even_w_out": even_w_out[0].astype(bf16),
        "odd_w_in": odd_w_in[0].astype(bf16), "odd_w_out": odd_w_out[0].astype(bf16),
        "even_consts": (gate_bias, row(mlstm_norm_g[0]), ssd_conv_w[0], row(ssd_conv_b[0]),
                        _pad_lanes(row(ssd_A_log[0])), _pad_lanes(row(ssd_D[0])), row(ssd_norm_g[0])),
        "ret_norm_g": row(ret_norm_g[0]),
        "w_up0": ffn_w_up[0].astype(bf16), "w_up1": ffn_w_up[1].astype(bf16),
        "w_dn0": ffn_w_down[0].astype(bf16), "w_dn1": ffn_w_down[1].astype(bf16),
        "ffn_cw0": ffn_conv_w[0], "ffn_cw1": ffn_conv_w[1],
        "ffn_cb0": row(ffn_conv_b[0]), "ffn_cb1": row(ffn_conv_b[1]),
    }

    cos_p, sin_p = _rope_tables(jnp.arange(seq, dtype=jnp.int32))
    mode_p = dict(seq_rows=seq, L=CHUNK, n_pad=0, n_sub=1, tm=256, tt=256, cw=D_FF // 2,
                  cos=cos_p, sin=sin_p, ret_tables=_retention_tables(CHUNK, 0))
    yp, pC, pn, pm, pS, p_conv, pSr, p_f0, p_f1 = _trunk(x_prompt.reshape(bsz * seq, d), mode_p, None, w)

    xs = jnp.pad(x_sample, ((0, 0), (n_pad, 0), (0, 0))).reshape(dbsz * SEQ_PAD, d)
    pos_s = PAST_LEN + jnp.arange(SEQ_PAD, dtype=jnp.int32) - n_pad
    cos_s, sin_s = _rope_tables(pos_s)
    hist = lambda a: jnp.pad(a, ((0, 0), (n_pad - a.shape[1], dseq), (0, 0))).reshape(dbsz * SEQ_PAD, a.shape[-1])
    states = {
        "hist_ssd": hist(state_ssd_conv[0]),
        "C": state_mlstm_C[0],
        "n": jnp.pad(state_mlstm_n[0], ((0, 0), (0, SUBLANES - A_HEADS), (0, 0))),
        "m": _pad_lanes(state_mlstm_m[0])[:, None, :],
        "S": state_ssd_h[0],
        "Sr": state_ret_S[0],
        "hist_ffn0": hist(state_ffn_conv[0]), "hist_ffn1": hist(state_ffn_conv[1]),
    }
    mode_s = dict(seq_rows=SEQ_PAD, L=SEQ_PAD, n_pad=n_pad, n_sub=4, tm=256, tt=128, cw=D_FF // 2,
                  cos=cos_s, sin=sin_s, ret_tables=_retention_tables(SEQ_PAD, n_pad))
    ys, sC, sn, sm, sS, s_conv, sSr, s_f0, s_f1 = _trunk(xs, mode_s, states, w)

    def pack(bs, y, C, n, m, S, conv_tail, Sr, f0, f1, rows_per_seq):
        conv = conv_tail.reshape(bs, SUBLANES, B_CONV_DIM)[:, SUBLANES - (B_CONV - 1):, :]
        ff = jnp.stack([f0.reshape(bs, SUBLANES, 2 * D_FF), f1.reshape(bs, SUBLANES, 2 * D_FF)])
        ff = ff[:, :, SUBLANES - (FFN_CONV - 1):, :]
        return (C[None], n[None, :, :A_HEADS, :], m[None, :, 0, :A_HEADS], conv[None], S[None], Sr[None], ff)

    p_states = pack(bsz, yp, pC, pn, pm, pS, p_conv, pSr, p_f0, p_f1, seq)
    s_states = pack(dbsz, ys, sC, sn, sm, sS, s_conv, sSr, s_f0, s_f1, SEQ_PAD)
    y_prompt = yp.reshape(bsz, seq, d)
    y_sample = ys.reshape(dbsz, SEQ_PAD, d)[:, n_pad:, :]
    return (y_prompt, y_sample) + p_states + s_states
```

```python
import functools

import numpy as np
import jax
import jax.numpy as jnp
from jax import lax
from jax.experimental import pallas as pl
from jax.experimental.pallas import tpu as pltpu

f32 = jnp.float32
bf16 = jnp.bfloat16

EPS = 1e-6
CHUNK = 128
D_MODEL = 1024
A_HEADS, A_DK, A_DV = 4, 128, 256
A_WIDTH = A_HEADS * A_DV
B_HEADS, B_HEADDIM, B_GROUPS, B_STATE, B_CONV = 16, 64, 2, 128, 4
B_WIDTH = B_HEADS * B_HEADDIM
B_CONV_DIM = B_WIDTH + 2 * B_GROUPS * B_STATE
B_HEADS_PER_GROUP = B_HEADS // B_GROUPS
B_GROUP_WIDTH = B_WIDTH // B_GROUPS
C_HEADS, C_DK, C_DV = 8, 128, 256
C_WIDTH = C_HEADS * C_DV
ROPE_BASE = 10000.0
D_FF = 2816
FFN_CONV = 3
PAST_LEN = 16384

LANES = 128
SUBLANES = 8
SEQ_PAD = SUBLANES
NEG_BIG = -1e30
VMEM_LIMIT = 56 * 1024 * 1024

EVEN_SPLITS = (A_HEADS * A_DK, A_HEADS * A_DK, A_WIDTH, A_WIDTH, B_WIDTH, B_CONV_DIM, 3 * LANES)
ODD_SPLITS = (C_HEADS * C_DK, C_HEADS * C_DK, C_WIDTH, C_WIDTH)


def _params(sem):
    return pltpu.CompilerParams(dimension_semantics=sem, vmem_limit_bytes=VMEM_LIMIT)


def _resident(shape, index_map):
    return pl.BlockSpec(shape, index_map, pipeline_mode=pl.Buffered(1))


def _nt(a, b):
    return lax.dot_general(a, b, (((1,), (1,)), ((), ())), preferred_element_type=f32)


def _tn(a, b):
    return lax.dot_general(a, b, (((0,), (0,)), ((), ())), preferred_element_type=f32)


def _nn(a, b):
    return jnp.dot(a, b, preferred_element_type=f32)


def _rmsnorm(x, g):
    return x * lax.rsqrt(jnp.mean(x * x, axis=-1, keepdims=True) + EPS) * g


def _layernorm_nogain(h):
    mu = jnp.mean(h, axis=-1, keepdims=True)
    hc = h - mu
    return hc * lax.rsqrt(jnp.mean(hc * hc, axis=-1, keepdims=True) + EPS)


def _softplus(x):
    return jnp.maximum(x, 0.0) + jnp.log1p(jnp.exp(-jnp.abs(x)))


def _silu(x):
    return x * jax.nn.sigmoid(x)


def _split3(x):
    hi = x.astype(bf16)
    r1 = x - hi.astype(f32)
    mid = r1.astype(bf16)
    lo = (r1 - mid.astype(f32)).astype(bf16)
    return hi, mid, lo


def _cumsum_rows(tril, x):
    hi, mid, lo = _split3(x)
    return _nn(tril, hi) + _nn(tril, mid) + _nn(tril, lo)


def _expand_lanes(x, onehot):
    hi, mid, lo = _split3(x)
    return _nn(hi, onehot) + _nn(mid, onehot) + _nn(lo, onehot)


def _split_views(ref, splits):
    views, off = [], 0
    for n in splits:
        views.append(ref.at[:, off:off + n])
        off += n
    return views


def _project(h, w_ref, p_ref, splits):
    off = 0
    for n in splits:
        p_ref[:, off:off + n] = _nn(h, w_ref[:, off:off + n])
        off += n


def _norm_proj_body(x_ref, g_ref, w_ref, *o_refs, splits):
    h = _rmsnorm(x_ref[...], g_ref[...]).astype(bf16)
    off = 0
    for o_ref, n in zip(o_refs, splits):
        o_ref[...] = _nn(h, w_ref[:, off:off + n])
        off += n


def _norm_proj(x, g, w, splits, tm):
    rows, d = x.shape
    return pl.pallas_call(
        functools.partial(_norm_proj_body, splits=splits),
        grid=(rows // tm,),
        in_specs=[pl.BlockSpec((tm, d), lambda i: (i, 0)),
                  pl.BlockSpec((1, d), lambda i: (0, 0)),
                  _resident(w.shape, lambda i: (0, 0))],
        out_specs=[pl.BlockSpec((tm, n), lambda i: (i, 0)) for n in splits],
        out_shape=[jax.ShapeDtypeStruct((rows, n), f32) for n in splits],
        compiler_params=_params(("arbitrary",)),
        name="norm_proj",
    )(x, g, w)


def _proj_res_body(x_ref, y_ref, w_ref, o_ref):
    o_ref[...] = x_ref[...] + _nn(y_ref[...].astype(bf16), w_ref[...])


def _proj_res(x, y, w, tm):
    rows, d = x.shape
    k = y.shape[1]
    return pl.pallas_call(
        _proj_res_body,
        grid=(rows // tm,),
        in_specs=[pl.BlockSpec((tm, d), lambda i: (i, 0)),
                  pl.BlockSpec((tm, k), lambda i: (i, 0)),
                  _resident(w.shape, lambda i: (0, 0))],
        out_specs=pl.BlockSpec((tm, d), lambda i: (i, 0)),
        out_shape=jax.ShapeDtypeStruct((rows, d), f32),
        compiler_params=_params(("arbitrary",)),
        name="proj_res",
    )(x, y, w)


def _ffn_body(*refs, tt, cw, n_pad, sample, final):
    it = iter(refs)
    x_ref = next(it)
    buf_ref = next(it) if sample else None
    g_ref, wup_ref, cw_ref, cb_ref, wdn_ref = next(it), next(it), next(it), next(it), next(it)
    gf_ref = next(it) if final else None
    o_ref, tail_ref = next(it), next(it)
    yn_ref = next(it) if final else None
    scr = next(it)
    carry = None if sample else next(it)

    if not sample:
        @pl.when(pl.program_id(1) == 0)
        def _():
            carry[...] = jnp.zeros(carry.shape, f32)
    else:
        row = lax.broadcasted_iota(jnp.int32, (tt, 1), 0) % SEQ_PAD
        is_hist = (row >= n_pad - (FFN_CONV - 1)) & (row < n_pad)

    x = x_ref[...]
    h = _rmsnorm(x, g_ref[...]).astype(bf16)
    acc = jnp.zeros((tt, D_MODEL), f32)
    for j in range(D_FF // cw):
        conv = []
        for part in range(2):
            c0 = part * D_FF + j * cw
            u = _nn(h, wup_ref[:, c0:c0 + cw])
            if sample:
                u = jnp.where(is_hist, buf_ref[:, c0:c0 + cw], u)
                scr[0:SUBLANES, :] = jnp.zeros((SUBLANES, cw), f32)
                tail_ref[:, c0:c0 + cw] = u
            else:
                scr[0:SUBLANES, :] = carry[:, c0:c0 + cw]
                tail_ref[:, c0:c0 + cw] = u[tt - SUBLANES:tt, :]
                carry[:, c0:c0 + cw] = u[tt - SUBLANES:tt, :]
            scr[SUBLANES:SUBLANES + tt, :] = u
            y = cb_ref[:, c0:c0 + cw] + scr[SUBLANES - 2:SUBLANES - 2 + tt, :] * cw_ref[0:1, c0:c0 + cw]
            y = y + scr[SUBLANES - 1:SUBLANES - 1 + tt, :] * cw_ref[1:2, c0:c0 + cw]
            y = y + u * cw_ref[2:3, c0:c0 + cw]
            conv.append(y)
        act = (_silu(conv[0]) * conv[1]).astype(bf16)
        acc = acc + _nn(act, wdn_ref[j * cw:(j + 1) * cw, :])
    out = x + acc
    o_ref[...] = out
    if final:
        yn_ref[...] = _rmsnorm(out, gf_ref[...])


def _ffn(x, hist, g, wup, conv_w, conv_b, wdn, g_final, *, seq_rows, tt, cw, n_pad):
    rows, d = x.shape
    sample = hist is not None
    final = g_final is not None
    n_seq = rows // seq_rows
    if sample:
        grid = (rows // tt,)
        rmap = lambda i: (i, 0)
        cmap = lambda i: (0, 0)
        tail_spec = pl.BlockSpec((tt, 2 * D_FF), rmap)
        sem = ("arbitrary",)
    else:
        n_inner = seq_rows // tt
        grid = (n_seq, n_inner)
        rmap = lambda s, t: (s * n_inner + t, 0)
        cmap = lambda s, t: (0, 0)
        tail_spec = pl.BlockSpec((SUBLANES, 2 * D_FF), lambda s, t: (s, 0))
        sem = ("arbitrary", "arbitrary")
    in_specs = [pl.BlockSpec((tt, d), rmap)]
    args = [x]
    if sample:
        in_specs.append(pl.BlockSpec((tt, 2 * D_FF), rmap))
        args.append(hist)
    in_specs += [pl.BlockSpec((1, d), cmap), _resident(wup.shape, cmap), pl.BlockSpec(conv_w.shape, cmap),
                 pl.BlockSpec(conv_b.shape, cmap), _resident(wdn.shape, cmap)]
    args += [g, wup, conv_w, conv_b, wdn]
    out_specs = [pl.BlockSpec((tt, d), rmap), tail_spec]
    out_shape = [jax.ShapeDtypeStruct((rows, d), f32), jax.ShapeDtypeStruct((n_seq * SUBLANES, 2 * D_FF), f32)]
    if final:
        in_specs.append(pl.BlockSpec((1, d), cmap))
        args.append(g_final)
        out_specs.append(pl.BlockSpec((tt, d), rmap))
        out_shape.append(jax.ShapeDtypeStruct((rows, d), f32))
    scratch = [pltpu.VMEM((SUBLANES + tt, cw), f32)]
    if not sample:
        scratch.append(pltpu.VMEM((SUBLANES, 2 * D_FF), f32))
    return pl.pallas_call(
        functools.partial(_ffn_body, tt=tt, cw=cw, n_pad=n_pad, sample=sample, final=final),
        grid=grid, in_specs=in_specs, out_specs=out_specs, out_shape=out_shape,
        scratch_shapes=scratch, compiler_params=_params(sem),
        name="ffn_sample" if sample else "ffn_prompt",
    )(*args)


def _even_chunk(r0, slot, L, n_pad, pairs, proj_refs, hist_ref, const_refs, y_ref, state_refs, tail_ref,
                conv_scr, yb_scr):
    q_ref, k_ref, v_ref, og_ref, z_ref, xbc_ref, gt_ref = proj_refs
    gb_ref, anorm_ref, convw_ref, convb_ref, alog_ref, dx_ref, bnorm_ref, expand_ref = const_refs
    C_ref, n_ref, m_ref, S_ref = state_refs
    rows = pl.ds(r0, L)
    ri = lax.broadcasted_iota(jnp.int32, (L, L), 0)
    ci = lax.broadcasted_iota(jnp.int32, (L, L), 1)
    causal = ri >= ci
    tril = causal.astype(bf16)
    valid = None
    if n_pad:
        valid = lax.broadcasted_iota(jnp.int32, (L, 1), 0) >= n_pad

    gates = gt_ref[rows, :]
    li = gates[:, 0:LANES] + gb_ref[0:1, :]
    fpre = gates[:, LANES:2 * LANES] + gb_ref[1:2, :]
    lf = -_softplus(-fpre)
    dt = _softplus(gates[:, 2 * LANES:3 * LANES] + gb_ref[2:3, :])
    if n_pad:
        li = jnp.where(valid, li, NEG_BIG)
        lf = jnp.where(valid, lf, 0.0)
        dt = jnp.where(valid, dt, 0.0)
    a = dt * (-jnp.exp(alog_ref[...]))
    bcum = _cumsum_rows(tril, lf)
    acum = _cumsum_rows(tril, a)
    liT, bT, aT, dtT = li.T, bcum.T, acum.T, dt.T

    m_row = m_ref[slot]
    b_last = bcum[L - 1:L, :]
    log_g = b_last - bcum + li
    m_new = jnp.maximum(b_last + m_row, jnp.max(log_g, axis=0, keepdims=True))
    gfac = jnp.exp(log_g - m_new)
    cdecay = jnp.exp(b_last + m_row - m_new)
    m_ref[slot] = m_new
    for h in range(A_HEADS):
        bcol, brow, lirow = bcum[:, h:h + 1], bT[h:h + 1, :], liT[h:h + 1, :]
        m_h = m_row[:, h:h + 1]
        logw = jnp.where(causal, bcol - brow + lirow, -jnp.inf)
        log_prev = bcol + m_h
        m_t = jnp.maximum(log_prev, jnp.max(logw, axis=-1, keepdims=True))
        w_in = jnp.exp(logw - m_t)
        w_prev = jnp.exp(log_prev - m_t)
        qh = q_ref[rows, h * A_DK:(h + 1) * A_DK]
        kh = k_ref[rows, h * A_DK:(h + 1) * A_DK] * (A_DK ** -0.5)
        vb = v_ref[rows, h * A_DV:(h + 1) * A_DV].astype(bf16)
        qb = qh.astype(bf16)
        s = _nt(qb, kh.astype(bf16)) * w_in
        C = C_ref[slot, h]
        nrow = n_ref[slot, h:h + 1, :]
        num = _nn(s.astype(bf16), vb) + _nn(qb, C.astype(bf16)) * w_prev
        den = jnp.sum(s, axis=-1, keepdims=True) + jnp.sum(qh * nrow, axis=-1, keepdims=True) * w_prev
        hout = num / jnp.maximum(jnp.abs(den), jnp.exp(-m_t))
        kg = kh * gfac[:, h:h + 1]
        dec_h = cdecay[:, h:h + 1]
        C_ref[slot, h] = C * dec_h + _tn(kg.astype(bf16), vb)
        n_ref[slot, h:h + 1, :] = nrow * dec_h + jnp.sum(kg, axis=0, keepdims=True)
        ya = _layernorm_nogain(hout) * anorm_ref[:, h * A_DV:(h + 1) * A_DV]
        ya = ya * jax.nn.sigmoid(og_ref[rows, h * A_DV:(h + 1) * A_DV])
        y_ref[rows, h * A_DV:(h + 1) * A_DV] = ya.astype(y_ref.dtype)

    xraw = xbc_ref[rows, :]
    if hist_ref is not None:
        is_hist = (lax.broadcasted_iota(jnp.int32, (L, 1), 0) >= n_pad - (B_CONV - 1)) & jnp.logical_not(valid)
        xraw = jnp.where(is_hist, hist_ref[rows, :], xraw)
    conv_scr[SUBLANES:SUBLANES + L, :] = xraw
    xc = convb_ref[...] + conv_scr[SUBLANES - 3:SUBLANES - 3 + L, :] * convw_ref[0:1, :]
    xc = xc + conv_scr[SUBLANES - 2:SUBLANES - 2 + L, :] * convw_ref[1:2, :]
    xc = xc + conv_scr[SUBLANES - 1:SUBLANES - 1 + L, :] * convw_ref[2:3, :]
    xc = xc + xraw * convw_ref[3:4, :]
    new_tail = conv_scr[L:L + SUBLANES, :]
    tail_ref[pl.ds(slot * SUBLANES, SUBLANES), :] = new_tail
    conv_scr[0:SUBLANES, :] = new_tail
    xc = _silu(xc)

    a_last = acum[L - 1:L, :]
    wtile = jnp.exp(a_last - acum) * dt
    expa = jnp.exp(acum)

    def head_matrix(cb, h):
        acol, arow, dtrow = acum[:, h:h + 1], aT[h:h + 1, :], dtT[h:h + 1, :]
        decay = jnp.where(causal, jnp.exp(jnp.where(causal, acol - arow, 0.0)), 0.0)
        return (cb * decay * dtrow).astype(bf16)

    def group_bc(g):
        Bg = xc[:, B_WIDTH + g * B_STATE:B_WIDTH + (g + 1) * B_STATE].astype(bf16)
        c0 = B_WIDTH + B_GROUPS * B_STATE + g * B_STATE
        Cg = xc[:, c0:c0 + B_STATE].astype(bf16)
        return Bg, Cg, _nt(Cg, Bg)

    if pairs:
        onehot = expand_ref[...]
        expa_x = _expand_lanes(expa, onehot)
        w_x = _expand_lanes(wtile, onehot)
        dec_x = expa_x[L - 1:L, :]
        low_half = lax.broadcasted_iota(jnp.int32, (L, LANES), 1) < B_HEADDIM
        for g in range(B_GROUPS):
            Bg, Cg, cb = group_bc(g)
            gs = slice(g * B_GROUP_WIDTH, (g + 1) * B_GROUP_WIDTH)
            for jp in range(B_HEADS_PER_GROUP // 2):
                j = g * (B_HEADS_PER_GROUP // 2) + jp
                mcat = jnp.concatenate([head_matrix(cb, 2 * j), head_matrix(cb, 2 * j + 1)], axis=1)
                xp = xc[:, j * LANES:(j + 1) * LANES]
                xbd = jnp.concatenate([jnp.where(low_half, xp, 0.0).astype(bf16),
                                       jnp.where(low_half, 0.0, xp).astype(bf16)], axis=0)
                yb_scr[:, j * LANES:(j + 1) * LANES] = _nn(mcat, xbd)
            xg = xc[:, gs]
            ST = S_ref[slot, :, gs]
            yb_scr[:, gs] = (yb_scr[:, gs] + _nn(Cg, ST.astype(bf16)) * expa_x[:, gs]) + dx_ref[:, gs] * xg
            S_ref[slot, :, gs] = ST * dec_x[:, gs] + _tn(Bg, (xg * w_x[:, gs]).astype(bf16))
    else:
        dec_last = jnp.exp(a_last)
        for g in range(B_GROUPS):
            Bg, Cg, cb = group_bc(g)
            for e in range(B_HEADS_PER_GROUP):
                h = g * B_HEADS_PER_GROUP + e
                hs = slice(h * B_HEADDIM, (h + 1) * B_HEADDIM)
                xh = xc[:, hs]
                S = S_ref[slot, h]
                yh = _nn(head_matrix(cb, h), xh.astype(bf16)) + _nt(Cg, S.astype(bf16)) * expa[:, h:h + 1]
                S_ref[slot, h] = S * dec_last[:, h:h + 1] + _tn((xh * wtile[:, h:h + 1]).astype(bf16), Bg)
                yb_scr[:, hs] = yh + dx_ref[:, hs] * xh
    yb = yb_scr[...] * _silu(z_ref[rows, :])
    for g in range(B_GROUPS):
        gs = slice(g * B_GROUP_WIDTH, (g + 1) * B_GROUP_WIDTH)
        yg = yb[:, gs]
        yg = yg * lax.rsqrt(jnp.mean(yg * yg, axis=-1, keepdims=True) + EPS)
        y_ref[rows, A_WIDTH + g * B_GROUP_WIDTH:A_WIDTH + (g + 1) * B_GROUP_WIDTH] = (
            yg * bnorm_ref[:, gs]).astype(y_ref.dtype)


N_EVEN_CONSTS = 8


def _even_prompt_body(*refs, L, n_sub):
    it = iter(refs)
    x_ref, gmix_ref, win_ref = next(it), next(it), next(it)
    const_refs = tuple(next(it) for _ in range(N_EVEN_CONSTS))
    wout_ref = next(it)
    o_ref = next(it)
    state_refs = tuple(next(it) for _ in range(4))
    tail_ref = next(it)
    p_scr, y_scr, conv_scr, yb_scr = (next(it) for _ in range(4))

    @pl.when(pl.program_id(1) == 0)
    def _():
        conv_scr[0:SUBLANES, :] = jnp.zeros((SUBLANES, B_CONV_DIM), f32)
        for r in state_refs:
            r[...] = jnp.zeros(r.shape, f32)

    x = x_ref[...]
    _project(_rmsnorm(x, gmix_ref[...]).astype(bf16), win_ref, p_scr, EVEN_SPLITS)
    proj_refs = _split_views(p_scr, EVEN_SPLITS)
    for c in range(n_sub):
        _even_chunk(c * L, 0, L, 0, True, proj_refs, None, const_refs, y_scr, state_refs, tail_ref, conv_scr, yb_scr)
    o_ref[...] = x + _nn(y_scr[...], wout_ref[...])


def _even_prompt(x, gmix, win, consts, wout, *, seq_rows, L, n_sub):
    rows, d = x.shape
    n_seq = rows // seq_rows
    tt = L * n_sub
    n_inner = seq_rows // tt
    rmap = lambda s, t: (s * n_inner + t, 0)
    cmap = lambda s, t: (0, 0)
    smap4 = lambda s, t: (s, 0, 0, 0)
    smap3 = lambda s, t: (s, 0, 0)
    st_specs = [pl.BlockSpec((1, A_HEADS, A_DK, A_DV), smap4), pl.BlockSpec((1, SUBLANES, A_DK), smap3),
                pl.BlockSpec((1, 1, LANES), smap3), pl.BlockSpec((1, B_STATE, B_WIDTH), smap3)]
    st_shapes = [jax.ShapeDtypeStruct((n_seq, A_HEADS, A_DK, A_DV), f32),
                 jax.ShapeDtypeStruct((n_seq, SUBLANES, A_DK), f32),
                 jax.ShapeDtypeStruct((n_seq, 1, LANES), f32),
                 jax.ShapeDtypeStruct((n_seq, B_STATE, B_WIDTH), f32)]
    in_specs = ([pl.BlockSpec((tt, d), rmap), pl.BlockSpec((1, d), cmap), _resident(win.shape, cmap)]
                + [pl.BlockSpec(c.shape, cmap) for c in consts] + [_resident(wout.shape, cmap)])
    return pl.pallas_call(
        functools.partial(_even_prompt_body, L=L, n_sub=n_sub),
        grid=(n_seq, n_inner), in_specs=in_specs,
        out_specs=[pl.BlockSpec((tt, d), rmap)] + st_specs + [pl.BlockSpec((SUBLANES, B_CONV_DIM), lambda s, t: (s, 0))],
        out_shape=[jax.ShapeDtypeStruct((rows, d), f32)] + st_shapes
        + [jax.ShapeDtypeStruct((n_seq * SUBLANES, B_CONV_DIM), f32)],
        scratch_shapes=[pltpu.VMEM((tt, sum(EVEN_SPLITS)), f32), pltpu.VMEM((tt, A_WIDTH + B_WIDTH), bf16),
                        pltpu.VMEM((SUBLANES + L, B_CONV_DIM), f32), pltpu.VMEM((L, B_WIDTH), f32)],
        compiler_params=_params(("arbitrary", "arbitrary")),
        name="even_prompt",
    )(x, gmix, win, *consts, wout)


def _even_sample_body(*refs, L, n_sub, n_pad):
    it = iter(refs)
    proj_refs = tuple(next(it) for _ in range(7))
    hist_ref = next(it)
    state0_refs = tuple(next(it) for _ in range(4))
    const_refs = tuple(next(it) for _ in range(N_EVEN_CONSTS))
    y_ref = next(it)
    state_refs = tuple(next(it) for _ in range(4))
    tail_ref = next(it)
    conv_scr, yb_scr = next(it), next(it)

    conv_scr[0:SUBLANES, :] = jnp.zeros((SUBLANES, B_CONV_DIM), f32)
    for r, r0 in zip(state_refs, state0_refs):
        r[...] = r0[...]

    def step(c, carry):
        _even_chunk(pl.multiple_of(c * L, L), c, L, n_pad, False, proj_refs, hist_ref, const_refs, y_ref,
                    state_refs, tail_ref, conv_scr, yb_scr)
        return carry

    lax.fori_loop(0, n_sub, step, 0)


def _even_sample(proj, state, consts, *, n_sub, n_pad):
    rows = proj[0].shape[0]
    L = SEQ_PAD
    n_seq = rows // L
    tt = L * n_sub
    rmap = lambda o: (o, 0)
    cmap = lambda o: (0, 0)
    smap4 = lambda o: (o, 0, 0, 0)
    smap3 = lambda o: (o, 0, 0)
    st_specs = [pl.BlockSpec((n_sub, A_HEADS, A_DK, A_DV), smap4), pl.BlockSpec((n_sub, SUBLANES, A_DK), smap3),
                pl.BlockSpec((n_sub, 1, LANES), smap3), pl.BlockSpec((n_sub, B_HEADS, B_HEADDIM, B_STATE), smap4)]
    st_shapes = [jax.ShapeDtypeStruct(s.shape, f32) for s in state[1:]]
    in_specs = ([pl.BlockSpec((tt, a.shape[1]), rmap) for a in proj] + [pl.BlockSpec((tt, B_CONV_DIM), rmap)]
                + st_specs + [pl.BlockSpec(c.shape, cmap) for c in consts])
    return pl.pallas_call(
        functools.partial(_even_sample_body, L=L, n_sub=n_sub, n_pad=n_pad),
        grid=(n_seq // n_sub,), in_specs=in_specs,
        out_specs=[pl.BlockSpec((tt, A_WIDTH + B_WIDTH), rmap)] + st_specs + [pl.BlockSpec((tt, B_CONV_DIM), rmap)],
        out_shape=[jax.ShapeDtypeStruct((rows, A_WIDTH + B_WIDTH), f32)] + st_shapes
        + [jax.ShapeDtypeStruct((rows, B_CONV_DIM), f32)],
        scratch_shapes=[pltpu.VMEM((SUBLANES + L, B_CONV_DIM), f32), pltpu.VMEM((L, B_WIDTH), f32)],
        compiler_params=_params(("arbitrary",)),
        name="even_sample",
    )(*proj, *state, *consts)


def _odd_chunk(r0, rope_r0, slot, L, proj_refs, const_refs, y_ref, S_ref):
    q_ref, k_ref, v_ref, g_ref = proj_refs
    cos_ref, sin_ref, intra_ref, cross_ref, into_ref, cdec_ref, norm_ref = const_refs
    rows = pl.ds(r0, L)
    cosf = cos_ref[pl.ds(rope_r0, L), :]
    sinf = sin_ref[pl.ds(rope_r0, L), :]
    cross = cross_ref[...]
    into = into_ref[...]
    cdec = cdec_ref[...]
    for h in range(C_HEADS):
        qh = q_ref[rows, h * C_DK:(h + 1) * C_DK]
        kh = k_ref[rows, h * C_DK:(h + 1) * C_DK]
        qh = qh * cosf + pltpu.roll(qh, C_DK // 2, 1) * sinf
        kh = (kh * cosf + pltpu.roll(kh, C_DK // 2, 1) * sinf) * (C_DK ** -0.5)
        vb = v_ref[rows, h * C_DV:(h + 1) * C_DV].astype(bf16)
        qb = qh.astype(bf16)
        s = _nt(qb, kh.astype(bf16)) * intra_ref[h]
        S = S_ref[slot, h]
        o = _nn(s.astype(bf16), vb) + _nn(qb, S.astype(bf16)) * cross[:, h:h + 1]
        S_ref[slot, h] = S * cdec[:, h:h + 1] + _tn((kh * into[:, h:h + 1]).astype(bf16), vb)
        y = _layernorm_nogain(o) * norm_ref[:, h * C_DV:(h + 1) * C_DV]
        y = y * _silu(g_ref[rows, h * C_DV:(h + 1) * C_DV])
        y_ref[rows, h * C_DV:(h + 1) * C_DV] = y.astype(y_ref.dtype)


N_ODD_CONSTS = 7


def _odd_prompt_body(*refs, L, n_sub):
    it = iter(refs)
    x_ref, gmix_ref, win_ref = next(it), next(it), next(it)
    const_refs = tuple(next(it) for _ in range(N_ODD_CONSTS))
    wout_ref = next(it)
    o_ref, S_ref = next(it), next(it)
    p_scr, y_scr = next(it), next(it)

    @pl.when(pl.program_id(1) == 0)
    def _():
        S_ref[...] = jnp.zeros(S_ref.shape, f32)

    x = x_ref[...]
    _project(_rmsnorm(x, gmix_ref[...]).astype(bf16), win_ref, p_scr, ODD_SPLITS)
    proj_refs = _split_views(p_scr, ODD_SPLITS)
    for c in range(n_sub):
        _odd_chunk(c * L, c * L, 0, L, proj_refs, const_refs, y_scr, S_ref)
    o_ref[...] = x + _nn(y_scr[...], wout_ref[...])


def _odd_prompt(x, gmix, win, cos, sin, consts, wout, *, seq_rows, L, n_sub):
    rows, d = x.shape
    n_seq = rows // seq_rows
    tt = L * n_sub
    n_inner = seq_rows // tt
    rmap = lambda s, t: (s * n_inner + t, 0)
    cmap = lambda s, t: (0, 0)
    st_spec = pl.BlockSpec((1, C_HEADS, C_DK, C_DV), lambda s, t: (s, 0, 0, 0))
    rope_spec = pl.BlockSpec((tt, LANES), lambda s, t: (t, 0))
    in_specs = [pl.BlockSpec((tt, d), rmap), pl.BlockSpec((1, d), cmap), _resident(win.shape, cmap), rope_spec, rope_spec]
    for c in consts:
        in_specs.append(pl.BlockSpec(c.shape, (lambda s, t: (0, 0, 0)) if c.ndim == 3 else cmap))
    in_specs.append(_resident(wout.shape, cmap))
    return pl.pallas_call(
        functools.partial(_odd_prompt_body, L=L, n_sub=n_sub),
        grid=(n_seq, n_inner), in_specs=in_specs,
        out_specs=[pl.BlockSpec((tt, d), rmap), st_spec],
        out_shape=[jax.ShapeDtypeStruct((rows, d), f32), jax.ShapeDtypeStruct((n_seq, C_HEADS, C_DK, C_DV), f32)],
        scratch_shapes=[pltpu.VMEM((tt, sum(ODD_SPLITS)), f32), pltpu.VMEM((tt, C_WIDTH), bf16)],
        compiler_params=_params(("arbitrary", "arbitrary")),
        name="odd_prompt",
    )(x, gmix, win, cos, sin, *consts, wout)


def _odd_sample_body(*refs, L, n_sub):
    it = iter(refs)
    proj_refs = tuple(next(it) for _ in range(4))
    S0_ref = next(it)
    const_refs = tuple(next(it) for _ in range(N_ODD_CONSTS))
    y_ref, S_ref = next(it), next(it)
    S_ref[...] = S0_ref[...]

    def step(c, carry):
        _odd_chunk(pl.multiple_of(c * L, L), 0, c, L, proj_refs, const_refs, y_ref, S_ref)
        return carry

    lax.fori_loop(0, n_sub, step, 0)


def _odd_sample(proj, S0, cos, sin, consts, *, n_sub):
    rows = proj[0].shape[0]
    L = SEQ_PAD
    n_seq = rows // L
    tt = L * n_sub
    rmap = lambda o: (o, 0)
    cmap = lambda o: (0, 0)
    st_spec = pl.BlockSpec((n_sub, C_HEADS, C_DK, C_DV), lambda o: (o, 0, 0, 0))
    in_specs = [pl.BlockSpec((tt, a.shape[1]), rmap) for a in proj] + [st_spec]
    for c in (cos, sin) + tuple(consts):
        in_specs.append(pl.BlockSpec(c.shape, (lambda o: (0, 0, 0)) if c.ndim == 3 else cmap))
    return pl.pallas_call(
        functools.partial(_odd_sample_body, L=L, n_sub=n_sub),
        grid=(n_seq // n_sub,), in_specs=in_specs,
        out_specs=[pl.BlockSpec((tt, C_WIDTH), rmap), st_spec],
        out_shape=[jax.ShapeDtypeStruct((rows, C_WIDTH), f32), jax.ShapeDtypeStruct(S0.shape, f32)],
        compiler_params=_params(("arbitrary",)),
        name="odd_sample",
    )(*proj, S0, cos, sin, *consts)


def _pad_lanes(a, width=LANES):
    return jnp.pad(a, [(0, 0)] * (a.ndim - 1) + [(0, width - a.shape[-1])])


def _even_w_in_cols(w):
    sizes = [A_HEADS * A_DK, A_HEADS * A_DK, A_WIDTH, A_HEADS, A_HEADS, A_WIDTH, B_WIDTH, B_CONV_DIM, B_HEADS]
    q, k, v, ig, fg, og, z, xbc, dt = jnp.split(w, np.cumsum(sizes)[:-1].tolist(), axis=-1)
    return jnp.concatenate([q, k, v, og, z, xbc, _pad_lanes(ig), _pad_lanes(fg), _pad_lanes(dt)], axis=-1).astype(bf16)


def _retention_tables(L, n_pad):
    log_gamma = jnp.log1p(-jnp.exp2(-5.0 - jnp.arange(C_HEADS, dtype=f32)))
    t_real = L - n_pad
    idx = jnp.arange(L, dtype=f32) - n_pad
    real = idx >= 0
    diff = idx[:, None] - idx[None, :]
    intra = jnp.where((diff >= 0) & real[None, :], jnp.exp(log_gamma[:, None, None] * jnp.maximum(diff, 0.0)), 0.0)
    cross = jnp.where(real[:, None], jnp.exp(log_gamma[None, :] * (idx[:, None] + 1.0)), 0.0)
    into = jnp.where(real[:, None], jnp.exp(log_gamma[None, :] * (t_real - 1.0 - idx[:, None])), 0.0)
    cdec = jnp.exp(log_gamma * t_real)[None, :]
    return intra.astype(f32), _pad_lanes(cross), _pad_lanes(into), _pad_lanes(cdec)


def _rope_tables(pos):
    half = C_DK // 2
    inv = ROPE_BASE ** (-jnp.arange(half, dtype=f32) / half)
    ang = pos.astype(f32)[:, None] * inv[None, :]
    cos, sin = jnp.cos(ang), jnp.sin(ang)
    return jnp.concatenate([cos, cos], axis=-1), jnp.concatenate([-sin, sin], axis=-1)


def kernel(x_prompt, x_sample, state_mlstm_C, state_mlstm_n, state_mlstm_m, state_ssd_conv, state_ssd_h, state_ret_S, state_ffn_conv, norm_mix_g, norm_ffn_g, norm_final_g, even_w_in, mlstm_igate_b, mlstm_fgate_b, mlstm_norm_g, ssd_conv_w, ssd_conv_b, ssd_dt_bias, ssd_A_log, ssd_D, ssd_norm_g, even_w_out, odd_w_in, ret_norm_g, odd_w_out, ffn_w_up, ffn_conv_w, ffn_conv_b, ffn_w_down):
    bsz, seq, d = x_prompt.shape
    dbsz, dseq, _ = x_sample.shape
    n_pad = SEQ_PAD - dseq
    assert norm_mix_g.shape[0] == 2 and B_CONV - 1 <= n_pad < SEQ_PAD and seq % (2 * CHUNK) == 0

    row = lambda a: a.reshape(1, -1)
    gate_bias = jnp.concatenate([_pad_lanes(row(mlstm_igate_b[0])), _pad_lanes(row(mlstm_fgate_b[0])),
                                 _pad_lanes(row(ssd_dt_bias[0]))], axis=0)
    head_of_lane = jnp.arange(B_WIDTH, dtype=jnp.int32) // B_HEADDIM
    onehot = (jnp.arange(LANES, dtype=jnp.int32)[:, None] == head_of_lane[None, :]).astype(bf16)
    even_consts = (gate_bias, row(mlstm_norm_g[0]), ssd_conv_w[0], row(ssd_conv_b[0]), _pad_lanes(row(ssd_A_log[0])),
                   row(jnp.repeat(ssd_D[0], B_HEADDIM)), row(ssd_norm_g[0]), onehot)
    g_mix0, g_mix1 = row(norm_mix_g[0]), row(norm_mix_g[1])
    g_ffn0, g_ffn1, g_final = row(norm_ffn_g[0]), row(norm_ffn_g[1]), row(norm_final_g)
    w_even_in, w_even_out = _even_w_in_cols(even_w_in[0]), even_w_out[0].astype(bf16)
    w_odd_in, w_odd_out = odd_w_in[0].astype(bf16), odd_w_out[0].astype(bf16)
    w_up, w_dn = ffn_w_up.astype(bf16), ffn_w_down.astype(bf16)
    ffn_cb = ffn_conv_b[:, None, :]
    ret_norm = row(ret_norm_g[0])
    cw = D_FF // 2

    cos_p, sin_p = _rope_tables(jnp.arange(seq, dtype=jnp.int32))
    xp = x_prompt.reshape(bsz * seq, d)
    xp, pC, pn, pm, pST, p_conv = _even_prompt(xp, g_mix0, w_even_in, even_consts, w_even_out,
                                               seq_rows=seq, L=CHUNK, n_sub=2)
    xp, p_f0 = _ffn(xp, None, g_ffn0, w_up[0], ffn_conv_w[0], ffn_cb[0], w_dn[0], None,
                    seq_rows=seq, tt=512, cw=cw, n_pad=0)
    xp, pSr = _odd_prompt(xp, g_mix1, w_odd_in, cos_p, sin_p, _retention_tables(CHUNK, 0) + (ret_norm,), w_odd_out,
                          seq_rows=seq, L=CHUNK, n_sub=2)
    _, p_f1, yp = _ffn(xp, None, g_ffn1, w_up[1], ffn_conv_w[1], ffn_cb[1], w_dn[1], g_final,
                       seq_rows=seq, tt=512, cw=cw, n_pad=0)
    pS = pST.reshape(bsz, B_STATE, B_HEADS, B_HEADDIM).transpose(0, 2, 3, 1)

    xs = jnp.pad(x_sample, ((0, 0), (n_pad, 0), (0, 0))).reshape(dbsz * SEQ_PAD, d)
    cos_s, sin_s = _rope_tables(PAST_LEN + jnp.arange(SEQ_PAD, dtype=jnp.int32) - n_pad)
    hist = lambda a: jnp.pad(a, ((0, 0), (n_pad - a.shape[1], dseq), (0, 0))).reshape(dbsz * SEQ_PAD, a.shape[-1])
    state = (hist(state_ssd_conv[0]), state_mlstm_C[0],
             jnp.pad(state_mlstm_n[0], ((0, 0), (0, SUBLANES - A_HEADS), (0, 0))),
             _pad_lanes(state_mlstm_m[0])[:, None, :], state_ssd_h[0])
    tm = 256
    proj = _norm_proj(xs, g_mix0, w_even_in, EVEN_SPLITS, tm)
    y, sC, sn, sm, sS, s_conv = _even_sample(proj, state, even_consts, n_sub=4, n_pad=n_pad)
    xs = _proj_res(xs, y, w_even_out, tm)
    xs, s_f0 = _ffn(xs, hist(state_ffn_conv[0]), g_ffn0, w_up[0], ffn_conv_w[0], ffn_cb[0], w_dn[0], None,
                    seq_rows=SEQ_PAD, tt=128, cw=cw, n_pad=n_pad)
    proj = _norm_proj(xs, g_mix1, w_odd_in, ODD_SPLITS, tm)
    y, sSr = _odd_sample(proj, state_ret_S[0], cos_s, sin_s, _retention_tables(SEQ_PAD, n_pad) + (ret_norm,), n_sub=4)
    xs = _proj_res(xs, y, w_odd_out, tm)
    _, s_f1, ys = _ffn(xs, hist(state_ffn_conv[1]), g_ffn1, w_up[1], ffn_conv_w[1], ffn_cb[1], w_dn[1], g_final,
                       seq_rows=SEQ_PAD, tt=128, cw=cw, n_pad=n_pad)

    def pack(bs, C, n, m, S, conv_tail, Sr, f0, f1):
        conv = conv_tail.reshape(bs, SUBLANES, B_CONV_DIM)[:, SUBLANES - (B_CONV - 1):, :]
        ff = jnp.stack([f0.reshape(bs, SUBLANES, 2 * D_FF), f1.reshape(bs, SUBLANES, 2 * D_FF)])
        ff = ff[:, :, SUBLANES - (FFN_CONV - 1):, :]
        return (C[None], n[None, :, :A_HEADS, :], m[None, :, 0, :A_HEADS], conv[None], S[None], Sr[None], ff)

    y_prompt = yp.reshape(bsz, seq, d)
    y_sample = ys.reshape(dbsz, SEQ_PAD, d)[:, n_pad:, :]
    return ((y_prompt, y_sample) + pack(bsz, pC, pn, pm, pS, p_conv, pSr, p_f0, p_f1)
            + pack(dbsz, sC, sn, sm, sS, s_conv, sSr, s_f0, s_f1))
```

```python
import functools

import numpy as np
import jax
import jax.numpy as jnp
from jax import lax
from jax.experimental import pallas as pl
from jax.experimental.pallas import tpu as pltpu

f32 = jnp.float32
bf16 = jnp.bfloat16

EPS = 1e-6
CHUNK = 128
D_MODEL = 1024
A_HEADS, A_DK, A_DV = 4, 128, 256
A_WIDTH = A_HEADS * A_DV
B_HEADS, B_HEADDIM, B_GROUPS, B_STATE, B_CONV = 16, 64, 2, 128, 4
B_WIDTH = B_HEADS * B_HEADDIM
B_CONV_DIM = B_WIDTH + 2 * B_GROUPS * B_STATE
B_HEADS_PER_GROUP = B_HEADS // B_GROUPS
B_GROUP_WIDTH = B_WIDTH // B_GROUPS
C_HEADS, C_DK, C_DV = 8, 128, 256
C_WIDTH = C_HEADS * C_DV
ROPE_BASE = 10000.0
D_FF = 2816
FFN_CONV = 3
PAST_LEN = 16384

LANES = 128
SUBLANES = 8
SEQ_PAD = SUBLANES
NEG_BIG = -1e30
VMEM_LIMIT = 56 * 1024 * 1024

EVEN_SPLITS = (A_HEADS * A_DK, A_HEADS * A_DK, A_WIDTH, A_WIDTH, B_WIDTH, B_CONV_DIM, 3 * LANES)
ODD_SPLITS = (C_HEADS * C_DK, C_HEADS * C_DK, C_WIDTH, C_WIDTH)


def _params(sem):
    return pltpu.CompilerParams(dimension_semantics=sem, vmem_limit_bytes=VMEM_LIMIT)


def _resident(shape, index_map):
    return pl.BlockSpec(shape, index_map, pipeline_mode=pl.Buffered(1))


def _nt(a, b):
    return lax.dot_general(a, b, (((1,), (1,)), ((), ())), preferred_element_type=f32)


def _tn(a, b):
    return lax.dot_general(a, b, (((0,), (0,)), ((), ())), preferred_element_type=f32)


def _nn(a, b):
    return jnp.dot(a, b, preferred_element_type=f32)


def _rmsnorm(x, g):
    return x * lax.rsqrt(jnp.mean(x * x, axis=-1, keepdims=True) + EPS) * g


def _layernorm_nogain(h):
    mu = jnp.mean(h, axis=-1, keepdims=True)
    hc = h - mu
    return hc * lax.rsqrt(jnp.mean(hc * hc, axis=-1, keepdims=True) + EPS)


def _softplus(x):
    return jnp.maximum(x, 0.0) + jnp.log1p(jnp.exp(-jnp.abs(x)))


def _silu(x):
    return x * jax.nn.sigmoid(x)


def _split3(x):
    hi = x.astype(bf16)
    r1 = x - hi.astype(f32)
    mid = r1.astype(bf16)
    lo = (r1 - mid.astype(f32)).astype(bf16)
    return hi, mid, lo


def _cumsum_rows(tril, x):
    hi, mid, lo = _split3(x)
    return _nn(tril, hi) + _nn(tril, mid) + _nn(tril, lo)


def _expand_lanes(x, onehot):
    hi, mid, lo = _split3(x)
    return _nn(hi, onehot) + _nn(mid, onehot) + _nn(lo, onehot)


def _split_views(ref, splits):
    views, off = [], 0
    for n in splits:
        views.append(ref.at[:, off:off + n])
        off += n
    return views


def _project(h, w_ref, p_ref, splits):
    off = 0
    for n in splits:
        p_ref[:, off:off + n] = _nn(h, w_ref[:, off:off + n])
        off += n


MXU_COLS = 256


def _proj_pieces(h_ref, w_ref, dst_ref):
    def piece(c0, n):
        def run():
            dst_ref[:, c0:c0 + n] = _nn(h_ref[...], w_ref[:, c0:c0 + n])
        return run
    total = w_ref.shape[1]
    return [piece(c0, min(MXU_COLS, total - c0)) for c0 in range(0, total, MXU_COLS)]


def _out_pieces(x_ref, y_ref, w_ref, o_ref, rows):
    def piece(c0, n):
        def run():
            o_ref[rows, c0:c0 + n] = x_ref[rows, c0:c0 + n] + _nn(y_ref[...], w_ref[:, c0:c0 + n])
        return run
    total = w_ref.shape[1]
    return [piece(c0, min(MXU_COLS, total - c0)) for c0 in range(0, total, MXU_COLS)]


def _spread(work, n_slots):
    return [work[i * len(work) // n_slots:(i + 1) * len(work) // n_slots] for i in range(n_slots)]


def _norm_proj_body(x_ref, g_ref, w_ref, *o_refs, splits):
    h = _rmsnorm(x_ref[...], g_ref[...]).astype(bf16)
    off = 0
    for o_ref, n in zip(o_refs, splits):
        o_ref[...] = _nn(h, w_ref[:, off:off + n])
        off += n


def _norm_proj(x, g, w, splits, tm):
    rows, d = x.shape
    return pl.pallas_call(
        functools.partial(_norm_proj_body, splits=splits),
        grid=(rows // tm,),
        in_specs=[pl.BlockSpec((tm, d), lambda i: (i, 0)),
                  pl.BlockSpec((1, d), lambda i: (0, 0)),
                  _resident(w.shape, lambda i: (0, 0))],
        out_specs=[pl.BlockSpec((tm, n), lambda i: (i, 0)) for n in splits],
        out_shape=[jax.ShapeDtypeStruct((rows, n), f32) for n in splits],
        compiler_params=_params(("arbitrary",)),
        name="norm_proj",
    )(x, g, w)


def _proj_res_body(x_ref, y_ref, w_ref, o_ref):
    o_ref[...] = x_ref[...] + _nn(y_ref[...].astype(bf16), w_ref[...])


def _proj_res(x, y, w, tm):
    rows, d = x.shape
    k = y.shape[1]
    return pl.pallas_call(
        _proj_res_body,
        grid=(rows // tm,),
        in_specs=[pl.BlockSpec((tm, d), lambda i: (i, 0)),
                  pl.BlockSpec((tm, k), lambda i: (i, 0)),
                  _resident(w.shape, lambda i: (0, 0))],
        out_specs=pl.BlockSpec((tm, d), lambda i: (i, 0)),
        out_shape=jax.ShapeDtypeStruct((rows, d), f32),
        compiler_params=_params(("arbitrary",)),
        name="proj_res",
    )(x, y, w)


def _ffn_body(*refs, tt, cw, n_pad, sample, final):
    it = iter(refs)
    x_ref = next(it)
    buf_ref = next(it) if sample else None
    g_ref, wup_ref, cw_ref, cb_ref, wdn_ref = next(it), next(it), next(it), next(it), next(it)
    gf_ref = next(it) if final else None
    o_ref, tail_ref = next(it), next(it)
    yn_ref = next(it) if final else None
    scr = next(it)
    carry = None if sample else next(it)

    if not sample:
        @pl.when(pl.program_id(1) == 0)
        def _():
            carry[...] = jnp.zeros(carry.shape, f32)
    else:
        row = lax.broadcasted_iota(jnp.int32, (tt, 1), 0) % SEQ_PAD
        is_hist = (row >= n_pad - (FFN_CONV - 1)) & (row < n_pad)

    x = x_ref[...]
    h = _rmsnorm(x, g_ref[...]).astype(bf16)
    acc = jnp.zeros((tt, D_MODEL), f32)
    for j in range(D_FF // cw):
        conv = []
        for part in range(2):
            c0 = part * D_FF + j * cw
            u = _nn(h, wup_ref[:, c0:c0 + cw])
            if sample:
                u = jnp.where(is_hist, buf_ref[:, c0:c0 + cw], u)
                scr[0:SUBLANES, :] = jnp.zeros((SUBLANES, cw), f32)
                tail_ref[:, c0:c0 + cw] = u
            else:
                scr[0:SUBLANES, :] = carry[:, c0:c0 + cw]
                tail_ref[:, c0:c0 + cw] = u[tt - SUBLANES:tt, :]
                carry[:, c0:c0 + cw] = u[tt - SUBLANES:tt, :]
            scr[SUBLANES:SUBLANES + tt, :] = u
            y = cb_ref[:, c0:c0 + cw] + scr[SUBLANES - 2:SUBLANES - 2 + tt, :] * cw_ref[0:1, c0:c0 + cw]
            y = y + scr[SUBLANES - 1:SUBLANES - 1 + tt, :] * cw_ref[1:2, c0:c0 + cw]
            y = y + u * cw_ref[2:3, c0:c0 + cw]
            conv.append(y)
        act = (_silu(conv[0]) * conv[1]).astype(bf16)
        acc = acc + _nn(act, wdn_ref[j * cw:(j + 1) * cw, :])
    out = x + acc
    o_ref[...] = out
    if final:
        yn_ref[...] = _rmsnorm(out, gf_ref[...])


def _ffn(x, hist, g, wup, conv_w, conv_b, wdn, g_final, *, seq_rows, tt, cw, n_pad):
    rows, d = x.shape
    sample = hist is not None
    final = g_final is not None
    n_seq = rows // seq_rows
    if sample:
        grid = (rows // tt,)
        rmap = lambda i: (i, 0)
        cmap = lambda i: (0, 0)
        tail_spec = pl.BlockSpec((tt, 2 * D_FF), rmap)
        sem = ("arbitrary",)
    else:
        n_inner = seq_rows // tt
        grid = (n_seq, n_inner)
        rmap = lambda s, t: (s * n_inner + t, 0)
        cmap = lambda s, t: (0, 0)
        tail_spec = pl.BlockSpec((SUBLANES, 2 * D_FF), lambda s, t: (s, 0))
        sem = ("arbitrary", "arbitrary")
    in_specs = [pl.BlockSpec((tt, d), rmap)]
    args = [x]
    if sample:
        in_specs.append(pl.BlockSpec((tt, 2 * D_FF), rmap))
        args.append(hist)
    in_specs += [pl.BlockSpec((1, d), cmap), _resident(wup.shape, cmap), pl.BlockSpec(conv_w.shape, cmap),
                 pl.BlockSpec(conv_b.shape, cmap), _resident(wdn.shape, cmap)]
    args += [g, wup, conv_w, conv_b, wdn]
    out_specs = [pl.BlockSpec((tt, d), rmap), tail_spec]
    out_shape = [jax.ShapeDtypeStruct((rows, d), f32), jax.ShapeDtypeStruct((n_seq * SUBLANES, 2 * D_FF), f32)]
    if final:
        in_specs.append(pl.BlockSpec((1, d), cmap))
        args.append(g_final)
        out_specs.append(pl.BlockSpec((tt, d), rmap))
        out_shape.append(jax.ShapeDtypeStruct((rows, d), f32))
    scratch = [pltpu.VMEM((SUBLANES + tt, cw), f32)]
    if not sample:
        scratch.append(pltpu.VMEM((SUBLANES, 2 * D_FF), f32))
    return pl.pallas_call(
        functools.partial(_ffn_body, tt=tt, cw=cw, n_pad=n_pad, sample=sample, final=final),
        grid=grid, in_specs=in_specs, out_specs=out_specs, out_shape=out_shape,
        scratch_shapes=scratch, compiler_params=_params(sem),
        name="ffn_sample" if sample else "ffn_prompt",
    )(*args)


EVEN_SLOTS = A_HEADS + B_HEADS // 2


def _even_chunk(r0, slot, L, n_pad, pairs, proj_refs, hist_ref, const_refs, y_ref, state_refs, tail_ref,
                conv_scr, yb_scr, side=None):
    q_ref, k_ref, v_ref, og_ref, z_ref, xbc_ref, gt_ref = proj_refs
    gb_ref, anorm_ref, convw_ref, convb_ref, alog_ref, dx_ref, bnorm_ref, expand_ref = const_refs
    C_ref, n_ref, m_ref, S_ref = state_refs
    rows = pl.ds(r0, L)
    ri = lax.broadcasted_iota(jnp.int32, (L, L), 0)
    ci = lax.broadcasted_iota(jnp.int32, (L, L), 1)
    causal = ri >= ci
    tril = causal.astype(bf16)
    valid = None
    if n_pad:
        valid = lax.broadcasted_iota(jnp.int32, (L, 1), 0) >= n_pad

    gates = gt_ref[rows, :]
    li = gates[:, 0:LANES] + gb_ref[0:1, :]
    fpre = gates[:, LANES:2 * LANES] + gb_ref[1:2, :]
    lf = -_softplus(-fpre)
    dt = _softplus(gates[:, 2 * LANES:3 * LANES] + gb_ref[2:3, :])
    if n_pad:
        li = jnp.where(valid, li, NEG_BIG)
        lf = jnp.where(valid, lf, 0.0)
        dt = jnp.where(valid, dt, 0.0)
    a = dt * (-jnp.exp(alog_ref[...]))
    bcum = _cumsum_rows(tril, lf)
    acum = _cumsum_rows(tril, a)
    liT, bT, aT, dtT = li.T, bcum.T, acum.T, dt.T

    m_row = m_ref[slot]
    b_last = bcum[L - 1:L, :]
    log_g = b_last - bcum + li
    m_new = jnp.maximum(b_last + m_row, jnp.max(log_g, axis=0, keepdims=True))
    gfac = jnp.exp(log_g - m_new)
    cdecay = jnp.exp(b_last + m_row - m_new)
    m_ref[slot] = m_new
    for h in range(A_HEADS):
        for run in (side[h] if side else ()):
            run()
        bcol, brow, lirow = bcum[:, h:h + 1], bT[h:h + 1, :], liT[h:h + 1, :]
        m_h = m_row[:, h:h + 1]
        logw = jnp.where(causal, bcol - brow + lirow, -jnp.inf)
        log_prev = bcol + m_h
        m_t = jnp.maximum(log_prev, jnp.max(logw, axis=-1, keepdims=True))
        w_in = jnp.exp(logw - m_t)
        w_prev = jnp.exp(log_prev - m_t)
        qh = q_ref[rows, h * A_DK:(h + 1) * A_DK]
        kh = k_ref[rows, h * A_DK:(h + 1) * A_DK] * (A_DK ** -0.5)
        vb = v_ref[rows, h * A_DV:(h + 1) * A_DV].astype(bf16)
        qb = qh.astype(bf16)
        s = _nt(qb, kh.astype(bf16)) * w_in
        C = C_ref[slot, h]
        nrow = n_ref[slot, h:h + 1, :]
        num = _nn(s.astype(bf16), vb) + _nn(qb, C.astype(bf16)) * w_prev
        den = jnp.sum(s, axis=-1, keepdims=True) + jnp.sum(qh * nrow, axis=-1, keepdims=True) * w_prev
        hout = num / jnp.maximum(jnp.abs(den), jnp.exp(-m_t))
        kg = kh * gfac[:, h:h + 1]
        dec_h = cdecay[:, h:h + 1]
        C_ref[slot, h] = C * dec_h + _tn(kg.astype(bf16), vb)
        n_ref[slot, h:h + 1, :] = nrow * dec_h + jnp.sum(kg, axis=0, keepdims=True)
        ya = _layernorm_nogain(hout) * anorm_ref[:, h * A_DV:(h + 1) * A_DV]
        ya = ya * jax.nn.sigmoid(og_ref[rows, h * A_DV:(h + 1) * A_DV])
        y_ref[rows, h * A_DV:(h + 1) * A_DV] = ya.astype(y_ref.dtype)

    xraw = xbc_ref[rows, :]
    if hist_ref is not None:
        is_hist = (lax.broadcasted_iota(jnp.int32, (L, 1), 0) >= n_pad - (B_CONV - 1)) & jnp.logical_not(valid)
        xraw = jnp.where(is_hist, hist_ref[rows, :], xraw)
    conv_scr[SUBLANES:SUBLANES + L, :] = xraw
    xc = convb_ref[...] + conv_scr[SUBLANES - 3:SUBLANES - 3 + L, :] * convw_ref[0:1, :]
    xc = xc + conv_scr[SUBLANES - 2:SUBLANES - 2 + L, :] * convw_ref[1:2, :]
    xc = xc + conv_scr[SUBLANES - 1:SUBLANES - 1 + L, :] * convw_ref[2:3, :]
    xc = xc + xraw * convw_ref[3:4, :]
    new_tail = conv_scr[L:L + SUBLANES, :]
    tail_ref[pl.ds(slot * SUBLANES, SUBLANES), :] = new_tail
    conv_scr[0:SUBLANES, :] = new_tail
    xc = _silu(xc)

    a_last = acum[L - 1:L, :]
    wtile = jnp.exp(a_last - acum) * dt
    expa = jnp.exp(acum)

    def head_matrix(cb, h):
        acol, arow, dtrow = acum[:, h:h + 1], aT[h:h + 1, :], dtT[h:h + 1, :]
        decay = jnp.where(causal, jnp.exp(jnp.where(causal, acol - arow, 0.0)), 0.0)
        return (cb * decay * dtrow).astype(bf16)

    def group_bc(g):
        Bg = xc[:, B_WIDTH + g * B_STATE:B_WIDTH + (g + 1) * B_STATE].astype(bf16)
        c0 = B_WIDTH + B_GROUPS * B_STATE + g * B_STATE
        Cg = xc[:, c0:c0 + B_STATE].astype(bf16)
        return Bg, Cg, _nt(Cg, Bg)

    if pairs:
        onehot = expand_ref[...]
        expa_x = _expand_lanes(expa, onehot)
        w_x = _expand_lanes(wtile, onehot)
        dec_x = expa_x[L - 1:L, :]
        low_half = lax.broadcasted_iota(jnp.int32, (L, LANES), 1) < B_HEADDIM
        for g in range(B_GROUPS):
            Bg, Cg, cb = group_bc(g)
            gs = slice(g * B_GROUP_WIDTH, (g + 1) * B_GROUP_WIDTH)
            for jp in range(B_HEADS_PER_GROUP // 2):
                j = g * (B_HEADS_PER_GROUP // 2) + jp
                for run in (side[A_HEADS + j] if side else ()):
                    run()
                mcat = jnp.concatenate([head_matrix(cb, 2 * j), head_matrix(cb, 2 * j + 1)], axis=1)
                xp = xc[:, j * LANES:(j + 1) * LANES]
                xbd = jnp.concatenate([jnp.where(low_half, xp, 0.0).astype(bf16),
                                       jnp.where(low_half, 0.0, xp).astype(bf16)], axis=0)
                yb_scr[:, j * LANES:(j + 1) * LANES] = _nn(mcat, xbd)
            xg = xc[:, gs]
            ST = S_ref[slot, :, gs]
            yb_scr[:, gs] = (yb_scr[:, gs] + _nn(Cg, ST.astype(bf16)) * expa_x[:, gs]) + dx_ref[:, gs] * xg
            S_ref[slot, :, gs] = ST * dec_x[:, gs] + _tn(Bg, (xg * w_x[:, gs]).astype(bf16))
    else:
        dec_last = jnp.exp(a_last)
        for g in range(B_GROUPS):
            Bg, Cg, cb = group_bc(g)
            for e in range(B_HEADS_PER_GROUP):
                h = g * B_HEADS_PER_GROUP + e
                hs = slice(h * B_HEADDIM, (h + 1) * B_HEADDIM)
                xh = xc[:, hs]
                S = S_ref[slot, h]
                yh = _nn(head_matrix(cb, h), xh.astype(bf16)) + _nt(Cg, S.astype(bf16)) * expa[:, h:h + 1]
                S_ref[slot, h] = S * dec_last[:, h:h + 1] + _tn((xh * wtile[:, h:h + 1]).astype(bf16), Bg)
                yb_scr[:, hs] = yh + dx_ref[:, hs] * xh
    yb = yb_scr[...] * _silu(z_ref[rows, :])
    for g in range(B_GROUPS):
        gs = slice(g * B_GROUP_WIDTH, (g + 1) * B_GROUP_WIDTH)
        yg = yb[:, gs]
        yg = yg * lax.rsqrt(jnp.mean(yg * yg, axis=-1, keepdims=True) + EPS)
        y_ref[rows, A_WIDTH + g * B_GROUP_WIDTH:A_WIDTH + (g + 1) * B_GROUP_WIDTH] = (
            yg * bnorm_ref[:, gs]).astype(y_ref.dtype)


N_EVEN_CONSTS = 8


def _even_prompt_body(*refs, L, n_sub):
    it = iter(refs)
    x_ref, xnext_ref, gmix_ref, win_ref = next(it), next(it), next(it), next(it)
    const_refs = tuple(next(it) for _ in range(N_EVEN_CONSTS))
    wout_ref = next(it)
    o_ref = next(it)
    state_refs = tuple(next(it) for _ in range(4))
    tail_ref = next(it)
    p_scr, y_scr = (next(it), next(it)), (next(it), next(it))
    h_scr, conv_scr, yb_scr = (next(it) for _ in range(3))
    th = L * n_sub
    gmix = gmix_ref[...]

    @pl.when((pl.program_id(0) == 0) & (pl.program_id(1) == 0))
    def _():
        _project(_rmsnorm(x_ref[0:th, :], gmix).astype(bf16), win_ref, p_scr[0], EVEN_SPLITS)

    @pl.when(pl.program_id(1) == 0)
    def _():
        conv_scr[0:SUBLANES, :] = jnp.zeros((SUBLANES, B_CONV_DIM), f32)
        for r in state_refs:
            r[...] = jnp.zeros(r.shape, f32)

    for half in range(2):
        ahead = xnext_ref[...] if half else x_ref[th:2 * th, :]
        h_scr[...] = _rmsnorm(ahead, gmix).astype(bf16)
        work = _proj_pieces(h_scr, win_ref, p_scr[1 - half])
        if half:
            work = work + _out_pieces(x_ref, y_scr[0], wout_ref, o_ref, slice(0, th))
        side = _spread(work, n_sub * EVEN_SLOTS)
        proj_refs = _split_views(p_scr[half], EVEN_SPLITS)
        for c in range(n_sub):
            _even_chunk(c * L, 0, L, 0, True, proj_refs, None, const_refs, y_scr[half], state_refs, tail_ref,
                        conv_scr, yb_scr, side[c * EVEN_SLOTS:(c + 1) * EVEN_SLOTS])
    for run in _out_pieces(x_ref, y_scr[1], wout_ref, o_ref, slice(th, 2 * th)):
        run()


def _even_prompt(x, gmix, win, consts, wout, *, seq_rows, L, n_sub):
    rows, d = x.shape
    n_seq = rows // seq_rows
    th = L * n_sub
    tt = 2 * th
    n_inner = seq_rows // tt
    n_half = rows // th
    rmap = lambda s, t: (s * n_inner + t, 0)
    nextmap = lambda s, t: (jnp.minimum(2 * (s * n_inner + t) + 2, n_half - 1), 0)
    cmap = lambda s, t: (0, 0)
    smap4 = lambda s, t: (s, 0, 0, 0)
    smap3 = lambda s, t: (s, 0, 0)
    st_specs = [pl.BlockSpec((1, A_HEADS, A_DK, A_DV), smap4), pl.BlockSpec((1, SUBLANES, A_DK), smap3),
                pl.BlockSpec((1, 1, LANES), smap3), pl.BlockSpec((1, B_STATE, B_WIDTH), smap3)]
    st_shapes = [jax.ShapeDtypeStruct((n_seq, A_HEADS, A_DK, A_DV), f32),
                 jax.ShapeDtypeStruct((n_seq, SUBLANES, A_DK), f32),
                 jax.ShapeDtypeStruct((n_seq, 1, LANES), f32),
                 jax.ShapeDtypeStruct((n_seq, B_STATE, B_WIDTH), f32)]
    in_specs = ([pl.BlockSpec((tt, d), rmap), pl.BlockSpec((th, d), nextmap), pl.BlockSpec((1, d), cmap),
                 _resident(win.shape, cmap)]
                + [pl.BlockSpec(c.shape, cmap) for c in consts] + [_resident(wout.shape, cmap)])
    p_shape = pltpu.VMEM((th, sum(EVEN_SPLITS)), f32)
    y_shape = pltpu.VMEM((th, A_WIDTH + B_WIDTH), bf16)
    return pl.pallas_call(
        functools.partial(_even_prompt_body, L=L, n_sub=n_sub),
        grid=(n_seq, n_inner), in_specs=in_specs,
        out_specs=[pl.BlockSpec((tt, d), rmap)] + st_specs + [pl.BlockSpec((SUBLANES, B_CONV_DIM), lambda s, t: (s, 0))],
        out_shape=[jax.ShapeDtypeStruct((rows, d), f32)] + st_shapes
        + [jax.ShapeDtypeStruct((n_seq * SUBLANES, B_CONV_DIM), f32)],
        scratch_shapes=[p_shape, p_shape, y_shape, y_shape, pltpu.VMEM((th, d), bf16),
                        pltpu.VMEM((SUBLANES + L, B_CONV_DIM), f32), pltpu.VMEM((L, B_WIDTH), f32)],
        compiler_params=_params(("arbitrary", "arbitrary")),
        name="even_prompt",
    )(x, x, gmix, win, *consts, wout)


def _even_sample_body(*refs, L, n_sub, n_pad):
    it = iter(refs)
    proj_refs = tuple(next(it) for _ in range(7))
    hist_ref = next(it)
    state0_refs = tuple(next(it) for _ in range(4))
    const_refs = tuple(next(it) for _ in range(N_EVEN_CONSTS))
    y_ref = next(it)
    state_refs = tuple(next(it) for _ in range(4))
    tail_ref = next(it)
    conv_scr, yb_scr = next(it), next(it)

    conv_scr[0:SUBLANES, :] = jnp.zeros((SUBLANES, B_CONV_DIM), f32)
    for r, r0 in zip(state_refs, state0_refs):
        r[...] = r0[...]

    def step(c, carry):
        _even_chunk(pl.multiple_of(c * L, L), c, L, n_pad, False, proj_refs, hist_ref, const_refs, y_ref,
                    state_refs, tail_ref, conv_scr, yb_scr)
        return carry

    lax.fori_loop(0, n_sub, step, 0)


def _even_sample(proj, state, consts, *, n_sub, n_pad):
    rows = proj[0].shape[0]
    L = SEQ_PAD
    n_seq = rows // L
    tt = L * n_sub
    rmap = lambda o: (o, 0)
    cmap = lambda o: (0, 0)
    smap4 = lambda o: (o, 0, 0, 0)
    smap3 = lambda o: (o, 0, 0)
    st_specs = [pl.BlockSpec((n_sub, A_HEADS, A_DK, A_DV), smap4), pl.BlockSpec((n_sub, SUBLANES, A_DK), smap3),
                pl.BlockSpec((n_sub, 1, LANES), smap3), pl.BlockSpec((n_sub, B_HEADS, B_HEADDIM, B_STATE), smap4)]
    st_shapes = [jax.ShapeDtypeStruct(s.shape, f32) for s in state[1:]]
    in_specs = ([pl.BlockSpec((tt, a.shape[1]), rmap) for a in proj] + [pl.BlockSpec((tt, B_CONV_DIM), rmap)]
                + st_specs + [pl.BlockSpec(c.shape, cmap) for c in consts])
    return pl.pallas_call(
        functools.partial(_even_sample_body, L=L, n_sub=n_sub, n_pad=n_pad),
        grid=(n_seq // n_sub,), in_specs=in_specs,
        out_specs=[pl.BlockSpec((tt, A_WIDTH + B_WIDTH), rmap)] + st_specs + [pl.BlockSpec((tt, B_CONV_DIM), rmap)],
        out_shape=[jax.ShapeDtypeStruct((rows, A_WIDTH + B_WIDTH), f32)] + st_shapes
        + [jax.ShapeDtypeStruct((rows, B_CONV_DIM), f32)],
        scratch_shapes=[pltpu.VMEM((SUBLANES + L, B_CONV_DIM), f32), pltpu.VMEM((L, B_WIDTH), f32)],
        compiler_params=_params(("arbitrary",)),
        name="even_sample",
    )(*proj, *state, *consts)


def _odd_chunk(r0, rope_r0, slot, L, proj_refs, const_refs, y_ref, S_ref, side=None):
    q_ref, k_ref, v_ref, g_ref = proj_refs
    cos_ref, sin_ref, intra_ref, cross_ref, into_ref, cdec_ref, norm_ref = const_refs
    rows = pl.ds(r0, L)
    cosf = cos_ref[pl.ds(rope_r0, L), :]
    sinf = sin_ref[pl.ds(rope_r0, L), :]
    cross = cross_ref[...]
    into = into_ref[...]
    cdec = cdec_ref[...]
    for h in range(C_HEADS):
        for run in (side[h] if side else ()):
            run()
        qh = q_ref[rows, h * C_DK:(h + 1) * C_DK]
        kh = k_ref[rows, h * C_DK:(h + 1) * C_DK]
        qh = qh * cosf + pltpu.roll(qh, C_DK // 2, 1) * sinf
        kh = (kh * cosf + pltpu.roll(kh, C_DK // 2, 1) * sinf) * (C_DK ** -0.5)
        vb = v_ref[rows, h * C_DV:(h + 1) * C_DV].astype(bf16)
        qb = qh.astype(bf16)
        s = _nt(qb, kh.astype(bf16)) * intra_ref[h]
        S = S_ref[slot, h]
        o = _nn(s.astype(bf16), vb) + _nn(qb, S.astype(bf16)) * cross[:, h:h + 1]
        S_ref[slot, h] = S * cdec[:, h:h + 1] + _tn((kh * into[:, h:h + 1]).astype(bf16), vb)
        y = _layernorm_nogain(o) * norm_ref[:, h * C_DV:(h + 1) * C_DV]
        y = y * _silu(g_ref[rows, h * C_DV:(h + 1) * C_DV])
        y_ref[rows, h * C_DV:(h + 1) * C_DV] = y.astype(y_ref.dtype)


N_ODD_CONSTS = 7


def _odd_prompt_body(*refs, L, n_sub):
    it = iter(refs)
    x_ref, xnext_ref, gmix_ref, win_ref = next(it), next(it), next(it), next(it)
    const_refs = tuple(next(it) for _ in range(N_ODD_CONSTS))
    wout_ref = next(it)
    o_ref, S_ref = next(it), next(it)
    p_scr, y_scr = (next(it), next(it)), (next(it), next(it))
    h_scr = next(it)
    th = L * n_sub
    gmix = gmix_ref[...]

    @pl.when((pl.program_id(0) == 0) & (pl.program_id(1) == 0))
    def _():
        _project(_rmsnorm(x_ref[0:th, :], gmix).astype(bf16), win_ref, p_scr[0], ODD_SPLITS)

    @pl.when(pl.program_id(1) == 0)
    def _():
        S_ref[...] = jnp.zeros(S_ref.shape, f32)

    for half in range(2):
        ahead = xnext_ref[...] if half else x_ref[th:2 * th, :]
        h_scr[...] = _rmsnorm(ahead, gmix).astype(bf16)
        work = _proj_pieces(h_scr, win_ref, p_scr[1 - half])
        if half:
            work = work + _out_pieces(x_ref, y_scr[0], wout_ref, o_ref, slice(0, th))
        side = _spread(work, n_sub * C_HEADS)
        proj_refs = _split_views(p_scr[half], ODD_SPLITS)
        for c in range(n_sub):
            _odd_chunk(c * L, half * th + c * L, 0, L, proj_refs, const_refs, y_scr[half], S_ref,
                       side[c * C_HEADS:(c + 1) * C_HEADS])
    for run in _out_pieces(x_ref, y_scr[1], wout_ref, o_ref, slice(th, 2 * th)):
        run()


def _odd_prompt(x, gmix, win, cos, sin, consts, wout, *, seq_rows, L, n_sub):
    rows, d = x.shape
    n_seq = rows // seq_rows
    th = L * n_sub
    tt = 2 * th
    n_inner = seq_rows // tt
    n_half = rows // th
    rmap = lambda s, t: (s * n_inner + t, 0)
    nextmap = lambda s, t: (jnp.minimum(2 * (s * n_inner + t) + 2, n_half - 1), 0)
    cmap = lambda s, t: (0, 0)
    st_spec = pl.BlockSpec((1, C_HEADS, C_DK, C_DV), lambda s, t: (s, 0, 0, 0))
    rope_spec = pl.BlockSpec((tt, LANES), lambda s, t: (t, 0))
    in_specs = [pl.BlockSpec((tt, d), rmap), pl.BlockSpec((th, d), nextmap), pl.BlockSpec((1, d), cmap),
                _resident(win.shape, cmap), rope_spec, rope_spec]
    for c in consts:
        in_specs.append(pl.BlockSpec(c.shape, (lambda s, t: (0, 0, 0)) if c.ndim == 3 else cmap))
    in_specs.append(_resident(wout.shape, cmap))
    p_shape = pltpu.VMEM((th, sum(ODD_SPLITS)), f32)
    y_shape = pltpu.VMEM((th, C_WIDTH), bf16)
    return pl.pallas_call(
        functools.partial(_odd_prompt_body, L=L, n_sub=n_sub),
        grid=(n_seq, n_inner), in_specs=in_specs,
        out_specs=[pl.BlockSpec((tt, d), rmap), st_spec],
        out_shape=[jax.ShapeDtypeStruct((rows, d), f32), jax.ShapeDtypeStruct((n_seq, C_HEADS, C_DK, C_DV), f32)],
        scratch_shapes=[p_shape, p_shape, y_shape, y_shape, pltpu.VMEM((th, d), bf16)],
        compiler_params=_params(("arbitrary", "arbitrary")),
        name="odd_prompt",
    )(x, x, gmix, win, cos, sin, *consts, wout)


def _odd_sample_body(*refs, L, n_sub):
    it = iter(refs)
    proj_refs = tuple(next(it) for _ in range(4))
    S0_ref = next(it)
    const_refs = tuple(next(it) for _ in range(N_ODD_CONSTS))
    y_ref, S_ref = next(it), next(it)
    S_ref[...] = S0_ref[...]

    def step(c, carry):
        _odd_chunk(pl.multiple_of(c * L, L), 0, c, L, proj_refs, const_refs, y_ref, S_ref)
        return carry

    lax.fori_loop(0, n_sub, step, 0)


def _odd_sample(proj, S0, cos, sin, consts, *, n_sub):
    rows = proj[0].shape[0]
    L = SEQ_PAD
    n_seq = rows // L
    tt = L * n_sub
    rmap = lambda o: (o, 0)
    cmap = lambda o: (0, 0)
    st_spec = pl.BlockSpec((n_sub, C_HEADS, C_DK, C_DV), lambda o: (o, 0, 0, 0))
    in_specs = [pl.BlockSpec((tt, a.shape[1]), rmap) for a in proj] + [st_spec]
    for c in (cos, sin) + tuple(consts):
        in_specs.append(pl.BlockSpec(c.shape, (lambda o: (0, 0, 0)) if c.ndim == 3 else cmap))
    return pl.pallas_call(
        functools.partial(_odd_sample_body, L=L, n_sub=n_sub),
        grid=(n_seq // n_sub,), in_specs=in_specs,
        out_specs=[pl.BlockSpec((tt, C_WIDTH), rmap), st_spec],
        out_shape=[jax.ShapeDtypeStruct((rows, C_WIDTH), f32), jax.ShapeDtypeStruct(S0.shape, f32)],
        compiler_params=_params(("arbitrary",)),
        name="odd_sample",
    )(*proj, S0, cos, sin, *consts)


def _pad_lanes(a, width=LANES):
    return jnp.pad(a, [(0, 0)] * (a.ndim - 1) + [(0, width - a.shape[-1])])


def _even_w_in_cols(w):
    sizes = [A_HEADS * A_DK, A_HEADS * A_DK, A_WIDTH, A_HEADS, A_HEADS, A_WIDTH, B_WIDTH, B_CONV_DIM, B_HEADS]
    q, k, v, ig, fg, og, z, xbc, dt = jnp.split(w, np.cumsum(sizes)[:-1].tolist(), axis=-1)
    return jnp.concatenate([q, k, v, og, z, xbc, _pad_lanes(ig), _pad_lanes(fg), _pad_lanes(dt)], axis=-1).astype(bf16)


def _retention_tables(L, n_pad):
    log_gamma = jnp.log1p(-jnp.exp2(-5.0 - jnp.arange(C_HEADS, dtype=f32)))
    t_real = L - n_pad
    idx = jnp.arange(L, dtype=f32) - n_pad
    real = idx >= 0
    diff = idx[:, None] - idx[None, :]
    intra = jnp.where((diff >= 0) & real[None, :], jnp.exp(log_gamma[:, None, None] * jnp.maximum(diff, 0.0)), 0.0)
    cross = jnp.where(real[:, None], jnp.exp(log_gamma[None, :] * (idx[:, None] + 1.0)), 0.0)
    into = jnp.where(real[:, None], jnp.exp(log_gamma[None, :] * (t_real - 1.0 - idx[:, None])), 0.0)
    cdec = jnp.exp(log_gamma * t_real)[None, :]
    return intra.astype(f32), _pad_lanes(cross), _pad_lanes(into), _pad_lanes(cdec)


def _rope_tables(pos):
    half = C_DK // 2
    inv = ROPE_BASE ** (-jnp.arange(half, dtype=f32) / half)
    ang = pos.astype(f32)[:, None] * inv[None, :]
    cos, sin = jnp.cos(ang), jnp.sin(ang)
    return jnp.concatenate([cos, cos], axis=-1), jnp.concatenate([-sin, sin], axis=-1)


def kernel(x_prompt, x_sample, state_mlstm_C, state_mlstm_n, state_mlstm_m, state_ssd_conv, state_ssd_h, state_ret_S, state_ffn_conv, norm_mix_g, norm_ffn_g, norm_final_g, even_w_in, mlstm_igate_b, mlstm_fgate_b, mlstm_norm_g, ssd_conv_w, ssd_conv_b, ssd_dt_bias, ssd_A_log, ssd_D, ssd_norm_g, even_w_out, odd_w_in, ret_norm_g, odd_w_out, ffn_w_up, ffn_conv_w, ffn_conv_b, ffn_w_down):
    bsz, seq, d = x_prompt.shape
    dbsz, dseq, _ = x_sample.shape
    n_pad = SEQ_PAD - dseq
    assert norm_mix_g.shape[0] == 2 and B_CONV - 1 <= n_pad < SEQ_PAD and seq % (4 * CHUNK) == 0

    row = lambda a: a.reshape(1, -1)
    gate_bias = jnp.concatenate([_pad_lanes(row(mlstm_igate_b[0])), _pad_lanes(row(mlstm_fgate_b[0])),
                                 _pad_lanes(row(ssd_dt_bias[0]))], axis=0)
    head_of_lane = jnp.arange(B_WIDTH, dtype=jnp.int32) // B_HEADDIM
    onehot = (jnp.arange(LANES, dtype=jnp.int32)[:, None] == head_of_lane[None, :]).astype(bf16)
    even_consts = (gate_bias, row(mlstm_norm_g[0]), ssd_conv_w[0], row(ssd_conv_b[0]), _pad_lanes(row(ssd_A_log[0])),
                   row(jnp.repeat(ssd_D[0], B_HEADDIM)), row(ssd_norm_g[0]), onehot)
    g_mix0, g_mix1 = row(norm_mix_g[0]), row(norm_mix_g[1])
    g_ffn0, g_ffn1, g_final = row(norm_ffn_g[0]), row(norm_ffn_g[1]), row(norm_final_g)
    w_even_in, w_even_out = _even_w_in_cols(even_w_in[0]), even_w_out[0].astype(bf16)
    w_odd_in, w_odd_out = odd_w_in[0].astype(bf16), odd_w_out[0].astype(bf16)
    w_up, w_dn = ffn_w_up.astype(bf16), ffn_w_down.astype(bf16)
    ffn_cb = ffn_conv_b[:, None, :]
    ret_norm = row(ret_norm_g[0])
    cw = D_FF // 2

    cos_p, sin_p = _rope_tables(jnp.arange(seq, dtype=jnp.int32))
    xp = x_prompt.reshape(bsz * seq, d)
    xp, pC, pn, pm, pST, p_conv = _even_prompt(xp, g_mix0, w_even_in, even_consts, w_even_out,
                                               seq_rows=seq, L=CHUNK, n_sub=2)
    xp, p_f0 = _ffn(xp, None, g_ffn0, w_up[0], ffn_conv_w[0], ffn_cb[0], w_dn[0], None,
                    seq_rows=seq, tt=512, cw=cw, n_pad=0)
    xp, pSr = _odd_prompt(xp, g_mix1, w_odd_in, cos_p, sin_p, _retention_tables(CHUNK, 0) + (ret_norm,), w_odd_out,
                          seq_rows=seq, L=CHUNK, n_sub=2)
    _, p_f1, yp = _ffn(xp, None, g_ffn1, w_up[1], ffn_conv_w[1], ffn_cb[1], w_dn[1], g_final,
                       seq_rows=seq, tt=512, cw=cw, n_pad=0)
    pS = pST.reshape(bsz, B_STATE, B_HEADS, B_HEADDIM).transpose(0, 2, 3, 1)

    xs = jnp.pad(x_sample, ((0, 0), (n_pad, 0), (0, 0))).reshape(dbsz * SEQ_PAD, d)
    cos_s, sin_s = _rope_tables(PAST_LEN + jnp.arange(SEQ_PAD, dtype=jnp.int32) - n_pad)
    hist = lambda a: jnp.pad(a, ((0, 0), (n_pad - a.shape[1], dseq), (0, 0))).reshape(dbsz * SEQ_PAD, a.shape[-1])
    state = (hist(state_ssd_conv[0]), state_mlstm_C[0],
             jnp.pad(state_mlstm_n[0], ((0, 0), (0, SUBLANES - A_HEADS), (0, 0))),
             _pad_lanes(state_mlstm_m[0])[:, None, :], state_ssd_h[0])
    tm = 256
    proj = _norm_proj(xs, g_mix0, w_even_in, EVEN_SPLITS, tm)
    y, sC, sn, sm, sS, s_conv = _even_sample(proj, state, even_consts, n_sub=4, n_pad=n_pad)
    xs = _proj_res(xs, y, w_even_out, tm)
    xs, s_f0 = _ffn(xs, hist(state_ffn_conv[0]), g_ffn0, w_up[0], ffn_conv_w[0], ffn_cb[0], w_dn[0], None,
                    seq_rows=SEQ_PAD, tt=128, cw=cw, n_pad=n_pad)
    proj = _norm_proj(xs, g_mix1, w_odd_in, ODD_SPLITS, tm)
    y, sSr = _odd_sample(proj, state_ret_S[0], cos_s, sin_s, _retention_tables(SEQ_PAD, n_pad) + (ret_norm,), n_sub=4)
    xs = _proj_res(xs, y, w_odd_out, tm)
    _, s_f1, ys = _ffn(xs, hist(state_ffn_conv[1]), g_ffn1, w_up[1], ffn_conv_w[1], ffn_cb[1], w_dn[1], g_final,
                       seq_rows=SEQ_PAD, tt=128, cw=cw, n_pad=n_pad)

    def pack(bs, C, n, m, S, conv_tail, Sr, f0, f1):
        conv = conv_tail.reshape(bs, SUBLANES, B_CONV_DIM)[:, SUBLANES - (B_CONV - 1):, :]
        ff = jnp.stack([f0.reshape(bs, SUBLANES, 2 * D_FF), f1.reshape(bs, SUBLANES, 2 * D_FF)])
        ff = ff[:, :, SUBLANES - (FFN_CONV - 1):, :]
        return (C[None], n[None, :, :A_HEADS, :], m[None, :, 0, :A_HEADS], conv[None], S[None], Sr[None], ff)

    y_prompt = yp.reshape(bsz, seq, d)
    y_sample = ys.reshape(dbsz, SEQ_PAD, d)[:, n_pad:, :]
    return ((y_prompt, y_sample) + pack(bsz, pC, pn, pm, pS, p_conv, pSr, p_f0, p_f1)
            + pack(dbsz, sC, sn, sm, sS, s_conv, sSr, s_f0, s_f1))
```

```python
import functools

import numpy as np
import jax
import jax.numpy as jnp
from jax import lax
from jax.experimental import pallas as pl
from jax.experimental.pallas import tpu as pltpu

f32 = jnp.float32
bf16 = jnp.bfloat16

EPS = 1e-6
CHUNK = 128
D_MODEL = 1024
A_HEADS, A_DK, A_DV = 4, 128, 256
A_WIDTH = A_HEADS * A_DV
B_HEADS, B_HEADDIM, B_GROUPS, B_STATE, B_CONV = 16, 64, 2, 128, 4
B_WIDTH = B_HEADS * B_HEADDIM
B_CONV_DIM = B_WIDTH + 2 * B_GROUPS * B_STATE
B_HEADS_PER_GROUP = B_HEADS // B_GROUPS
B_GROUP_WIDTH = B_WIDTH // B_GROUPS
C_HEADS, C_DK, C_DV = 8, 128, 256
C_WIDTH = C_HEADS * C_DV
ROPE_BASE = 10000.0
D_FF = 2816
FFN_CONV = 3
PAST_LEN = 16384

LANES = 128
SUBLANES = 8
SEQ_PAD = SUBLANES
NEG_BIG = -1e30
VMEM_LIMIT = 56 * 1024 * 1024

EVEN_SPLITS = (A_HEADS * A_DK, A_HEADS * A_DK, A_WIDTH, A_WIDTH, B_WIDTH, B_CONV_DIM, 3 * LANES)
ODD_SPLITS = (C_HEADS * C_DK, C_HEADS * C_DK, C_WIDTH, C_WIDTH)


def _params(sem):
    return pltpu.CompilerParams(dimension_semantics=sem, vmem_limit_bytes=VMEM_LIMIT)


def _resident(shape, index_map):
    return pl.BlockSpec(shape, index_map, pipeline_mode=pl.Buffered(1))


def _nt(a, b):
    return lax.dot_general(a, b, (((1,), (1,)), ((), ())), preferred_element_type=f32)


def _tn(a, b):
    return lax.dot_general(a, b, (((0,), (0,)), ((), ())), preferred_element_type=f32)


def _nn(a, b):
    return jnp.dot(a, b, preferred_element_type=f32)


def _rmsnorm(x, g):
    return x * lax.rsqrt(jnp.mean(x * x, axis=-1, keepdims=True) + EPS) * g


def _layernorm_nogain(h):
    mu = jnp.mean(h, axis=-1, keepdims=True)
    hc = h - mu
    return hc * lax.rsqrt(jnp.mean(hc * hc, axis=-1, keepdims=True) + EPS)


def _softplus(x):
    return jnp.maximum(x, 0.0) + jnp.log1p(jnp.exp(-jnp.abs(x)))


def _silu(x):
    return x * jax.nn.sigmoid(x)


def _split3(x):
    hi = x.astype(bf16)
    r1 = x - hi.astype(f32)
    mid = r1.astype(bf16)
    lo = (r1 - mid.astype(f32)).astype(bf16)
    return hi, mid, lo


def _cumsum_rows(tril, x):
    hi, mid, lo = _split3(x)
    return _nn(tril, hi) + _nn(tril, mid) + _nn(tril, lo)


def _split_views(ref, splits):
    views, off = [], 0
    for n in splits:
        views.append(ref.at[:, off:off + n])
        off += n
    return views


def _project(h, w_ref, p_ref, splits):
    off = 0
    for n in splits:
        p_ref[:, off:off + n] = _nn(h, w_ref[:, off:off + n])
        off += n


MXU_COLS = 256


def _proj_pieces(h_ref, w_ref, dst_ref):
    def piece(c0, n):
        def run():
            dst_ref[:, c0:c0 + n] = _nn(h_ref[...], w_ref[:, c0:c0 + n])
        return run
    total = w_ref.shape[1]
    return [piece(c0, min(MXU_COLS, total - c0)) for c0 in range(0, total, MXU_COLS)]


def _out_pieces(x_ref, y_ref, w_ref, o_ref, rows):
    def piece(c0, n):
        def run():
            o_ref[rows, c0:c0 + n] = x_ref[rows, c0:c0 + n] + _nn(y_ref[...], w_ref[:, c0:c0 + n])
        return run
    total = w_ref.shape[1]
    return [piece(c0, min(MXU_COLS, total - c0)) for c0 in range(0, total, MXU_COLS)]


def _spread(work, n_slots):
    return [work[i * len(work) // n_slots:(i + 1) * len(work) // n_slots] for i in range(n_slots)]


def _norm_proj_body(x_ref, g_ref, w_ref, *o_refs, splits):
    h = _rmsnorm(x_ref[...], g_ref[...]).astype(bf16)
    off = 0
    for o_ref, n in zip(o_refs, splits):
        o_ref[...] = _nn(h, w_ref[:, off:off + n])
        off += n


def _norm_proj(x, g, w, splits, tm):
    rows, d = x.shape
    return pl.pallas_call(
        functools.partial(_norm_proj_body, splits=splits),
        grid=(rows // tm,),
        in_specs=[pl.BlockSpec((tm, d), lambda i: (i, 0)),
                  pl.BlockSpec((1, d), lambda i: (0, 0)),
                  _resident(w.shape, lambda i: (0, 0))],
        out_specs=[pl.BlockSpec((tm, n), lambda i: (i, 0)) for n in splits],
        out_shape=[jax.ShapeDtypeStruct((rows, n), f32) for n in splits],
        compiler_params=_params(("arbitrary",)),
        name="norm_proj",
    )(x, g, w)


def _proj_res_body(x_ref, y_ref, w_ref, o_ref):
    o_ref[...] = x_ref[...] + _nn(y_ref[...].astype(bf16), w_ref[...])


def _proj_res(x, y, w, tm):
    rows, d = x.shape
    k = y.shape[1]
    return pl.pallas_call(
        _proj_res_body,
        grid=(rows // tm,),
        in_specs=[pl.BlockSpec((tm, d), lambda i: (i, 0)),
                  pl.BlockSpec((tm, k), lambda i: (i, 0)),
                  _resident(w.shape, lambda i: (0, 0))],
        out_specs=pl.BlockSpec((tm, d), lambda i: (i, 0)),
        out_shape=jax.ShapeDtypeStruct((rows, d), f32),
        compiler_params=_params(("arbitrary",)),
        name="proj_res",
    )(x, y, w)


def _ffn_body(*refs, tt, cw, n_pad, sample, final):
    it = iter(refs)
    x_ref = next(it)
    buf_ref = next(it) if sample else None
    g_ref, wup_ref, cw_ref, cb_ref, wdn_ref = next(it), next(it), next(it), next(it), next(it)
    gf_ref = next(it) if final else None
    o_ref, tail_ref = next(it), next(it)
    yn_ref = next(it) if final else None
    scr = next(it)
    carry = None if sample else next(it)

    if not sample:
        @pl.when(pl.program_id(1) == 0)
        def _():
            carry[...] = jnp.zeros(carry.shape, f32)
    else:
        row = lax.broadcasted_iota(jnp.int32, (tt, 1), 0) % SEQ_PAD
        is_hist = (row >= n_pad - (FFN_CONV - 1)) & (row < n_pad)

    x = x_ref[...]
    h = _rmsnorm(x, g_ref[...]).astype(bf16)
    acc = jnp.zeros((tt, D_MODEL), f32)
    for j in range(D_FF // cw):
        conv = []
        for part in range(2):
            c0 = part * D_FF + j * cw
            u = _nn(h, wup_ref[:, c0:c0 + cw])
            if sample:
                u = jnp.where(is_hist, buf_ref[:, c0:c0 + cw], u)
                scr[0:SUBLANES, :] = jnp.zeros((SUBLANES, cw), f32)
                tail_ref[:, c0:c0 + cw] = u
            else:
                scr[0:SUBLANES, :] = carry[:, c0:c0 + cw]
                tail_ref[:, c0:c0 + cw] = u[tt - SUBLANES:tt, :]
                carry[:, c0:c0 + cw] = u[tt - SUBLANES:tt, :]
            scr[SUBLANES:SUBLANES + tt, :] = u
            y = cb_ref[:, c0:c0 + cw] + scr[SUBLANES - 2:SUBLANES - 2 + tt, :] * cw_ref[0:1, c0:c0 + cw]
            y = y + scr[SUBLANES - 1:SUBLANES - 1 + tt, :] * cw_ref[1:2, c0:c0 + cw]
            y = y + u * cw_ref[2:3, c0:c0 + cw]
            conv.append(y)
        act = (_silu(conv[0]) * conv[1]).astype(bf16)
        acc = acc + _nn(act, wdn_ref[j * cw:(j + 1) * cw, :])
    out = x + acc
    o_ref[...] = out
    if final:
        yn_ref[...] = _rmsnorm(out, gf_ref[...])


def _ffn(x, hist, g, wup, conv_w, conv_b, wdn, g_final, *, seq_rows, tt, cw, n_pad):
    rows, d = x.shape
    sample = hist is not None
    final = g_final is not None
    n_seq = rows // seq_rows
    if sample:
        grid = (rows // tt,)
        rmap = lambda i: (i, 0)
        cmap = lambda i: (0, 0)
        tail_spec = pl.BlockSpec((tt, 2 * D_FF), rmap)
        sem = ("arbitrary",)
    else:
        n_inner = seq_rows // tt
        grid = (n_seq, n_inner)
        rmap = lambda s, t: (s * n_inner + t, 0)
        cmap = lambda s, t: (0, 0)
        tail_spec = pl.BlockSpec((SUBLANES, 2 * D_FF), lambda s, t: (s, 0))
        sem = ("arbitrary", "arbitrary")
    in_specs = [pl.BlockSpec((tt, d), rmap)]
    args = [x]
    if sample:
        in_specs.append(pl.BlockSpec((tt, 2 * D_FF), rmap))
        args.append(hist)
    in_specs += [pl.BlockSpec((1, d), cmap), _resident(wup.shape, cmap), pl.BlockSpec(conv_w.shape, cmap),
                 pl.BlockSpec(conv_b.shape, cmap), _resident(wdn.shape, cmap)]
    args += [g, wup, conv_w, conv_b, wdn]
    out_specs = [pl.BlockSpec((tt, d), rmap), tail_spec]
    out_shape = [jax.ShapeDtypeStruct((rows, d), f32), jax.ShapeDtypeStruct((n_seq * SUBLANES, 2 * D_FF), f32)]
    if final:
        in_specs.append(pl.BlockSpec((1, d), cmap))
        args.append(g_final)
        out_specs.append(pl.BlockSpec((tt, d), rmap))
        out_shape.append(jax.ShapeDtypeStruct((rows, d), f32))
    scratch = [pltpu.VMEM((SUBLANES + tt, cw), f32)]
    if not sample:
        scratch.append(pltpu.VMEM((SUBLANES, 2 * D_FF), f32))
    return pl.pallas_call(
        functools.partial(_ffn_body, tt=tt, cw=cw, n_pad=n_pad, sample=sample, final=final),
        grid=grid, in_specs=in_specs, out_specs=out_specs, out_shape=out_shape,
        scratch_shapes=scratch, compiler_params=_params(sem),
        name="ffn_sample" if sample else "ffn_prompt",
    )(*args)


EVEN_SLOTS = A_HEADS + B_HEADS // 2


def _even_chunk(r0, slot, L, n_pad, pairs, proj_refs, hist_ref, const_refs, y_ref, state_refs, tail_ref,
                conv_scr, yb_scr, side=None):
    q_ref, k_ref, v_ref, og_ref, z_ref, xbc_ref, gt_ref = proj_refs
    gb_ref, anorm_ref, convw_ref, convb_ref, alog_ref, dx_ref, bnorm_ref = const_refs
    C_ref, n_ref, m_ref, S_ref = state_refs
    rows = pl.ds(r0, L)
    ri = lax.broadcasted_iota(jnp.int32, (L, L), 0)
    ci = lax.broadcasted_iota(jnp.int32, (L, L), 1)
    causal = ri >= ci
    tril = causal.astype(bf16)
    valid = None
    if n_pad:
        valid = lax.broadcasted_iota(jnp.int32, (L, 1), 0) >= n_pad

    gates = gt_ref[rows, :]
    li = gates[:, 0:LANES] + gb_ref[0:1, :]
    fpre = gates[:, LANES:2 * LANES] + gb_ref[1:2, :]
    lf = -_softplus(-fpre)
    dt = _softplus(gates[:, 2 * LANES:3 * LANES] + gb_ref[2:3, :])
    if n_pad:
        li = jnp.where(valid, li, NEG_BIG)
        lf = jnp.where(valid, lf, 0.0)
        dt = jnp.where(valid, dt, 0.0)
    a = dt * (-jnp.exp(alog_ref[...]))
    cums = _cumsum_rows(tril, jnp.concatenate([lf, a], axis=1))
    bcum = cums[:, 0:LANES]
    acum = cums[:, LANES:2 * LANES]
    liT, bT, aT, dtT = li.T, bcum.T, acum.T, dt.T

    m_row = m_ref[slot]
    b_last = bcum[L - 1:L, :]
    log_g = b_last - bcum + li
    m_new = jnp.maximum(b_last + m_row, jnp.max(log_g, axis=0, keepdims=True))
    gfac = jnp.exp(log_g - m_new)
    cdecay = jnp.exp(b_last + m_row - m_new)
    m_ref[slot] = m_new
    for h in range(A_HEADS):
        for run in (side[h] if side else ()):
            run()
        bcol, brow, lirow = bcum[:, h:h + 1], bT[h:h + 1, :], liT[h:h + 1, :]
        m_h = m_row[:, h:h + 1]
        logw = jnp.where(causal, bcol - brow + lirow, -jnp.inf)
        log_prev = bcol + m_h
        m_t = jnp.maximum(log_prev, jnp.max(logw, axis=-1, keepdims=True))
        w_in = jnp.exp(logw - m_t)
        w_prev = jnp.exp(log_prev - m_t)
        qh = q_ref[rows, h * A_DK:(h + 1) * A_DK]
        kh = k_ref[rows, h * A_DK:(h + 1) * A_DK] * (A_DK ** -0.5)
        vb = v_ref[rows, h * A_DV:(h + 1) * A_DV].astype(bf16)
        qb = qh.astype(bf16)
        s = _nt(qb, kh.astype(bf16)) * w_in
        C = C_ref[slot, h]
        nrow = n_ref[slot, h:h + 1, :]
        num = _nn(s.astype(bf16), vb) + _nn(qb, C.astype(bf16)) * w_prev
        den = jnp.sum(s, axis=-1, keepdims=True) + jnp.sum(qh * nrow, axis=-1, keepdims=True) * w_prev
        hout = num / jnp.maximum(jnp.abs(den), jnp.exp(-m_t))
        kg = kh * gfac[:, h:h + 1]
        dec_h = cdecay[:, h:h + 1]
        C_ref[slot, h] = C * dec_h + _tn(kg.astype(bf16), vb)
        n_ref[slot, h:h + 1, :] = nrow * dec_h + jnp.sum(kg, axis=0, keepdims=True)
        ya = _layernorm_nogain(hout) * anorm_ref[:, h * A_DV:(h + 1) * A_DV]
        ya = ya * jax.nn.sigmoid(og_ref[rows, h * A_DV:(h + 1) * A_DV])
        y_ref[rows, h * A_DV:(h + 1) * A_DV] = ya.astype(y_ref.dtype)

    xraw = xbc_ref[rows, :]
    if hist_ref is not None:
        is_hist = (lax.broadcasted_iota(jnp.int32, (L, 1), 0) >= n_pad - (B_CONV - 1)) & jnp.logical_not(valid)
        xraw = jnp.where(is_hist, hist_ref[rows, :], xraw)
    conv_scr[SUBLANES:SUBLANES + L, :] = xraw
    xc = convb_ref[...] + conv_scr[SUBLANES - 3:SUBLANES - 3 + L, :] * convw_ref[0:1, :]
    xc = xc + conv_scr[SUBLANES - 2:SUBLANES - 2 + L, :] * convw_ref[1:2, :]
    xc = xc + conv_scr[SUBLANES - 1:SUBLANES - 1 + L, :] * convw_ref[2:3, :]
    xc = xc + xraw * convw_ref[3:4, :]
    new_tail = conv_scr[L:L + SUBLANES, :]
    tail_ref[pl.ds(slot * SUBLANES, SUBLANES), :] = new_tail
    conv_scr[0:SUBLANES, :] = new_tail
    xc = _silu(xc)

    a_last = acum[L - 1:L, :]
    wtile = jnp.exp(a_last - acum) * dt
    expa = jnp.exp(acum)

    def head_matrix(cb, h):
        acol, arow, dtrow = acum[:, h:h + 1], aT[h:h + 1, :], dtT[h:h + 1, :]
        decay = jnp.where(causal, jnp.exp(jnp.where(causal, acol - arow, 0.0)), 0.0)
        return (cb * decay * dtrow).astype(bf16)

    def group_bc(g):
        Bg = xc[:, B_WIDTH + g * B_STATE:B_WIDTH + (g + 1) * B_STATE].astype(bf16)
        c0 = B_WIDTH + B_GROUPS * B_STATE + g * B_STATE
        Cg = xc[:, c0:c0 + B_STATE].astype(bf16)
        return Bg, Cg, _nt(Cg, Bg)

    if pairs:
        low_half = lax.broadcasted_iota(jnp.int32, (L, LANES), 1) < B_HEADDIM

        def pair_lanes(t, j):
            return jnp.where(low_half, jnp.broadcast_to(t[:, 2 * j:2 * j + 1], (L, LANES)),
                             jnp.broadcast_to(t[:, 2 * j + 1:2 * j + 2], (L, LANES)))

        pairs_per_group = B_HEADS_PER_GROUP // 2
        for g in range(B_GROUPS):
            Bg, Cg, cb = group_bc(g)
            gs = slice(g * B_GROUP_WIDTH, (g + 1) * B_GROUP_WIDTH)
            ST = S_ref[slot, :, gs]
            inter = _nn(Cg, ST.astype(bf16))
            xw, decs = [], []
            for jp in range(pairs_per_group):
                j = g * pairs_per_group + jp
                for run in (side[A_HEADS + j] if side else ()):
                    run()
                ps = slice(j * LANES, (j + 1) * LANES)
                e_pair, w_pair = pair_lanes(expa, j), pair_lanes(wtile, j)
                mcat = jnp.concatenate([head_matrix(cb, 2 * j), head_matrix(cb, 2 * j + 1)], axis=1)
                xp = xc[:, ps]
                xbd = jnp.concatenate([jnp.where(low_half, xp, 0.0).astype(bf16),
                                       jnp.where(low_half, 0.0, xp).astype(bf16)], axis=0)
                yb_scr[:, ps] = (_nn(mcat, xbd) + inter[:, jp * LANES:(jp + 1) * LANES] * e_pair) + dx_ref[:, ps] * xp
                xw.append((xp * w_pair).astype(bf16))
                decs.append(e_pair[L - 1:L, :])
            S_ref[slot, :, gs] = ST * jnp.concatenate(decs, axis=1) + _tn(Bg, jnp.concatenate(xw, axis=1))
    else:
        dec_last = jnp.exp(a_last)
        for g in range(B_GROUPS):
            Bg, Cg, cb = group_bc(g)
            for e in range(B_HEADS_PER_GROUP):
                h = g * B_HEADS_PER_GROUP + e
                hs = slice(h * B_HEADDIM, (h + 1) * B_HEADDIM)
                xh = xc[:, hs]
                S = S_ref[slot, h]
                yh = _nn(head_matrix(cb, h), xh.astype(bf16)) + _nt(Cg, S.astype(bf16)) * expa[:, h:h + 1]
                S_ref[slot, h] = S * dec_last[:, h:h + 1] + _tn((xh * wtile[:, h:h + 1]).astype(bf16), Bg)
                yb_scr[:, hs] = yh + dx_ref[:, hs] * xh
    yb = yb_scr[...] * _silu(z_ref[rows, :])
    for g in range(B_GROUPS):
        gs = slice(g * B_GROUP_WIDTH, (g + 1) * B_GROUP_WIDTH)
        yg = yb[:, gs]
        yg = yg * lax.rsqrt(jnp.mean(yg * yg, axis=-1, keepdims=True) + EPS)
        y_ref[rows, A_WIDTH + g * B_GROUP_WIDTH:A_WIDTH + (g + 1) * B_GROUP_WIDTH] = (
            yg * bnorm_ref[:, gs]).astype(y_ref.dtype)


N_EVEN_CONSTS = 7


def _even_prompt_body(*refs, L, n_sub):
    it = iter(refs)
    x_ref, xnext_ref, gmix_ref, win_ref = next(it), next(it), next(it), next(it)
    const_refs = tuple(next(it) for _ in range(N_EVEN_CONSTS))
    wout_ref = next(it)
    o_ref = next(it)
    state_refs = tuple(next(it) for _ in range(4))
    tail_ref = next(it)
    p_scr, y_scr = (next(it), next(it)), (next(it), next(it))
    h_scr, conv_scr, yb_scr = (next(it) for _ in range(3))
    th = L * n_sub
    gmix = gmix_ref[...]

    @pl.when((pl.program_id(0) == 0) & (pl.program_id(1) == 0))
    def _():
        _project(_rmsnorm(x_ref[0:th, :], gmix).astype(bf16), win_ref, p_scr[0], EVEN_SPLITS)

    @pl.when(pl.program_id(1) == 0)
    def _():
        conv_scr[0:SUBLANES, :] = jnp.zeros((SUBLANES, B_CONV_DIM), f32)
        for r in state_refs:
            r[...] = jnp.zeros(r.shape, f32)

    for half in range(2):
        ahead = xnext_ref[...] if half else x_ref[th:2 * th, :]
        h_scr[...] = _rmsnorm(ahead, gmix).astype(bf16)
        work = _proj_pieces(h_scr, win_ref, p_scr[1 - half])
        if half:
            work = work + _out_pieces(x_ref, y_scr[0], wout_ref, o_ref, slice(0, th))
        side = _spread(work, n_sub * EVEN_SLOTS)
        proj_refs = _split_views(p_scr[half], EVEN_SPLITS)
        for c in range(n_sub):
            _even_chunk(c * L, 0, L, 0, True, proj_refs, None, const_refs, y_scr[half], state_refs, tail_ref,
                        conv_scr, yb_scr, side[c * EVEN_SLOTS:(c + 1) * EVEN_SLOTS])
    for run in _out_pieces(x_ref, y_scr[1], wout_ref, o_ref, slice(th, 2 * th)):
        run()


def _even_prompt(x, gmix, win, consts, wout, *, seq_rows, L, n_sub):
    rows, d = x.shape
    n_seq = rows // seq_rows
    th = L * n_sub
    tt = 2 * th
    n_inner = seq_rows // tt
    n_half = rows // th
    rmap = lambda s, t: (s * n_inner + t, 0)
    nextmap = lambda s, t: (jnp.minimum(2 * (s * n_inner + t) + 2, n_half - 1), 0)
    cmap = lambda s, t: (0, 0)
    smap4 = lambda s, t: (s, 0, 0, 0)
    smap3 = lambda s, t: (s, 0, 0)
    st_specs = [pl.BlockSpec((1, A_HEADS, A_DK, A_DV), smap4), pl.BlockSpec((1, SUBLANES, A_DK), smap3),
                pl.BlockSpec((1, 1, LANES), smap3), pl.BlockSpec((1, B_STATE, B_WIDTH), smap3)]
    st_shapes = [jax.ShapeDtypeStruct((n_seq, A_HEADS, A_DK, A_DV), f32),
                 jax.ShapeDtypeStruct((n_seq, SUBLANES, A_DK), f32),
                 jax.ShapeDtypeStruct((n_seq, 1, LANES), f32),
                 jax.ShapeDtypeStruct((n_seq, B_STATE, B_WIDTH), f32)]
    in_specs = ([pl.BlockSpec((tt, d), rmap), pl.BlockSpec((th, d), nextmap), pl.BlockSpec((1, d), cmap),
                 _resident(win.shape, cmap)]
                + [pl.BlockSpec(c.shape, cmap) for c in consts] + [_resident(wout.shape, cmap)])
    p_shape = pltpu.VMEM((th, sum(EVEN_SPLITS)), f32)
    y_shape = pltpu.VMEM((th, A_WIDTH + B_WIDTH), bf16)
    return pl.pallas_call(
        functools.partial(_even_prompt_body, L=L, n_sub=n_sub),
        grid=(n_seq, n_inner), in_specs=in_specs,
        out_specs=[pl.BlockSpec((tt, d), rmap)] + st_specs + [pl.BlockSpec((SUBLANES, B_CONV_DIM), lambda s, t: (s, 0))],
        out_shape=[jax.ShapeDtypeStruct((rows, d), f32)] + st_shapes
        + [jax.ShapeDtypeStruct((n_seq * SUBLANES, B_CONV_DIM), f32)],
        scratch_shapes=[p_shape, p_shape, y_shape, y_shape, pltpu.VMEM((th, d), bf16),
                        pltpu.VMEM((SUBLANES + L, B_CONV_DIM), f32), pltpu.VMEM((L, B_WIDTH), f32)],
        compiler_params=_params(("arbitrary", "arbitrary")),
        name="even_prompt",
    )(x, x, gmix, win, *consts, wout)


def _even_sample_body(*refs, L, n_sub, n_pad):
    it = iter(refs)
    proj_refs = tuple(next(it) for _ in range(7))
    hist_ref = next(it)
    state0_refs = tuple(next(it) for _ in range(4))
    const_refs = tuple(next(it) for _ in range(N_EVEN_CONSTS))
    y_ref = next(it)
    state_refs = tuple(next(it) for _ in range(4))
    tail_ref = next(it)
    conv_scr, yb_scr = next(it), next(it)

    conv_scr[0:SUBLANES, :] = jnp.zeros((SUBLANES, B_CONV_DIM), f32)
    for r, r0 in zip(state_refs, state0_refs):
        r[...] = r0[...]

    def step(c, carry):
        _even_chunk(pl.multiple_of(c * L, L), c, L, n_pad, False, proj_refs, hist_ref, const_refs, y_ref,
                    state_refs, tail_ref, conv_scr, yb_scr)
        return carry

    lax.fori_loop(0, n_sub, step, 0)


def _even_sample(proj, state, consts, *, n_sub, n_pad):
    rows = proj[0].shape[0]
    L = SEQ_PAD
    n_seq = rows // L
    tt = L * n_sub
    rmap = lambda o: (o, 0)
    cmap = lambda o: (0, 0)
    smap4 = lambda o: (o, 0, 0, 0)
    smap3 = lambda o: (o, 0, 0)
    st_specs = [pl.BlockSpec((n_sub, A_HEADS, A_DK, A_DV), smap4), pl.BlockSpec((n_sub, SUBLANES, A_DK), smap3),
                pl.BlockSpec((n_sub, 1, LANES), smap3), pl.BlockSpec((n_sub, B_HEADS, B_HEADDIM, B_STATE), smap4)]
    st_shapes = [jax.ShapeDtypeStruct(s.shape, f32) for s in state[1:]]
    in_specs = ([pl.BlockSpec((tt, a.shape[1]), rmap) for a in proj] + [pl.BlockSpec((tt, B_CONV_DIM), rmap)]
                + st_specs + [pl.BlockSpec(c.shape, cmap) for c in consts])
    return pl.pallas_call(
        functools.partial(_even_sample_body, L=L, n_sub=n_sub, n_pad=n_pad),
        grid=(n_seq // n_sub,), in_specs=in_specs,
        out_specs=[pl.BlockSpec((tt, A_WIDTH + B_WIDTH), rmap)] + st_specs + [pl.BlockSpec((tt, B_CONV_DIM), rmap)],
        out_shape=[jax.ShapeDtypeStruct((rows, A_WIDTH + B_WIDTH), f32)] + st_shapes
        + [jax.ShapeDtypeStruct((rows, B_CONV_DIM), f32)],
        scratch_shapes=[pltpu.VMEM((SUBLANES + L, B_CONV_DIM), f32), pltpu.VMEM((L, B_WIDTH), f32)],
        compiler_params=_params(("arbitrary",)),
        name="even_sample",
    )(*proj, *state, *consts)


def _odd_chunk(r0, rope_r0, slot, L, proj_refs, const_refs, y_ref, S_ref, side=None):
    q_ref, k_ref, v_ref, g_ref = proj_refs
    cos_ref, sin_ref, intra_ref, cross_ref, into_ref, cdec_ref, norm_ref = const_refs
    rows = pl.ds(r0, L)
    cosf = cos_ref[pl.ds(rope_r0, L), :]
    sinf = sin_ref[pl.ds(rope_r0, L), :]
    cross = cross_ref[...]
    into = into_ref[...]
    cdec = cdec_ref[...]
    for h in range(C_HEADS):
        for run in (side[h] if side else ()):
            run()
        qh = q_ref[rows, h * C_DK:(h + 1) * C_DK]
        kh = k_ref[rows, h * C_DK:(h + 1) * C_DK]
        qh = qh * cosf + pltpu.roll(qh, C_DK // 2, 1) * sinf
        kh = (kh * cosf + pltpu.roll(kh, C_DK // 2, 1) * sinf) * (C_DK ** -0.5)
        vb = v_ref[rows, h * C_DV:(h + 1) * C_DV].astype(bf16)
        qb = qh.astype(bf16)
        s = _nt(qb, kh.astype(bf16)) * intra_ref[h]
        S = S_ref[slot, h]
        o = _nn(s.astype(bf16), vb) + _nn(qb, S.astype(bf16)) * cross[:, h:h + 1]
        S_ref[slot, h] = S * cdec[:, h:h + 1] + _tn((kh * into[:, h:h + 1]).astype(bf16), vb)
        y = _layernorm_nogain(o) * norm_ref[:, h * C_DV:(h + 1) * C_DV]
        y = y * _silu(g_ref[rows, h * C_DV:(h + 1) * C_DV])
        y_ref[rows, h * C_DV:(h + 1) * C_DV] = y.astype(y_ref.dtype)


N_ODD_CONSTS = 7


def _odd_prompt_body(*refs, L, n_sub):
    it = iter(refs)
    x_ref, xnext_ref, gmix_ref, win_ref = next(it), next(it), next(it), next(it)
    const_refs = tuple(next(it) for _ in range(N_ODD_CONSTS))
    wout_ref = next(it)
    o_ref, S_ref = next(it), next(it)
    p_scr, y_scr = (next(it), next(it)), (next(it), next(it))
    h_scr = next(it)
    th = L * n_sub
    gmix = gmix_ref[...]

    @pl.when((pl.program_id(0) == 0) & (pl.program_id(1) == 0))
    def _():
        _project(_rmsnorm(x_ref[0:th, :], gmix).astype(bf16), win_ref, p_scr[0], ODD_SPLITS)

    @pl.when(pl.program_id(1) == 0)
    def _():
        S_ref[...] = jnp.zeros(S_ref.shape, f32)

    for half in range(2):
        ahead = xnext_ref[...] if half else x_ref[th:2 * th, :]
        h_scr[...] = _rmsnorm(ahead, gmix).astype(bf16)
        work = _proj_pieces(h_scr, win_ref, p_scr[1 - half])
        if half:
            work = work + _out_pieces(x_ref, y_scr[0], wout_ref, o_ref, slice(0, th))
        side = _spread(work, n_sub * C_HEADS)
        proj_refs = _split_views(p_scr[half], ODD_SPLITS)
        for c in range(n_sub):
            _odd_chunk(c * L, half * th + c * L, 0, L, proj_refs, const_refs, y_scr[half], S_ref,
                       side[c * C_HEADS:(c + 1) * C_HEADS])
    for run in _out_pieces(x_ref, y_scr[1], wout_ref, o_ref, slice(th, 2 * th)):
        run()


def _odd_prompt(x, gmix, win, cos, sin, consts, wout, *, seq_rows, L, n_sub):
    rows, d = x.shape
    n_seq = rows // seq_rows
    th = L * n_sub
    tt = 2 * th
    n_inner = seq_rows // tt
    n_half = rows // th
    rmap = lambda s, t: (s * n_inner + t, 0)
    nextmap = lambda s, t: (jnp.minimum(2 * (s * n_inner + t) + 2, n_half - 1), 0)
    cmap = lambda s, t: (0, 0)
    st_spec = pl.BlockSpec((1, C_HEADS, C_DK, C_DV), lambda s, t: (s, 0, 0, 0))
    rope_spec = pl.BlockSpec((tt, LANES), lambda s, t: (t, 0))
    in_specs = [pl.BlockSpec((tt, d), rmap), pl.BlockSpec((th, d), nextmap), pl.BlockSpec((1, d), cmap),
                _resident(win.shape, cmap), rope_spec, rope_spec]
    for c in consts:
        in_specs.append(pl.BlockSpec(c.shape, (lambda s, t: (0, 0, 0)) if c.ndim == 3 else cmap))
    in_specs.append(_resident(wout.shape, cmap))
    p_shape = pltpu.VMEM((th, sum(ODD_SPLITS)), f32)
    y_shape = pltpu.VMEM((th, C_WIDTH), bf16)
    return pl.pallas_call(
        functools.partial(_odd_prompt_body, L=L, n_sub=n_sub),
        grid=(n_seq, n_inner), in_specs=in_specs,
        out_specs=[pl.BlockSpec((tt, d), rmap), st_spec],
        out_shape=[jax.ShapeDtypeStruct((rows, d), f32), jax.ShapeDtypeStruct((n_seq, C_HEADS, C_DK, C_DV), f32)],
        scratch_shapes=[p_shape, p_shape, y_shape, y_shape, pltpu.VMEM((th, d), bf16)],
        compiler_params=_params(("arbitrary", "arbitrary")),
        name="odd_prompt",
    )(x, x, gmix, win, cos, sin, *consts, wout)


def _odd_sample_body(*refs, L, n_sub):
    it = iter(refs)
    proj_refs = tuple(next(it) for _ in range(4))
    S0_ref = next(it)
    const_refs = tuple(next(it) for _ in range(N_ODD_CONSTS))
    y_ref, S_ref = next(it), next(it)
    S_ref[...] = S0_ref[...]

    def step(c, carry):
        _odd_chunk(pl.multiple_of(c * L, L), 0, c, L, proj_refs, const_refs, y_ref, S_ref)
        return carry

    lax.fori_loop(0, n_sub, step, 0)


def _odd_sample(proj, S0, cos, sin, consts, *, n_sub):
    rows = proj[0].shape[0]
    L = SEQ_PAD
    n_seq = rows // L
    tt = L * n_sub
    rmap = lambda o: (o, 0)
    cmap = lambda o: (0, 0)
    st_spec = pl.BlockSpec((n_sub, C_HEADS, C_DK, C_DV), lambda o: (o, 0, 0, 0))
    in_specs = [pl.BlockSpec((tt, a.shape[1]), rmap) for a in proj] + [st_spec]
    for c in (cos, sin) + tuple(consts):
        in_specs.append(pl.BlockSpec(c.shape, (lambda o: (0, 0, 0)) if c.ndim == 3 else cmap))
    return pl.pallas_call(
        functools.partial(_odd_sample_body, L=L, n_sub=n_sub),
        grid=(n_seq // n_sub,), in_specs=in_specs,
        out_specs=[pl.BlockSpec((tt, C_WIDTH), rmap), st_spec],
        out_shape=[jax.ShapeDtypeStruct((rows, C_WIDTH), f32), jax.ShapeDtypeStruct(S0.shape, f32)],
        compiler_params=_params(("arbitrary",)),
        name="odd_sample",
    )(*proj, S0, cos, sin, *consts)


def _pad_lanes(a, width=LANES):
    return jnp.pad(a, [(0, 0)] * (a.ndim - 1) + [(0, width - a.shape[-1])])


def _even_w_in_cols(w):
    sizes = [A_HEADS * A_DK, A_HEADS * A_DK, A_WIDTH, A_HEADS, A_HEADS, A_WIDTH, B_WIDTH, B_CONV_DIM, B_HEADS]
    q, k, v, ig, fg, og, z, xbc, dt = jnp.split(w, np.cumsum(sizes)[:-1].tolist(), axis=-1)
    return jnp.concatenate([q, k, v, og, z, xbc, _pad_lanes(ig), _pad_lanes(fg), _pad_lanes(dt)], axis=-1).astype(bf16)


def _retention_tables(L, n_pad):
    log_gamma = np.log1p(-np.exp2(-5.0 - np.arange(C_HEADS, dtype=np.float64)))
    t_real = L - n_pad
    idx = np.arange(L, dtype=np.float64) - n_pad
    real = idx >= 0
    diff = idx[:, None] - idx[None, :]
    intra = np.where((diff >= 0) & real[None, :], np.exp(log_gamma[:, None, None] * np.maximum(diff, 0.0)), 0.0)
    cross = np.where(real[:, None], np.exp(log_gamma[None, :] * (idx[:, None] + 1.0)), 0.0)
    into = np.where(real[:, None], np.exp(log_gamma[None, :] * (t_real - 1.0 - idx[:, None])), 0.0)
    cdec = np.exp(log_gamma * t_real)[None, :]
    lanes = lambda a: np.pad(a, ((0, 0), (0, LANES - a.shape[1]))).astype(np.float32)
    return intra.astype(np.float32), lanes(cross), lanes(into), lanes(cdec)


def _rope_tables(pos):
    half = C_DK // 2
    inv = ROPE_BASE ** (-np.arange(half, dtype=np.float64) / half)
    ang = np.asarray(pos, dtype=np.float64)[:, None] * inv[None, :]
    cos, sin = np.cos(ang), np.sin(ang)
    return (np.concatenate([cos, cos], axis=-1).astype(np.float32),
            np.concatenate([-sin, sin], axis=-1).astype(np.float32))


def kernel(x_prompt, x_sample, state_mlstm_C, state_mlstm_n, state_mlstm_m, state_ssd_conv, state_ssd_h, state_ret_S, state_ffn_conv, norm_mix_g, norm_ffn_g, norm_final_g, even_w_in, mlstm_igate_b, mlstm_fgate_b, mlstm_norm_g, ssd_conv_w, ssd_conv_b, ssd_dt_bias, ssd_A_log, ssd_D, ssd_norm_g, even_w_out, odd_w_in, ret_norm_g, odd_w_out, ffn_w_up, ffn_conv_w, ffn_conv_b, ffn_w_down):
    bsz, seq, d = x_prompt.shape
    dbsz, dseq, _ = x_sample.shape
    n_pad = SEQ_PAD - dseq
    assert norm_mix_g.shape[0] == 2 and B_CONV - 1 <= n_pad < SEQ_PAD and seq % (4 * CHUNK) == 0

    row = lambda a: a.reshape(1, -1)
    gate_bias = jnp.concatenate([_pad_lanes(row(mlstm_igate_b[0])), _pad_lanes(row(mlstm_fgate_b[0])),
                                 _pad_lanes(row(ssd_dt_bias[0]))], axis=0)
    even_consts = (gate_bias, row(mlstm_norm_g[0]), ssd_conv_w[0], row(ssd_conv_b[0]), _pad_lanes(row(ssd_A_log[0])),
                   row(jnp.repeat(ssd_D[0], B_HEADDIM)), row(ssd_norm_g[0]))
    g_mix0, g_mix1 = row(norm_mix_g[0]), row(norm_mix_g[1])
    g_ffn0, g_ffn1, g_final = row(norm_ffn_g[0]), row(norm_ffn_g[1]), row(norm_final_g)
    w_even_in, w_even_out = _even_w_in_cols(even_w_in[0]), even_w_out[0].astype(bf16)
    w_odd_in, w_odd_out = odd_w_in[0].astype(bf16), odd_w_out[0].astype(bf16)
    w_up, w_dn = ffn_w_up.astype(bf16), ffn_w_down.astype(bf16)
    ffn_cb = ffn_conv_b[:, None, :]
    ret_norm = row(ret_norm_g[0])
    cw = D_FF

    cos_p, sin_p = _rope_tables(np.arange(seq))
    xp = x_prompt.reshape(bsz * seq, d)
    xp, pC, pn, pm, pST, p_conv = _even_prompt(xp, g_mix0, w_even_in, even_consts, w_even_out,
                                               seq_rows=seq, L=CHUNK, n_sub=2)
    xp, p_f0 = _ffn(xp, None, g_ffn0, w_up[0], ffn_conv_w[0], ffn_cb[0], w_dn[0], None,
                    seq_rows=seq, tt=512, cw=cw, n_pad=0)
    xp, pSr = _odd_prompt(xp, g_mix1, w_odd_in, cos_p, sin_p, _retention_tables(CHUNK, 0) + (ret_norm,), w_odd_out,
                          seq_rows=seq, L=CHUNK, n_sub=2)
    _, p_f1, yp = _ffn(xp, None, g_ffn1, w_up[1], ffn_conv_w[1], ffn_cb[1], w_dn[1], g_final,
                       seq_rows=seq, tt=512, cw=cw, n_pad=0)
    pS = pST.reshape(bsz, B_STATE, B_HEADS, B_HEADDIM).transpose(0, 2, 3, 1)

    xs = jnp.pad(x_sample, ((0, 0), (n_pad, 0), (0, 0))).reshape(dbsz * SEQ_PAD, d)
    cos_s, sin_s = _rope_tables(PAST_LEN + np.arange(SEQ_PAD) - n_pad)
    hist = lambda a: jnp.pad(a, ((0, 0), (n_pad - a.shape[1], dseq), (0, 0))).reshape(dbsz * SEQ_PAD, a.shape[-1])
    state = (hist(state_ssd_conv[0]), state_mlstm_C[0],
             jnp.pad(state_mlstm_n[0], ((0, 0), (0, SUBLANES - A_HEADS), (0, 0))),
             _pad_lanes(state_mlstm_m[0])[:, None, :], state_ssd_h[0])
    tm = 256
    proj = _norm_proj(xs, g_mix0, w_even_in, EVEN_SPLITS, tm)
    y, sC, sn, sm, sS, s_conv = _even_sample(proj, state, even_consts, n_sub=4, n_pad=n_pad)
    xs = _proj_res(xs, y, w_even_out, tm)
    xs, s_f0 = _ffn(xs, hist(state_ffn_conv[0]), g_ffn0, w_up[0], ffn_conv_w[0], ffn_cb[0], w_dn[0], None,
                    seq_rows=SEQ_PAD, tt=128, cw=cw, n_pad=n_pad)
    proj = _norm_proj(xs, g_mix1, w_odd_in, ODD_SPLITS, tm)
    y, sSr = _odd_sample(proj, state_ret_S[0], cos_s, sin_s, _retention_tables(SEQ_PAD, n_pad) + (ret_norm,), n_sub=4)
    xs = _proj_res(xs, y, w_odd_out, tm)
    _, s_f1, ys = _ffn(xs, hist(state_ffn_conv[1]), g_ffn1, w_up[1], ffn_conv_w[1], ffn_cb[1], w_dn[1], g_final,
                       seq_rows=SEQ_PAD, tt=128, cw=cw, n_pad=n_pad)

    def pack(bs, C, n, m, S, conv_tail, Sr, f0, f1):
        conv = conv_tail.reshape(bs, SUBLANES, B_CONV_DIM)[:, SUBLANES - (B_CONV - 1):, :]
        ff = jnp.stack([f0.reshape(bs, SUBLANES, 2 * D_FF), f1.reshape(bs, SUBLANES, 2 * D_FF)])
        ff = ff[:, :, SUBLANES - (FFN_CONV - 1):, :]
        return (C[None], n[None, :, :A_HEADS, :], m[None, :, 0, :A_HEADS], conv[None], S[None], Sr[None], ff)

    y_prompt = yp.reshape(bsz, seq, d)
    y_sample = ys.reshape(dbsz, SEQ_PAD, d)[:, n_pad:, :]
    return ((y_prompt, y_sample) + pack(bsz, pC, pn, pm, pS, p_conv, pSr, p_f0, p_f1)
            + pack(dbsz, sC, sn, sm, sS, s_conv, sSr, s_f0, s_f1))
```

```python
import functools

import numpy as np
import jax
import jax.numpy as jnp
from jax import lax
from jax.experimental import pallas as pl
from jax.experimental.pallas import tpu as pltpu

f32 = jnp.float32
bf16 = jnp.bfloat16

EPS = 1e-6
CHUNK = 128
D_MODEL = 1024
A_HEADS, A_DK, A_DV = 4, 128, 256
A_WIDTH = A_HEADS * A_DV
B_HEADS, B_HEADDIM, B_GROUPS, B_STATE, B_CONV = 16, 64, 2, 128, 4
B_WIDTH = B_HEADS * B_HEADDIM
B_CONV_DIM = B_WIDTH + 2 * B_GROUPS * B_STATE
B_HEADS_PER_GROUP = B_HEADS // B_GROUPS
B_GROUP_WIDTH = B_WIDTH // B_GROUPS
C_HEADS, C_DK, C_DV = 8, 128, 256
C_WIDTH = C_HEADS * C_DV
ROPE_BASE = 10000.0
D_FF = 2816
FFN_CONV = 3
PAST_LEN = 16384

LANES = 128
SUBLANES = 8
SEQ_PAD = SUBLANES
NEG_BIG = -1e30
VMEM_LIMIT = 56 * 1024 * 1024

EVEN_SPLITS = (A_HEADS * A_DK, A_HEADS * A_DK, A_WIDTH, A_WIDTH, B_WIDTH, B_CONV_DIM, 3 * LANES)
ODD_SPLITS = (C_HEADS * C_DK, C_HEADS * C_DK, C_WIDTH, C_WIDTH)


def _params(sem):
    return pltpu.CompilerParams(dimension_semantics=sem, vmem_limit_bytes=VMEM_LIMIT)


def _resident(shape, index_map):
    return pl.BlockSpec(shape, index_map, pipeline_mode=pl.Buffered(1))


def _nt(a, b):
    return lax.dot_general(a, b, (((1,), (1,)), ((), ())), preferred_element_type=f32)


def _tn(a, b):
    return lax.dot_general(a, b, (((0,), (0,)), ((), ())), preferred_element_type=f32)


def _nn(a, b):
    return jnp.dot(a, b, preferred_element_type=f32)


def _rmsnorm(x, g):
    return x * lax.rsqrt(jnp.mean(x * x, axis=-1, keepdims=True) + EPS) * g


def _layernorm_nogain(h):
    mu = jnp.mean(h, axis=-1, keepdims=True)
    hc = h - mu
    return hc * lax.rsqrt(jnp.mean(hc * hc, axis=-1, keepdims=True) + EPS)


def _softplus(x):
    return jnp.maximum(x, 0.0) + jnp.log1p(jnp.exp(-jnp.abs(x)))


def _silu(x):
    return x * jax.nn.sigmoid(x)


def _split3(x):
    hi = x.astype(bf16)
    r1 = x - hi.astype(f32)
    mid = r1.astype(bf16)
    lo = (r1 - mid.astype(f32)).astype(bf16)
    return hi, mid, lo


def _cumsum_rows(tril, x):
    hi, mid, lo = _split3(x)
    return _nn(tril, hi) + _nn(tril, mid) + _nn(tril, lo)


def _split_views(ref, splits):
    views, off = [], 0
    for n in splits:
        views.append(ref.at[:, off:off + n])
        off += n
    return views


def _project(h, w_ref, p_ref, splits):
    off = 0
    for n in splits:
        p_ref[:, off:off + n] = _nn(h, w_ref[:, off:off + n])
        off += n


MXU_COLS = 256


def _proj_pieces(h_ref, w_ref, dst_ref):
    def piece(c0, n):
        def run():
            dst_ref[:, c0:c0 + n] = _nn(h_ref[...], w_ref[:, c0:c0 + n])
        return run
    total = w_ref.shape[1]
    return [piece(c0, min(MXU_COLS, total - c0)) for c0 in range(0, total, MXU_COLS)]


def _out_pieces(x_ref, y_ref, w_ref, o_ref, rows):
    def piece(c0, n):
        def run():
            o_ref[rows, c0:c0 + n] = x_ref[rows, c0:c0 + n] + _nn(y_ref[...], w_ref[:, c0:c0 + n])
        return run
    total = w_ref.shape[1]
    return [piece(c0, min(MXU_COLS, total - c0)) for c0 in range(0, total, MXU_COLS)]


def _spread(work, n_slots):
    return [work[i * len(work) // n_slots:(i + 1) * len(work) // n_slots] for i in range(n_slots)]


def _norm_proj_body(x_ref, g_ref, w_ref, *o_refs, splits):
    h = _rmsnorm(x_ref[...], g_ref[...]).astype(bf16)
    off = 0
    for o_ref, n in zip(o_refs, splits):
        o_ref[...] = _nn(h, w_ref[:, off:off + n])
        off += n


def _norm_proj(x, g, w, splits, tm):
    rows, d = x.shape
    return pl.pallas_call(
        functools.partial(_norm_proj_body, splits=splits),
        grid=(rows // tm,),
        in_specs=[pl.BlockSpec((tm, d), lambda i: (i, 0)),
                  pl.BlockSpec((1, d), lambda i: (0, 0)),
                  _resident(w.shape, lambda i: (0, 0))],
        out_specs=[pl.BlockSpec((tm, n), lambda i: (i, 0)) for n in splits],
        out_shape=[jax.ShapeDtypeStruct((rows, n), f32) for n in splits],
        compiler_params=_params(("arbitrary",)),
        name="norm_proj",
    )(x, g, w)


def _proj_res_body(x_ref, y_ref, w_ref, o_ref):
    o_ref[...] = x_ref[...] + _nn(y_ref[...].astype(bf16), w_ref[...])


def _proj_res(x, y, w, tm):
    rows, d = x.shape
    k = y.shape[1]
    return pl.pallas_call(
        _proj_res_body,
        grid=(rows // tm,),
        in_specs=[pl.BlockSpec((tm, d), lambda i: (i, 0)),
                  pl.BlockSpec((tm, k), lambda i: (i, 0)),
                  _resident(w.shape, lambda i: (0, 0))],
        out_specs=pl.BlockSpec((tm, d), lambda i: (i, 0)),
        out_shape=jax.ShapeDtypeStruct((rows, d), f32),
        compiler_params=_params(("arbitrary",)),
        name="proj_res",
    )(x, y, w)


def _ffn_body(*refs, tt, cw, n_pad, sample, final):
    it = iter(refs)
    x_ref = next(it)
    buf_ref = next(it) if sample else None
    g_ref, wup_ref, cw_ref, cb_ref, wdn_ref = next(it), next(it), next(it), next(it), next(it)
    gf_ref = next(it) if final else None
    o_ref, tail_ref = next(it), next(it)
    yn_ref = next(it) if final else None
    scr = next(it)
    carry = None if sample else next(it)

    if not sample:
        @pl.when(pl.program_id(1) == 0)
        def _():
            carry[...] = jnp.zeros(carry.shape, f32)
    else:
        row = lax.broadcasted_iota(jnp.int32, (tt, 1), 0) % SEQ_PAD
        is_hist = (row >= n_pad - (FFN_CONV - 1)) & (row < n_pad)

    x = x_ref[...]
    h = _rmsnorm(x, g_ref[...]).astype(bf16)
    acc = jnp.zeros((tt, D_MODEL), f32)
    for j in range(D_FF // cw):
        conv = []
        for part in range(2):
            c0 = part * D_FF + j * cw
            u = _nn(h, wup_ref[:, c0:c0 + cw])
            if sample:
                u = jnp.where(is_hist, buf_ref[:, c0:c0 + cw], u)
                scr[0:SUBLANES, :] = jnp.zeros((SUBLANES, cw), f32)
                tail_ref[:, c0:c0 + cw] = u
            else:
                scr[0:SUBLANES, :] = carry[:, c0:c0 + cw]
                tail_ref[:, c0:c0 + cw] = u[tt - SUBLANES:tt, :]
                carry[:, c0:c0 + cw] = u[tt - SUBLANES:tt, :]
            scr[SUBLANES:SUBLANES + tt, :] = u
            y = cb_ref[:, c0:c0 + cw] + scr[SUBLANES - 2:SUBLANES - 2 + tt, :] * cw_ref[0:1, c0:c0 + cw]
            y = y + scr[SUBLANES - 1:SUBLANES - 1 + tt, :] * cw_ref[1:2, c0:c0 + cw]
            y = y + u * cw_ref[2:3, c0:c0 + cw]
            conv.append(y)
        act = (_silu(conv[0]) * conv[1]).astype(bf16)
        acc = acc + _nn(act, wdn_ref[j * cw:(j + 1) * cw, :])
    out = x + acc
    o_ref[...] = out
    if final:
        yn_ref[...] = _rmsnorm(out, gf_ref[...])


def _ffn(x, hist, g, wup, conv_w, conv_b, wdn, g_final, *, seq_rows, tt, cw, n_pad):
    rows, d = x.shape
    sample = hist is not None
    final = g_final is not None
    n_seq = rows // seq_rows
    if sample:
        grid = (rows // tt,)
        rmap = lambda i: (i, 0)
        cmap = lambda i: (0, 0)
        tail_spec = pl.BlockSpec((tt, 2 * D_FF), rmap)
        sem = ("arbitrary",)
    else:
        n_inner = seq_rows // tt
        grid = (n_seq, n_inner)
        rmap = lambda s, t: (s * n_inner + t, 0)
        cmap = lambda s, t: (0, 0)
        tail_spec = pl.BlockSpec((SUBLANES, 2 * D_FF), lambda s, t: (s, 0))
        sem = ("arbitrary", "arbitrary")
    in_specs = [pl.BlockSpec((tt, d), rmap)]
    args = [x]
    if sample:
        in_specs.append(pl.BlockSpec((tt, 2 * D_FF), rmap))
        args.append(hist)
    in_specs += [pl.BlockSpec((1, d), cmap), _resident(wup.shape, cmap), pl.BlockSpec(conv_w.shape, cmap),
                 pl.BlockSpec(conv_b.shape, cmap), _resident(wdn.shape, cmap)]
    args += [g, wup, conv_w, conv_b, wdn]
    out_specs = [pl.BlockSpec((tt, d), rmap), tail_spec]
    out_shape = [jax.ShapeDtypeStruct((rows, d), f32), jax.ShapeDtypeStruct((n_seq * SUBLANES, 2 * D_FF), f32)]
    if final:
        in_specs.append(pl.BlockSpec((1, d), cmap))
        args.append(g_final)
        out_specs.append(pl.BlockSpec((tt, d), rmap))
        out_shape.append(jax.ShapeDtypeStruct((rows, d), f32))
    scratch = [pltpu.VMEM((SUBLANES + tt, cw), f32)]
    if not sample:
        scratch.append(pltpu.VMEM((SUBLANES, 2 * D_FF), f32))
    return pl.pallas_call(
        functools.partial(_ffn_body, tt=tt, cw=cw, n_pad=n_pad, sample=sample, final=final),
        grid=grid, in_specs=in_specs, out_specs=out_specs, out_shape=out_shape,
        scratch_shapes=scratch, compiler_params=_params(sem),
        name="ffn_sample" if sample else "ffn_prompt",
    )(*args)


EVEN_SLOTS = A_HEADS + B_HEADS // 2


def _even_chunk(r0, slot, L, n_pad, n_seg, proj_refs, hist_ref, const_refs, y_ref, state_in_refs, state_refs,
                tail_ref, conv_scr, yb_scr, side=None):
    q_ref, k_ref, v_ref, og_ref, z_ref, xbc_ref, gt_ref = proj_refs
    gb_ref, anorm_ref, convw_ref, convb_ref, alog_ref, dx_ref, bnorm_ref = const_refs
    C_in, n_in, m_in, S_in = state_in_refs
    C_ref, n_ref, m_ref, S_ref = state_refs
    rows = pl.ds(r0, L)
    seg = L // n_seg
    seg_rows = [slice(i * seg, (i + 1) * seg) for i in range(n_seg)]
    seg_last = [slice((i + 1) * seg - 1, (i + 1) * seg) for i in range(n_seg)]

    def per_row(vals):
        if n_seg == 1:
            return vals[0]
        return jnp.concatenate([jnp.broadcast_to(v, (seg, v.shape[1])) for v in vals], axis=0)

    ri = lax.broadcasted_iota(jnp.int32, (L, L), 0)
    ci = lax.broadcasted_iota(jnp.int32, (L, L), 1)
    causal = ri >= ci
    if n_seg > 1:
        causal = causal & ((ri // seg) == (ci // seg))
    tril = causal.astype(bf16)
    valid = None
    if n_pad:
        valid = (lax.broadcasted_iota(jnp.int32, (L, 1), 0) % seg) >= n_pad

    gates = gt_ref[rows, :]
    li = gates[:, 0:LANES] + gb_ref[0:1, :]
    fpre = gates[:, LANES:2 * LANES] + gb_ref[1:2, :]
    lf = -_softplus(-fpre)
    dt = _softplus(gates[:, 2 * LANES:3 * LANES] + gb_ref[2:3, :])
    if n_pad:
        li = jnp.where(valid, li, NEG_BIG)
        lf = jnp.where(valid, lf, 0.0)
        dt = jnp.where(valid, dt, 0.0)
    a = dt * (-jnp.exp(alog_ref[...]))
    cums = _cumsum_rows(tril, jnp.concatenate([lf, a], axis=1))
    bcum = cums[:, 0:LANES]
    acum = cums[:, LANES:2 * LANES]
    liT, bT, aT, dtT = li.T, bcum.T, acum.T, dt.T

    m_old = [m_in[slot + i] for i in range(n_seg)]
    b_lasts = [bcum[r, :] for r in seg_last]
    m_rows, b_last = per_row(m_old), per_row(b_lasts)
    log_g = b_last - bcum + li
    m_news = [jnp.maximum(b_lasts[i] + m_old[i], jnp.max(log_g[seg_rows[i], :], axis=0, keepdims=True))
              for i in range(n_seg)]
    m_new = per_row(m_news)
    gfac = jnp.exp(log_g - m_new)
    cdecay = jnp.exp(b_last + m_rows - m_new)
    for i in range(n_seg):
        m_ref[slot + i] = m_news[i]
    for h in range(A_HEADS):
        for run in (side[h] if side else ()):
            run()
        bcol, brow, lirow = bcum[:, h:h + 1], bT[h:h + 1, :], liT[h:h + 1, :]
        m_h = m_rows[:, h:h + 1]
        logw = jnp.where(causal, bcol - brow + lirow, -jnp.inf)
        log_prev = bcol + m_h
        m_t = jnp.maximum(log_prev, jnp.max(logw, axis=-1, keepdims=True))
        w_in = jnp.exp(logw - m_t)
        w_prev = jnp.exp(log_prev - m_t)
        qh = q_ref[rows, h * A_DK:(h + 1) * A_DK]
        kh = k_ref[rows, h * A_DK:(h + 1) * A_DK] * (A_DK ** -0.5)
        vh = v_ref[rows, h * A_DV:(h + 1) * A_DV]
        vb = vh.astype(bf16)
        qb = qh.astype(bf16)
        s = _nt(qb, kh.astype(bf16)) * w_in
        kg = kh * gfac[:, h:h + 1]
        Cs = [C_in[slot + i, h] for i in range(n_seg)]
        ns = [n_in[slot + i, h:h + 1, :] for i in range(n_seg)]
        if n_seg == 1:
            inter = _nn(qb, Cs[0].astype(bf16))
            dec_h = cdecay[:, h:h + 1]
            C_ref[slot, h] = Cs[0] * dec_h + _tn(kg.astype(bf16), vb)
            n_ref[slot, h:h + 1, :] = ns[0] * dec_h + jnp.sum(kg, axis=0, keepdims=True)
        else:
            inter = jnp.concatenate([_nn(qh[seg_rows[i], :].astype(bf16), Cs[i].astype(bf16))
                                     for i in range(n_seg)], axis=0)
            for i in range(n_seg):
                rs = seg_rows[i]
                dec_h = cdecay[seg_last[i], h:h + 1]
                C_ref[slot + i, h] = Cs[i] * dec_h + _tn(kg[rs, :].astype(bf16), vh[rs, :].astype(bf16))
                n_ref[slot + i, h:h + 1, :] = ns[i] * dec_h + jnp.sum(kg[rs, :], axis=0, keepdims=True)
        num = _nn(s.astype(bf16), vb) + inter * w_prev
        den = jnp.sum(s, axis=-1, keepdims=True) + jnp.sum(qh * per_row(ns), axis=-1, keepdims=True) * w_prev
        hout = num / jnp.maximum(jnp.abs(den), jnp.exp(-m_t))
        ya = _layernorm_nogain(hout) * anorm_ref[:, h * A_DV:(h + 1) * A_DV]
        ya = ya * jax.nn.sigmoid(og_ref[rows, h * A_DV:(h + 1) * A_DV])
        y_ref[rows, h * A_DV:(h + 1) * A_DV] = ya.astype(y_ref.dtype)

    xraw = xbc_ref[rows, :]
    if hist_ref is not None:
        seg_row = lax.broadcasted_iota(jnp.int32, (L, 1), 0) % seg
        is_hist = (seg_row >= n_pad - (B_CONV - 1)) & jnp.logical_not(valid)
        xraw = jnp.where(is_hist, hist_ref[rows, :], xraw)
    conv_scr[SUBLANES:SUBLANES + L, :] = xraw
    xc = convb_ref[...] + conv_scr[SUBLANES - 3:SUBLANES - 3 + L, :] * convw_ref[0:1, :]
    xc = xc + conv_scr[SUBLANES - 2:SUBLANES - 2 + L, :] * convw_ref[1:2, :]
    xc = xc + conv_scr[SUBLANES - 1:SUBLANES - 1 + L, :] * convw_ref[2:3, :]
    xc = xc + xraw * convw_ref[3:4, :]
    if n_seg == 1:
        new_tail = conv_scr[L:L + SUBLANES, :]
        tail_ref[...] = new_tail
        conv_scr[0:SUBLANES, :] = new_tail
    else:
        tail_ref[...] = xraw
    xc = _silu(xc)

    a_last = per_row([acum[r, :] for r in seg_last])
    wtile = jnp.exp(a_last - acum) * dt
    expa = jnp.exp(acum)

    def head_matrix(cb, h):
        acol, arow, dtrow = acum[:, h:h + 1], aT[h:h + 1, :], dtT[h:h + 1, :]
        decay = jnp.where(causal, jnp.exp(jnp.where(causal, acol - arow, 0.0)), 0.0)
        return (cb * decay * dtrow).astype(bf16)

    def group_bc(g):
        Bg = xc[:, B_WIDTH + g * B_STATE:B_WIDTH + (g + 1) * B_STATE].astype(bf16)
        c0 = B_WIDTH + B_GROUPS * B_STATE + g * B_STATE
        Cg = xc[:, c0:c0 + B_STATE].astype(bf16)
        return Bg, Cg, _nt(Cg, Bg)

    low_half = lax.broadcasted_iota(jnp.int32, (L, LANES), 1) < B_HEADDIM
    first_head_rows = lax.broadcasted_iota(jnp.int32, (2 * B_HEADDIM, 1), 0) < B_HEADDIM

    def pair_lanes(t, j):
        return jnp.where(low_half, jnp.broadcast_to(t[:, 2 * j:2 * j + 1], (L, LANES)),
                         jnp.broadcast_to(t[:, 2 * j + 1:2 * j + 2], (L, LANES)))

    pairs_per_group = B_HEADS_PER_GROUP // 2
    for g in range(B_GROUPS):
        Bg, Cg, cb = group_bc(g)
        gs = slice(g * B_GROUP_WIDTH, (g + 1) * B_GROUP_WIDTH)
        if n_seg == 1:
            ST = S_in[slot, :, gs]
            inter_g = _nn(Cg, ST.astype(bf16))
        else:
            Bf = xc[:, B_WIDTH + g * B_STATE:B_WIDTH + (g + 1) * B_STATE]
            c0 = B_WIDTH + B_GROUPS * B_STATE + g * B_STATE
            Cf = xc[:, c0:c0 + B_STATE]
        xw, decs = [], []
        for jp in range(pairs_per_group):
            j = g * pairs_per_group + jp
            for run in (side[A_HEADS + j] if side else ()):
                run()
            ps = slice(j * LANES, (j + 1) * LANES)
            e_pair, w_pair = pair_lanes(expa, j), pair_lanes(wtile, j)
            mcat = jnp.concatenate([head_matrix(cb, 2 * j), head_matrix(cb, 2 * j + 1)], axis=1)
            xp = xc[:, ps]
            xbd = jnp.concatenate([jnp.where(low_half, xp, 0.0).astype(bf16),
                                   jnp.where(low_half, 0.0, xp).astype(bf16)], axis=0)
            if n_seg == 1:
                inter = inter_g[:, jp * LANES:(jp + 1) * LANES]
                xw.append((xp * w_pair).astype(bf16))
                decs.append(e_pair[L - 1:L, :])
            else:
                xwf = xp * w_pair
                parts = []
                for i in range(n_seg):
                    rs = seg_rows[i]
                    Sp = S_in[slot + i, j]
                    parts.append(_nt(Cf[rs, :].astype(bf16), Sp.astype(bf16)))
                    e_last = e_pair[seg_last[i], :]
                    dec_col = jnp.where(first_head_rows, e_last[:, 0:1], e_last[:, B_HEADDIM:B_HEADDIM + 1])
                    S_ref[slot + i, j] = Sp * dec_col + _tn(xwf[rs, :].astype(bf16), Bf[rs, :].astype(bf16))
                inter = jnp.concatenate(parts, axis=0)
            yb_scr[:, ps] = (_nn(mcat, xbd) + inter * e_pair) + dx_ref[:, ps] * xp
        if n_seg == 1:
            S_ref[slot, :, gs] = ST * jnp.concatenate(decs, axis=1) + _tn(Bg, jnp.concatenate(xw, axis=1))
    yb = yb_scr[...] * _silu(z_ref[rows, :])
    for g in range(B_GROUPS):
        gs = slice(g * B_GROUP_WIDTH, (g + 1) * B_GROUP_WIDTH)
        yg = yb[:, gs]
        yg = yg * lax.rsqrt(jnp.mean(yg * yg, axis=-1, keepdims=True) + EPS)
        y_ref[rows, A_WIDTH + g * B_GROUP_WIDTH:A_WIDTH + (g + 1) * B_GROUP_WIDTH] = (
            yg * bnorm_ref[:, gs]).astype(y_ref.dtype)


N_EVEN_CONSTS = 7


def _even_prompt_body(*refs, L, n_sub):
    it = iter(refs)
    x_ref, xnext_ref, gmix_ref, win_ref = next(it), next(it), next(it), next(it)
    const_refs = tuple(next(it) for _ in range(N_EVEN_CONSTS))
    wout_ref = next(it)
    o_ref = next(it)
    state_refs = tuple(next(it) for _ in range(4))
    tail_ref = next(it)
    p_scr, y_scr = (next(it), next(it)), (next(it), next(it))
    h_scr, conv_scr, yb_scr = (next(it) for _ in range(3))
    th = L * n_sub
    gmix = gmix_ref[...]

    @pl.when((pl.program_id(0) == 0) & (pl.program_id(1) == 0))
    def _():
        _project(_rmsnorm(x_ref[0:th, :], gmix).astype(bf16), win_ref, p_scr[0], EVEN_SPLITS)

    @pl.when(pl.program_id(1) == 0)
    def _():
        conv_scr[0:SUBLANES, :] = jnp.zeros((SUBLANES, B_CONV_DIM), f32)
        for r in state_refs:
            r[...] = jnp.zeros(r.shape, f32)

    for half in range(2):
        ahead = xnext_ref[...] if half else x_ref[th:2 * th, :]
        h_scr[...] = _rmsnorm(ahead, gmix).astype(bf16)
        work = _proj_pieces(h_scr, win_ref, p_scr[1 - half])
        if half:
            work = work + _out_pieces(x_ref, y_scr[0], wout_ref, o_ref, slice(0, th))
        side = _spread(work, n_sub * EVEN_SLOTS)
        proj_refs = _split_views(p_scr[half], EVEN_SPLITS)
        for c in range(n_sub):
            _even_chunk(c * L, 0, L, 0, 1, proj_refs, None, const_refs, y_scr[half], state_refs, state_refs,
                        tail_ref, conv_scr, yb_scr, side[c * EVEN_SLOTS:(c + 1) * EVEN_SLOTS])
    for run in _out_pieces(x_ref, y_scr[1], wout_ref, o_ref, slice(th, 2 * th)):
        run()


def _even_prompt(x, gmix, win, consts, wout, *, seq_rows, L, n_sub):
    rows, d = x.shape
    n_seq = rows // seq_rows
    th = L * n_sub
    tt = 2 * th
    n_inner = seq_rows // tt
    n_half = rows // th
    rmap = lambda s, t: (s * n_inner + t, 0)
    nextmap = lambda s, t: (jnp.minimum(2 * (s * n_inner + t) + 2, n_half - 1), 0)
    cmap = lambda s, t: (0, 0)
    smap4 = lambda s, t: (s, 0, 0, 0)
    smap3 = lambda s, t: (s, 0, 0)
    st_specs = [pl.BlockSpec((1, A_HEADS, A_DK, A_DV), smap4), pl.BlockSpec((1, SUBLANES, A_DK), smap3),
                pl.BlockSpec((1, 1, LANES), smap3), pl.BlockSpec((1, B_STATE, B_WIDTH), smap3)]
    st_shapes = [jax.ShapeDtypeStruct((n_seq, A_HEADS, A_DK, A_DV), f32),
                 jax.ShapeDtypeStruct((n_seq, SUBLANES, A_DK), f32),
                 jax.ShapeDtypeStruct((n_seq, 1, LANES), f32),
                 jax.ShapeDtypeStruct((n_seq, B_STATE, B_WIDTH), f32)]
    in_specs = ([pl.BlockSpec((tt, d), rmap), pl.BlockSpec((th, d), nextmap), pl.BlockSpec((1, d), cmap),
                 _resident(win.shape, cmap)]
                + [pl.BlockSpec(c.shape, cmap) for c in consts] + [_resident(wout.shape, cmap)])
    p_shape = pltpu.VMEM((th, sum(EVEN_SPLITS)), f32)
    y_shape = pltpu.VMEM((th, A_WIDTH + B_WIDTH), bf16)
    return pl.pallas_call(
        functools.partial(_even_prompt_body, L=L, n_sub=n_sub),
        grid=(n_seq, n_inner), in_specs=in_specs,
        out_specs=[pl.BlockSpec((tt, d), rmap)] + st_specs + [pl.BlockSpec((SUBLANES, B_CONV_DIM), lambda s, t: (s, 0))],
        out_shape=[jax.ShapeDtypeStruct((rows, d), f32)] + st_shapes
        + [jax.ShapeDtypeStruct((n_seq * SUBLANES, B_CONV_DIM), f32)],
        scratch_shapes=[p_shape, p_shape, y_shape, y_shape, pltpu.VMEM((th, d), bf16),
                        pltpu.VMEM((SUBLANES + L, B_CONV_DIM), f32), pltpu.VMEM((L, B_WIDTH), f32)],
        compiler_params=_params(("arbitrary", "arbitrary")),
        name="even_prompt",
    )(x, x, gmix, win, *consts, wout)


def _even_sample_body(*refs, L, n_seg, n_pad):
    it = iter(refs)
    proj_refs = tuple(next(it) for _ in range(7))
    hist_ref = next(it)
    state_in_refs = tuple(next(it) for _ in range(4))
    const_refs = tuple(next(it) for _ in range(N_EVEN_CONSTS))
    y_ref = next(it)
    state_refs = tuple(next(it) for _ in range(4))
    tail_ref = next(it)
    conv_scr, yb_scr = next(it), next(it)

    conv_scr[0:SUBLANES, :] = jnp.zeros((SUBLANES, B_CONV_DIM), f32)
    n_ref = state_refs[1]
    n_ref[:, A_HEADS:, :] = jnp.zeros((n_seg, SUBLANES - A_HEADS, A_DK), f32)
    _even_chunk(0, 0, L, n_pad, n_seg, proj_refs, hist_ref, const_refs, y_ref, state_in_refs, state_refs,
                tail_ref, conv_scr, yb_scr)


def _even_sample(proj, state, consts, *, n_seg, n_pad):
    rows = proj[0].shape[0]
    L = SEQ_PAD * n_seg
    rmap = lambda o: (o, 0)
    cmap = lambda o: (0, 0)
    smap4 = lambda o: (o, 0, 0, 0)
    smap3 = lambda o: (o, 0, 0)
    st_specs = [pl.BlockSpec((n_seg, A_HEADS, A_DK, A_DV), smap4), pl.BlockSpec((n_seg, SUBLANES, A_DK), smap3),
                pl.BlockSpec((n_seg, 1, LANES), smap3),
                pl.BlockSpec((n_seg, B_HEADS // 2, 2 * B_HEADDIM, B_STATE), smap4)]
    st_shapes = [jax.ShapeDtypeStruct(s.shape, f32) for s in state[1:]]
    in_specs = ([pl.BlockSpec((L, a.shape[1]), rmap) for a in proj] + [pl.BlockSpec((L, B_CONV_DIM), rmap)]
                + st_specs + [pl.BlockSpec(c.shape, cmap) for c in consts])
    return pl.pallas_call(
        functools.partial(_even_sample_body, L=L, n_seg=n_seg, n_pad=n_pad),
        grid=(rows // L,), in_specs=in_specs,
        out_specs=[pl.BlockSpec((L, A_WIDTH + B_WIDTH), rmap)] + st_specs + [pl.BlockSpec((L, B_CONV_DIM), rmap)],
        out_shape=[jax.ShapeDtypeStruct((rows, A_WIDTH + B_WIDTH), f32)] + st_shapes
        + [jax.ShapeDtypeStruct((rows, B_CONV_DIM), f32)],
        scratch_shapes=[pltpu.VMEM((SUBLANES + L, B_CONV_DIM), f32), pltpu.VMEM((L, B_WIDTH), f32)],
        compiler_params=_params(("arbitrary",)),
        name="even_sample",
    )(*proj, *state, *consts)


def _odd_chunk(r0, rope_r0, slot, L, n_seg, proj_refs, const_refs, y_ref, S_in, S_ref, side=None):
    q_ref, k_ref, v_ref, g_ref = proj_refs
    cos_ref, sin_ref, intra_ref, cross_ref, into_ref, cdec_ref, norm_ref = const_refs
    rows = pl.ds(r0, L)
    seg = L // n_seg
    cosf = cos_ref[pl.ds(rope_r0, L), :]
    sinf = sin_ref[pl.ds(rope_r0, L), :]
    cross = cross_ref[...]
    into = into_ref[...]
    cdec = cdec_ref[...]
    for h in range(C_HEADS):
        for run in (side[h] if side else ()):
            run()
        qh = q_ref[rows, h * C_DK:(h + 1) * C_DK]
        kh = k_ref[rows, h * C_DK:(h + 1) * C_DK]
        qh = qh * cosf + pltpu.roll(qh, C_DK // 2, 1) * sinf
        kh = (kh * cosf + pltpu.roll(kh, C_DK // 2, 1) * sinf) * (C_DK ** -0.5)
        vh = v_ref[rows, h * C_DV:(h + 1) * C_DV]
        vb = vh.astype(bf16)
        qb = qh.astype(bf16)
        s = _nt(qb, kh.astype(bf16)) * intra_ref[h]
        kd = kh * into[:, h:h + 1]
        if n_seg == 1:
            S = S_in[slot, h]
            inter = _nn(qb, S.astype(bf16))
            S_ref[slot, h] = S * cdec[:, h:h + 1] + _tn(kd.astype(bf16), vb)
        else:
            parts = []
            for i in range(n_seg):
                rs = slice(i * seg, (i + 1) * seg)
                S = S_in[slot + i, h]
                parts.append(_nn(qh[rs, :].astype(bf16), S.astype(bf16)))
                S_ref[slot + i, h] = S * cdec[:, h:h + 1] + _tn(kd[rs, :].astype(bf16), vh[rs, :].astype(bf16))
            inter = jnp.concatenate(parts, axis=0)
        o = _nn(s.astype(bf16), vb) + inter * cross[:, h:h + 1]
        y = _layernorm_nogain(o) * norm_ref[:, h * C_DV:(h + 1) * C_DV]
        y = y * _silu(g_ref[rows, h * C_DV:(h + 1) * C_DV])
        y_ref[rows, h * C_DV:(h + 1) * C_DV] = y.astype(y_ref.dtype)


N_ODD_CONSTS = 7


def _odd_prompt_body(*refs, L, n_sub):
    it = iter(refs)
    x_ref, xnext_ref, gmix_ref, win_ref = next(it), next(it), next(it), next(it)
    const_refs = tuple(next(it) for _ in range(N_ODD_CONSTS))
    wout_ref = next(it)
    o_ref, S_ref = next(it), next(it)
    p_scr, y_scr = (next(it), next(it)), (next(it), next(it))
    h_scr = next(it)
    th = L * n_sub
    gmix = gmix_ref[...]

    @pl.when((pl.program_id(0) == 0) & (pl.program_id(1) == 0))
    def _():
        _project(_rmsnorm(x_ref[0:th, :], gmix).astype(bf16), win_ref, p_scr[0], ODD_SPLITS)

    @pl.when(pl.program_id(1) == 0)
    def _():
        S_ref[...] = jnp.zeros(S_ref.shape, f32)

    for half in range(2):
        ahead = xnext_ref[...] if half else x_ref[th:2 * th, :]
        h_scr[...] = _rmsnorm(ahead, gmix).astype(bf16)
        work = _proj_pieces(h_scr, win_ref, p_scr[1 - half])
        if half:
            work = work + _out_pieces(x_ref, y_scr[0], wout_ref, o_ref, slice(0, th))
        side = _spread(work, n_sub * C_HEADS)
        proj_refs = _split_views(p_scr[half], ODD_SPLITS)
        for c in range(n_sub):
            _odd_chunk(c * L, half * th + c * L, 0, L, 1, proj_refs, const_refs, y_scr[half], S_ref, S_ref,
                       side[c * C_HEADS:(c + 1) * C_HEADS])
    for run in _out_pieces(x_ref, y_scr[1], wout_ref, o_ref, slice(th, 2 * th)):
        run()


def _odd_prompt(x, gmix, win, cos, sin, consts, wout, *, seq_rows, L, n_sub):
    rows, d = x.shape
    n_seq = rows // seq_rows
    th = L * n_sub
    tt = 2 * th
    n_inner = seq_rows // tt
    n_half = rows // th
    rmap = lambda s, t: (s * n_inner + t, 0)
    nextmap = lambda s, t: (jnp.minimum(2 * (s * n_inner + t) + 2, n_half - 1), 0)
    cmap = lambda s, t: (0, 0)
    st_spec = pl.BlockSpec((1, C_HEADS, C_DK, C_DV), lambda s, t: (s, 0, 0, 0))
    rope_spec = pl.BlockSpec((tt, LANES), lambda s, t: (t, 0))
    in_specs = [pl.BlockSpec((tt, d), rmap), pl.BlockSpec((th, d), nextmap), pl.BlockSpec((1, d), cmap),
                _resident(win.shape, cmap), rope_spec, rope_spec]
    for c in consts:
        in_specs.append(pl.BlockSpec(c.shape, (lambda s, t: (0, 0, 0)) if c.ndim == 3 else cmap))
    in_specs.append(_resident(wout.shape, cmap))
    p_shape = pltpu.VMEM((th, sum(ODD_SPLITS)), f32)
    y_shape = pltpu.VMEM((th, C_WIDTH), bf16)
    return pl.pallas_call(
        functools.partial(_odd_prompt_body, L=L, n_sub=n_sub),
        grid=(n_seq, n_inner), in_specs=in_specs,
        out_specs=[pl.BlockSpec((tt, d), rmap), st_spec],
        out_shape=[jax.ShapeDtypeStruct((rows, d), f32), jax.ShapeDtypeStruct((n_seq, C_HEADS, C_DK, C_DV), f32)],
        scratch_shapes=[p_shape, p_shape, y_shape, y_shape, pltpu.VMEM((th, d), bf16)],
        compiler_params=_params(("arbitrary", "arbitrary")),
        name="odd_prompt",
    )(x, x, gmix, win, cos, sin, *consts, wout)


def _odd_sample_body(*refs, L, n_seg):
    it = iter(refs)
    proj_refs = tuple(next(it) for _ in range(4))
    S_in = next(it)
    const_refs = tuple(next(it) for _ in range(N_ODD_CONSTS))
    y_ref, S_ref = next(it), next(it)
    _odd_chunk(0, 0, 0, L, n_seg, proj_refs, const_refs, y_ref, S_in, S_ref)


def _odd_sample(proj, S0, cos, sin, consts, *, n_seg):
    rows = proj[0].shape[0]
    L = SEQ_PAD * n_seg
    rmap = lambda o: (o, 0)
    cmap = lambda o: (0, 0)
    st_spec = pl.BlockSpec((n_seg, C_HEADS, C_DK, C_DV), lambda o: (o, 0, 0, 0))
    in_specs = [pl.BlockSpec((L, a.shape[1]), rmap) for a in proj] + [st_spec]
    for c in (cos, sin) + tuple(consts):
        in_specs.append(pl.BlockSpec(c.shape, (lambda o: (0, 0, 0)) if c.ndim == 3 else cmap))
    return pl.pallas_call(
        functools.partial(_odd_sample_body, L=L, n_seg=n_seg),
        grid=(rows // L,), in_specs=in_specs,
        out_specs=[pl.BlockSpec((L, C_WIDTH), rmap), st_spec],
        out_shape=[jax.ShapeDtypeStruct((rows, C_WIDTH), f32), jax.ShapeDtypeStruct(S0.shape, f32)],
        compiler_params=_params(("arbitrary",)),
        name="odd_sample",
    )(*proj, S0, cos, sin, *consts)


def _pad_lanes(a, width=LANES):
    return jnp.pad(a, [(0, 0)] * (a.ndim - 1) + [(0, width - a.shape[-1])])


def _even_w_in_cols(w):
    sizes = [A_HEADS * A_DK, A_HEADS * A_DK, A_WIDTH, A_HEADS, A_HEADS, A_WIDTH, B_WIDTH, B_CONV_DIM, B_HEADS]
    q, k, v, ig, fg, og, z, xbc, dt = jnp.split(w, np.cumsum(sizes)[:-1].tolist(), axis=-1)
    return jnp.concatenate([q, k, v, og, z, xbc, _pad_lanes(ig), _pad_lanes(fg), _pad_lanes(dt)], axis=-1).astype(bf16)


def _retention_tables(seg, n_pad, n_seg=1):
    log_gamma = np.log1p(-np.exp2(-5.0 - np.arange(C_HEADS, dtype=np.float64)))
    t_real = seg - n_pad
    idx = np.arange(seg, dtype=np.float64) - n_pad
    real = idx >= 0
    diff = idx[:, None] - idx[None, :]
    intra = np.where((diff >= 0) & real[None, :], np.exp(log_gamma[:, None, None] * np.maximum(diff, 0.0)), 0.0)
    intra = np.stack([np.kron(np.eye(n_seg), intra[h]) for h in range(C_HEADS)])
    cross = np.where(real[:, None], np.exp(log_gamma[None, :] * (idx[:, None] + 1.0)), 0.0)
    into = np.where(real[:, None], np.exp(log_gamma[None, :] * (t_real - 1.0 - idx[:, None])), 0.0)
    cdec = np.exp(log_gamma * t_real)[None, :]
    lanes = lambda a: np.pad(a, ((0, 0), (0, LANES - a.shape[1]))).astype(np.float32)
    return intra.astype(np.float32), lanes(np.tile(cross, (n_seg, 1))), lanes(np.tile(into, (n_seg, 1))), lanes(cdec)


def _rope_tables(pos):
    half = C_DK // 2
    inv = ROPE_BASE ** (-np.arange(half, dtype=np.float64) / half)
    ang = np.asarray(pos, dtype=np.float64)[:, None] * inv[None, :]
    cos, sin = np.cos(ang), np.sin(ang)
    return (np.concatenate([cos, cos], axis=-1).astype(np.float32),
            np.concatenate([-sin, sin], axis=-1).astype(np.float32))


def kernel(x_prompt, x_sample, state_mlstm_C, state_mlstm_n, state_mlstm_m, state_ssd_conv, state_ssd_h, state_ret_S, state_ffn_conv, norm_mix_g, norm_ffn_g, norm_final_g, even_w_in, mlstm_igate_b, mlstm_fgate_b, mlstm_norm_g, ssd_conv_w, ssd_conv_b, ssd_dt_bias, ssd_A_log, ssd_D, ssd_norm_g, even_w_out, odd_w_in, ret_norm_g, odd_w_out, ffn_w_up, ffn_conv_w, ffn_conv_b, ffn_w_down):
    bsz, seq, d = x_prompt.shape
    dbsz, dseq, _ = x_sample.shape
    n_pad = SEQ_PAD - dseq
    assert norm_mix_g.shape[0] == 2 and B_CONV - 1 <= n_pad < SEQ_PAD and seq % (4 * CHUNK) == 0

    row = lambda a: a.reshape(1, -1)
    gate_bias = jnp.concatenate([_pad_lanes(row(mlstm_igate_b[0])), _pad_lanes(row(mlstm_fgate_b[0])),
                                 _pad_lanes(row(ssd_dt_bias[0]))], axis=0)
    even_consts = (gate_bias, row(mlstm_norm_g[0]), ssd_conv_w[0], row(ssd_conv_b[0]), _pad_lanes(row(ssd_A_log[0])),
                   row(jnp.repeat(ssd_D[0], B_HEADDIM)), row(ssd_norm_g[0]))
    g_mix0, g_mix1 = row(norm_mix_g[0]), row(norm_mix_g[1])
    g_ffn0, g_ffn1, g_final = row(norm_ffn_g[0]), row(norm_ffn_g[1]), row(norm_final_g)
    w_even_in, w_even_out = _even_w_in_cols(even_w_in[0]), even_w_out[0].astype(bf16)
    w_odd_in, w_odd_out = odd_w_in[0].astype(bf16), odd_w_out[0].astype(bf16)
    w_up, w_dn = ffn_w_up.astype(bf16), ffn_w_down.astype(bf16)
    ffn_cb = ffn_conv_b[:, None, :]
    ret_norm = row(ret_norm_g[0])
    cw = D_FF

    cos_p, sin_p = _rope_tables(np.arange(seq))
    xp = x_prompt.reshape(bsz * seq, d)
    xp, pC, pn, pm, pST, p_conv = _even_prompt(xp, g_mix0, w_even_in, even_consts, w_even_out,
                                               seq_rows=seq, L=CHUNK, n_sub=2)
    xp, p_f0 = _ffn(xp, None, g_ffn0, w_up[0], ffn_conv_w[0], ffn_cb[0], w_dn[0], None,
                    seq_rows=seq, tt=512, cw=cw, n_pad=0)
    xp, pSr = _odd_prompt(xp, g_mix1, w_odd_in, cos_p, sin_p, _retention_tables(CHUNK, 0) + (ret_norm,), w_odd_out,
                          seq_rows=seq, L=CHUNK, n_sub=2)
    _, p_f1, yp = _ffn(xp, None, g_ffn1, w_up[1], ffn_conv_w[1], ffn_cb[1], w_dn[1], g_final,
                       seq_rows=seq, tt=512, cw=cw, n_pad=0)
    pS = pST.reshape(bsz, B_STATE, B_HEADS, B_HEADDIM).transpose(0, 2, 3, 1)

    xs = jnp.pad(x_sample, ((0, 0), (n_pad, 0), (0, 0))).reshape(dbsz * SEQ_PAD, d)
    n_seg = 8
    cos_s, sin_s = _rope_tables(np.tile(PAST_LEN + np.arange(SEQ_PAD) - n_pad, n_seg))
    hist = lambda a: jnp.pad(a, ((0, 0), (n_pad - a.shape[1], dseq), (0, 0))).reshape(dbsz * SEQ_PAD, a.shape[-1])
    state = (hist(state_ssd_conv[0]), state_mlstm_C[0],
             jnp.pad(state_mlstm_n[0], ((0, 0), (0, SUBLANES - A_HEADS), (0, 0))),
             _pad_lanes(state_mlstm_m[0])[:, None, :],
             state_ssd_h[0].reshape(dbsz, B_HEADS // 2, 2 * B_HEADDIM, B_STATE))
    tm = 256
    proj = _norm_proj(xs, g_mix0, w_even_in, EVEN_SPLITS, tm)
    y, sC, sn, sm, sS, s_conv = _even_sample(proj, state, even_consts, n_seg=n_seg, n_pad=n_pad)
    sS = sS.reshape(dbsz, B_HEADS, B_HEADDIM, B_STATE)
    xs = _proj_res(xs, y, w_even_out, tm)
    xs, s_f0 = _ffn(xs, hist(state_ffn_conv[0]), g_ffn0, w_up[0], ffn_conv_w[0], ffn_cb[0], w_dn[0], None,
                    seq_rows=SEQ_PAD, tt=128, cw=cw, n_pad=n_pad)
    proj = _norm_proj(xs, g_mix1, w_odd_in, ODD_SPLITS, tm)
    y, sSr = _odd_sample(proj, state_ret_S[0], cos_s, sin_s,
                         _retention_tables(SEQ_PAD, n_pad, n_seg) + (ret_norm,), n_seg=n_seg)
    xs = _proj_res(xs, y, w_odd_out, tm)
    _, s_f1, ys = _ffn(xs, hist(state_ffn_conv[1]), g_ffn1, w_up[1], ffn_conv_w[1], ffn_cb[1], w_dn[1], g_final,
                       seq_rows=SEQ_PAD, tt=128, cw=cw, n_pad=n_pad)

    def pack(bs, C, n, m, S, conv_tail, Sr, f0, f1):
        conv = conv_tail.reshape(bs, SUBLANES, B_CONV_DIM)[:, SUBLANES - (B_CONV - 1):, :]
        ff = jnp.stack([f0.reshape(bs, SUBLANES, 2 * D_FF), f1.reshape(bs, SUBLANES, 2 * D_FF)])
        ff = ff[:, :, SUBLANES - (FFN_CONV - 1):, :]
        return (C[None], n[None, :, :A_HEADS, :], m[None, :, 0, :A_HEADS], conv[None], S[None], Sr[None], ff)

    y_prompt = yp.reshape(bsz, seq, d)
    y_sample = ys.reshape(dbsz, SEQ_PAD, d)[:, n_pad:, :]
    return ((y_prompt, y_sample) + pack(bsz, pC, pn, pm, pS, p_conv, pSr, p_f0, p_f1)
            + pack(dbsz, sC, sn, sm, sS, s_conv, sSr, s_f0, s_f1))
```

```python
import functools

import numpy as np
import jax
import jax.numpy as jnp
from jax import lax
from jax.experimental import pallas as pl
from jax.experimental.pallas import tpu as pltpu

f32 = jnp.float32
bf16 = jnp.bfloat16

EPS = 1e-6
CHUNK = 128
D_MODEL = 1024
A_HEADS, A_DK, A_DV = 4, 128, 256
A_WIDTH = A_HEADS * A_DV
B_HEADS, B_HEADDIM, B_GROUPS, B_STATE, B_CONV = 16, 64, 2, 128, 4
B_WIDTH = B_HEADS * B_HEADDIM
B_CONV_DIM = B_WIDTH + 2 * B_GROUPS * B_STATE
B_HEADS_PER_GROUP = B_HEADS // B_GROUPS
B_GROUP_WIDTH = B_WIDTH // B_GROUPS
C_HEADS, C_DK, C_DV = 8, 128, 256
C_WIDTH = C_HEADS * C_DV
ROPE_BASE = 10000.0
D_FF = 2816
FFN_CONV = 3
PAST_LEN = 16384

LANES = 128
SUBLANES = 8
SEQ_PAD = SUBLANES
NEG_BIG = -1e30
VMEM_LIMIT = 56 * 1024 * 1024

EVEN_SPLITS = (A_HEADS * A_DK, A_HEADS * A_DK, A_WIDTH, A_WIDTH, B_WIDTH, B_CONV_DIM, 3 * LANES)
ODD_SPLITS = (C_HEADS * C_DK, C_HEADS * C_DK, C_WIDTH, C_WIDTH)


def _params(sem):
    return pltpu.CompilerParams(dimension_semantics=sem, vmem_limit_bytes=VMEM_LIMIT)


def _resident(shape, index_map):
    return pl.BlockSpec(shape, index_map, pipeline_mode=pl.Buffered(1))


def _nt(a, b):
    return lax.dot_general(a, b, (((1,), (1,)), ((), ())), preferred_element_type=f32)


def _tn(a, b):
    return lax.dot_general(a, b, (((0,), (0,)), ((), ())), preferred_element_type=f32)


def _nn(a, b):
    return jnp.dot(a, b, preferred_element_type=f32)


def _rmsnorm(x, g):
    return x * lax.rsqrt(jnp.mean(x * x, axis=-1, keepdims=True) + EPS) * g


def _layernorm_nogain(h):
    mu = jnp.mean(h, axis=-1, keepdims=True)
    hc = h - mu
    return hc * lax.rsqrt(jnp.mean(hc * hc, axis=-1, keepdims=True) + EPS)


def _softplus(x):
    return jnp.maximum(x, 0.0) + jnp.log1p(jnp.exp(-jnp.abs(x)))


def _silu(x):
    return x * jax.nn.sigmoid(x)


def _split3(x):
    hi = x.astype(bf16)
    r1 = x - hi.astype(f32)
    mid = r1.astype(bf16)
    lo = (r1 - mid.astype(f32)).astype(bf16)
    return hi, mid, lo


def _cumsum_rows(tril, x):
    hi, mid, lo = _split3(x)
    return _nn(tril, hi) + _nn(tril, mid) + _nn(tril, lo)


def _split_views(ref, splits):
    views, off = [], 0
    for n in splits:
        views.append(ref.at[:, off:off + n])
        off += n
    return views


def _project(h, w_ref, p_ref, splits):
    off = 0
    for n in splits:
        p_ref[:, off:off + n] = _nn(h, w_ref[:, off:off + n])
        off += n


MXU_COLS = 256


def _proj_pieces(h_ref, w_ref, dst_ref):
    def piece(c0, n):
        def run():
            dst_ref[:, c0:c0 + n] = _nn(h_ref[...], w_ref[:, c0:c0 + n])
        return run
    total = w_ref.shape[1]
    return [piece(c0, min(MXU_COLS, total - c0)) for c0 in range(0, total, MXU_COLS)]


def _out_pieces(x_ref, y_ref, w_ref, o_ref, rows):
    def piece(c0, n):
        def run():
            o_ref[rows, c0:c0 + n] = x_ref[rows, c0:c0 + n] + _nn(y_ref[...], w_ref[:, c0:c0 + n])
        return run
    total = w_ref.shape[1]
    return [piece(c0, min(MXU_COLS, total - c0)) for c0 in range(0, total, MXU_COLS)]


def _spread(work, n_slots):
    return [work[i * len(work) // n_slots:(i + 1) * len(work) // n_slots] for i in range(n_slots)]


def _norm_proj_body(x_ref, g_ref, w_ref, *o_refs, splits):
    h = _rmsnorm(x_ref[...], g_ref[...]).astype(bf16)
    off = 0
    for o_ref, n in zip(o_refs, splits):
        o_ref[...] = _nn(h, w_ref[:, off:off + n])
        off += n


def _norm_proj(x, g, w, splits, tm):
    rows, d = x.shape
    return pl.pallas_call(
        functools.partial(_norm_proj_body, splits=splits),
        grid=(rows // tm,),
        in_specs=[pl.BlockSpec((tm, d), lambda i: (i, 0)),
                  pl.BlockSpec((1, d), lambda i: (0, 0)),
                  _resident(w.shape, lambda i: (0, 0))],
        out_specs=[pl.BlockSpec((tm, n), lambda i: (i, 0)) for n in splits],
        out_shape=[jax.ShapeDtypeStruct((rows, n), f32) for n in splits],
        compiler_params=_params(("arbitrary",)),
        name="norm_proj",
    )(x, g, w)


def _proj_res_body(x_ref, y_ref, w_ref, o_ref):
    o_ref[...] = x_ref[...] + _nn(y_ref[...].astype(bf16), w_ref[...])


def _proj_res(x, y, w, tm):
    rows, d = x.shape
    k = y.shape[1]
    return pl.pallas_call(
        _proj_res_body,
        grid=(rows // tm,),
        in_specs=[pl.BlockSpec((tm, d), lambda i: (i, 0)),
                  pl.BlockSpec((tm, k), lambda i: (i, 0)),
                  _resident(w.shape, lambda i: (0, 0))],
        out_specs=pl.BlockSpec((tm, d), lambda i: (i, 0)),
        out_shape=jax.ShapeDtypeStruct((rows, d), f32),
        compiler_params=_params(("arbitrary",)),
        name="proj_res",
    )(x, y, w)


def _ffn_body(*refs, tt, cw, sample, final):
    it = iter(refs)
    x_ref = next(it)
    hist_ref = next(it) if sample else None
    g_ref, wup_ref, cw_ref, cb_ref, wdn_ref = next(it), next(it), next(it), next(it), next(it)
    gf_ref = next(it) if final else None
    o_ref, tail_ref = next(it), next(it)
    yn_ref = next(it) if final else None
    if sample:
        nb = hist_ref.shape[0]
    else:
        scr, carry = next(it), next(it)

        @pl.when(pl.program_id(1) == 0)
        def _():
            carry[...] = jnp.zeros(carry.shape, f32)

    x = x_ref[...]
    h = _rmsnorm(x, g_ref[...]).astype(bf16)
    acc = jnp.zeros((tt, D_MODEL), f32)
    for j in range(D_FF // cw):
        conv = []
        for part in range(2):
            c0 = part * D_FF + j * cw
            u = _nn(h, wup_ref[:, c0:c0 + cw])
            if sample:
                ext = jnp.concatenate([hist_ref[:, k * 2 * D_FF + c0:k * 2 * D_FF + c0 + cw]
                                       for k in range(FFN_CONV - 1)] + [u], axis=0)
                prev2, prev1 = ext[0:tt, :], ext[nb:nb + tt, :]
                for k in range(FFN_CONV - 1):
                    tail_ref[:, k * 2 * D_FF + c0:k * 2 * D_FF + c0 + cw] = ext[tt + k * nb:tt + (k + 1) * nb, :]
            else:
                scr[0:SUBLANES, :] = carry[:, c0:c0 + cw]
                tail_ref[:, c0:c0 + cw] = u[tt - SUBLANES:tt, :]
                carry[:, c0:c0 + cw] = u[tt - SUBLANES:tt, :]
                scr[SUBLANES:SUBLANES + tt, :] = u
                prev2, prev1 = scr[SUBLANES - 2:SUBLANES - 2 + tt, :], scr[SUBLANES - 1:SUBLANES - 1 + tt, :]
            y = cb_ref[:, c0:c0 + cw] + prev2 * cw_ref[0:1, c0:c0 + cw]
            y = y + prev1 * cw_ref[1:2, c0:c0 + cw]
            y = y + u * cw_ref[2:3, c0:c0 + cw]
            conv.append(y)
        act = (_silu(conv[0]) * conv[1]).astype(bf16)
        acc = acc + _nn(act, wdn_ref[j * cw:(j + 1) * cw, :])
    out = x + acc
    o_ref[...] = out
    if final:
        yn_ref[...] = _rmsnorm(out, gf_ref[...])


def _ffn(x, hist, g, wup, conv_w, conv_b, wdn, g_final, *, seq_rows, tt, cw):
    rows, d = x.shape
    sample = hist is not None
    final = g_final is not None
    if sample:
        tt = rows
        grid = (1, 1)
        rmap = lambda s, t: (0, 0)
        tail_spec = pl.BlockSpec(hist.shape, rmap)
        tail_shape = jax.ShapeDtypeStruct(hist.shape, f32)
        scratch = []
    else:
        n_seq = rows // seq_rows
        n_inner = seq_rows // tt
        grid = (n_seq, n_inner)
        rmap = lambda s, t: (s * n_inner + t, 0)
        tail_spec = pl.BlockSpec((SUBLANES, 2 * D_FF), lambda s, t: (s, 0))
        tail_shape = jax.ShapeDtypeStruct((n_seq * SUBLANES, 2 * D_FF), f32)
        scratch = [pltpu.VMEM((SUBLANES + tt, cw), f32), pltpu.VMEM((SUBLANES, 2 * D_FF), f32)]
    cmap = lambda s, t: (0, 0)
    in_specs = [pl.BlockSpec((tt, d), rmap)]
    args = [x]
    if sample:
        in_specs.append(_resident(hist.shape, cmap))
        args.append(hist)
    in_specs += [pl.BlockSpec((1, d), cmap), _resident(wup.shape, cmap), pl.BlockSpec(conv_w.shape, cmap),
                 pl.BlockSpec(conv_b.shape, cmap), _resident(wdn.shape, cmap)]
    args += [g, wup, conv_w, conv_b, wdn]
    out_specs = [pl.BlockSpec((tt, d), rmap), tail_spec]
    out_shape = [jax.ShapeDtypeStruct((rows, d), f32), tail_shape]
    if final:
        in_specs.append(pl.BlockSpec((1, d), cmap))
        args.append(g_final)
        out_specs.append(pl.BlockSpec((tt, d), rmap))
        out_shape.append(jax.ShapeDtypeStruct((rows, d), f32))
    return pl.pallas_call(
        functools.partial(_ffn_body, tt=tt, cw=cw, sample=sample, final=final),
        grid=grid, in_specs=in_specs, out_specs=out_specs, out_shape=out_shape,
        scratch_shapes=scratch, compiler_params=_params(("arbitrary", "arbitrary")),
        name="ffn_sample" if sample else "ffn_prompt",
    )(*args)


EVEN_SLOTS = A_HEADS + B_HEADS // 2


def _even_chunk(r0, slot, L, n_pad, n_seg, proj_refs, hist_ref, const_refs, y_ref, state_in_refs, state_refs,
                tail_ref, conv_scr, yb_scr, side=None):
    q_ref, k_ref, v_ref, og_ref, z_ref, xbc_ref, gt_ref = proj_refs
    gb_ref, anorm_ref, convw_ref, convb_ref, alog_ref, dx_ref, bnorm_ref = const_refs
    C_in, n_in, m_in, S_in = state_in_refs
    C_ref, n_ref, m_ref, S_ref = state_refs
    rows = pl.ds(r0, L)
    seg = L // n_seg
    seg_rows = [slice(i * seg, (i + 1) * seg) for i in range(n_seg)]
    seg_last = [slice((i + 1) * seg - 1, (i + 1) * seg) for i in range(n_seg)]

    def per_row(vals):
        if n_seg == 1:
            return vals[0]
        return jnp.concatenate([jnp.broadcast_to(v, (seg, v.shape[1])) for v in vals], axis=0)

    ri = lax.broadcasted_iota(jnp.int32, (L, L), 0)
    ci = lax.broadcasted_iota(jnp.int32, (L, L), 1)
    causal = ri >= ci
    if n_seg > 1:
        causal = causal & ((ri // seg) == (ci // seg))
    tril = causal.astype(bf16)
    valid = None
    if n_pad:
        valid = (lax.broadcasted_iota(jnp.int32, (L, 1), 0) % seg) >= n_pad

    gates = gt_ref[rows, :]
    li = gates[:, 0:LANES] + gb_ref[0:1, :]
    fpre = gates[:, LANES:2 * LANES] + gb_ref[1:2, :]
    lf = -_softplus(-fpre)
    dt = _softplus(gates[:, 2 * LANES:3 * LANES] + gb_ref[2:3, :])
    if n_pad:
        li = jnp.where(valid, li, NEG_BIG)
        lf = jnp.where(valid, lf, 0.0)
        dt = jnp.where(valid, dt, 0.0)
    a = dt * (-jnp.exp(alog_ref[...]))
    cums = _cumsum_rows(tril, jnp.concatenate([lf, a], axis=1))
    bcum = cums[:, 0:LANES]
    acum = cums[:, LANES:2 * LANES]
    liT, bT, aT, dtT = li.T, bcum.T, acum.T, dt.T

    m_old = [m_in[slot + i] for i in range(n_seg)]
    b_lasts = [bcum[r, :] for r in seg_last]
    m_rows, b_last = per_row(m_old), per_row(b_lasts)
    log_g = b_last - bcum + li
    m_news = [jnp.maximum(b_lasts[i] + m_old[i], jnp.max(log_g[seg_rows[i], :], axis=0, keepdims=True))
              for i in range(n_seg)]
    m_new = per_row(m_news)
    gfac = jnp.exp(log_g - m_new)
    cdecay = jnp.exp(b_last + m_rows - m_new)
    for i in range(n_seg):
        m_ref[slot + i] = m_news[i]
    for h in range(A_HEADS):
        for run in (side[h] if side else ()):
            run()
        bcol, brow, lirow = bcum[:, h:h + 1], bT[h:h + 1, :], liT[h:h + 1, :]
        m_h = m_rows[:, h:h + 1]
        logw = jnp.where(causal, bcol - brow + lirow, -jnp.inf)
        log_prev = bcol + m_h
        m_t = jnp.maximum(log_prev, jnp.max(logw, axis=-1, keepdims=True))
        w_in = jnp.exp(logw - m_t)
        w_prev = jnp.exp(log_prev - m_t)
        qh = q_ref[rows, h * A_DK:(h + 1) * A_DK]
        kh = k_ref[rows, h * A_DK:(h + 1) * A_DK] * (A_DK ** -0.5)
        vh = v_ref[rows, h * A_DV:(h + 1) * A_DV]
        vb = vh.astype(bf16)
        qb = qh.astype(bf16)
        s = _nt(qb, kh.astype(bf16)) * w_in
        kg = kh * gfac[:, h:h + 1]
        Cs = [C_in[slot + i, h] for i in range(n_seg)]
        ns = [n_in[slot + i, h:h + 1, :] for i in range(n_seg)]
        if n_seg == 1:
            inter = _nn(qb, Cs[0].astype(bf16))
            dec_h = cdecay[:, h:h + 1]
            C_ref[slot, h] = Cs[0] * dec_h + _tn(kg.astype(bf16), vb)
            n_ref[slot, h:h + 1, :] = ns[0] * dec_h + jnp.sum(kg, axis=0, keepdims=True)
        else:
            inter = jnp.concatenate([_nn(qh[seg_rows[i], :].astype(bf16), Cs[i].astype(bf16))
                                     for i in range(n_seg)], axis=0)
            for i in range(n_seg):
                rs = seg_rows[i]
                dec_h = cdecay[seg_last[i], h:h + 1]
                C_ref[slot + i, h] = Cs[i] * dec_h + _tn(kg[rs, :].astype(bf16), vh[rs, :].astype(bf16))
                n_ref[slot + i, h:h + 1, :] = ns[i] * dec_h + jnp.sum(kg[rs, :], axis=0, keepdims=True)
        num = _nn(s.astype(bf16), vb) + inter * w_prev
        den = jnp.sum(s, axis=-1, keepdims=True) + jnp.sum(qh * per_row(ns), axis=-1, keepdims=True) * w_prev
        hout = num / jnp.maximum(jnp.abs(den), jnp.exp(-m_t))
        ya = _layernorm_nogain(hout) * anorm_ref[:, h * A_DV:(h + 1) * A_DV]
        ya = ya * jax.nn.sigmoid(og_ref[rows, h * A_DV:(h + 1) * A_DV])
        y_ref[rows, h * A_DV:(h + 1) * A_DV] = ya.astype(y_ref.dtype)

    xraw = xbc_ref[rows, :]
    if hist_ref is not None:
        seg_row = lax.broadcasted_iota(jnp.int32, (L, 1), 0) % seg
        is_hist = (seg_row >= n_pad - (B_CONV - 1)) & jnp.logical_not(valid)
        xraw = jnp.where(is_hist, hist_ref[rows, :], xraw)
    conv_scr[SUBLANES:SUBLANES + L, :] = xraw
    xc = convb_ref[...] + conv_scr[SUBLANES - 3:SUBLANES - 3 + L, :] * convw_ref[0:1, :]
    xc = xc + conv_scr[SUBLANES - 2:SUBLANES - 2 + L, :] * convw_ref[1:2, :]
    xc = xc + conv_scr[SUBLANES - 1:SUBLANES - 1 + L, :] * convw_ref[2:3, :]
    xc = xc + xraw * convw_ref[3:4, :]
    if n_seg == 1:
        new_tail = conv_scr[L:L + SUBLANES, :]
        tail_ref[...] = new_tail
        conv_scr[0:SUBLANES, :] = new_tail
    else:
        tail_ref[...] = xraw
    xc = _silu(xc)

    a_last = per_row([acum[r, :] for r in seg_last])
    wtile = jnp.exp(a_last - acum) * dt
    expa = jnp.exp(acum)

    def head_matrix(cb, h):
        acol, arow, dtrow = acum[:, h:h + 1], aT[h:h + 1, :], dtT[h:h + 1, :]
        decay = jnp.where(causal, jnp.exp(jnp.where(causal, acol - arow, 0.0)), 0.0)
        return (cb * decay * dtrow).astype(bf16)

    def group_bc(g):
        Bg = xc[:, B_WIDTH + g * B_STATE:B_WIDTH + (g + 1) * B_STATE].astype(bf16)
        c0 = B_WIDTH + B_GROUPS * B_STATE + g * B_STATE
        Cg = xc[:, c0:c0 + B_STATE].astype(bf16)
        return Bg, Cg, _nt(Cg, Bg)

    low_half = lax.broadcasted_iota(jnp.int32, (L, LANES), 1) < B_HEADDIM
    first_head_rows = lax.broadcasted_iota(jnp.int32, (2 * B_HEADDIM, 1), 0) < B_HEADDIM

    def pair_lanes(t, j):
        return jnp.where(low_half, jnp.broadcast_to(t[:, 2 * j:2 * j + 1], (L, LANES)),
                         jnp.broadcast_to(t[:, 2 * j + 1:2 * j + 2], (L, LANES)))

    pairs_per_group = B_HEADS_PER_GROUP // 2
    for g in range(B_GROUPS):
        Bg, Cg, cb = group_bc(g)
        gs = slice(g * B_GROUP_WIDTH, (g + 1) * B_GROUP_WIDTH)
        if n_seg == 1:
            ST = S_in[slot, :, gs]
            inter_g = _nn(Cg, ST.astype(bf16))
        else:
            Bf = xc[:, B_WIDTH + g * B_STATE:B_WIDTH + (g + 1) * B_STATE]
            c0 = B_WIDTH + B_GROUPS * B_STATE + g * B_STATE
            Cf = xc[:, c0:c0 + B_STATE]
        xw, decs = [], []
        for jp in range(pairs_per_group):
            j = g * pairs_per_group + jp
            for run in (side[A_HEADS + j] if side else ()):
                run()
            ps = slice(j * LANES, (j + 1) * LANES)
            e_pair, w_pair = pair_lanes(expa, j), pair_lanes(wtile, j)
            mcat = jnp.concatenate([head_matrix(cb, 2 * j), head_matrix(cb, 2 * j + 1)], axis=1)
            xp = xc[:, ps]
            xbd = jnp.concatenate([jnp.where(low_half, xp, 0.0).astype(bf16),
                                   jnp.where(low_half, 0.0, xp).astype(bf16)], axis=0)
            if n_seg == 1:
                inter = inter_g[:, jp * LANES:(jp + 1) * LANES]
                xw.append((xp * w_pair).astype(bf16))
                decs.append(e_pair[L - 1:L, :])
            else:
                xwf = xp * w_pair
                parts = []
                for i in range(n_seg):
                    rs = seg_rows[i]
                    Sp = S_in[slot + i, j]
                    parts.append(_nt(Cf[rs, :].astype(bf16), Sp.astype(bf16)))
                    e_last = e_pair[seg_last[i], :]
                    dec_col = jnp.where(first_head_rows, e_last[:, 0:1], e_last[:, B_HEADDIM:B_HEADDIM + 1])
                    S_ref[slot + i, j] = Sp * dec_col + _tn(xwf[rs, :].astype(bf16), Bf[rs, :].astype(bf16))
                inter = jnp.concatenate(parts, axis=0)
            yb_scr[:, ps] = (_nn(mcat, xbd) + inter * e_pair) + dx_ref[:, ps] * xp
        if n_seg == 1:
            S_ref[slot, :, gs] = ST * jnp.concatenate(decs, axis=1) + _tn(Bg, jnp.concatenate(xw, axis=1))
    yb = yb_scr[...] * _silu(z_ref[rows, :])
    for g in range(B_GROUPS):
        gs = slice(g * B_GROUP_WIDTH, (g + 1) * B_GROUP_WIDTH)
        yg = yb[:, gs]
        yg = yg * lax.rsqrt(jnp.mean(yg * yg, axis=-1, keepdims=True) + EPS)
        y_ref[rows, A_WIDTH + g * B_GROUP_WIDTH:A_WIDTH + (g + 1) * B_GROUP_WIDTH] = (
            yg * bnorm_ref[:, gs]).astype(y_ref.dtype)


N_EVEN_CONSTS = 7


def _even_prompt_body(*refs, L, n_sub):
    it = iter(refs)
    x_ref, xnext_ref, gmix_ref, win_ref = next(it), next(it), next(it), next(it)
    const_refs = tuple(next(it) for _ in range(N_EVEN_CONSTS))
    wout_ref = next(it)
    o_ref = next(it)
    state_refs = tuple(next(it) for _ in range(4))
    tail_ref = next(it)
    p_scr, y_scr = (next(it), next(it)), (next(it), next(it))
    h_scr, conv_scr, yb_scr = (next(it) for _ in range(3))
    th = L * n_sub
    gmix = gmix_ref[...]

    @pl.when((pl.program_id(0) == 0) & (pl.program_id(1) == 0))
    def _():
        _project(_rmsnorm(x_ref[0:th, :], gmix).astype(bf16), win_ref, p_scr[0], EVEN_SPLITS)

    @pl.when(pl.program_id(1) == 0)
    def _():
        conv_scr[0:SUBLANES, :] = jnp.zeros((SUBLANES, B_CONV_DIM), f32)
        for r in state_refs:
            r[...] = jnp.zeros(r.shape, f32)

    for half in range(2):
        ahead = xnext_ref[...] if half else x_ref[th:2 * th, :]
        h_scr[...] = _rmsnorm(ahead, gmix).astype(bf16)
        work = _proj_pieces(h_scr, win_ref, p_scr[1 - half])
        if half:
            work = work + _out_pieces(x_ref, y_scr[0], wout_ref, o_ref, slice(0, th))
        side = _spread(work, n_sub * EVEN_SLOTS)
        proj_refs = _split_views(p_scr[half], EVEN_SPLITS)
        for c in range(n_sub):
            _even_chunk(c * L, 0, L, 0, 1, proj_refs, None, const_refs, y_scr[half], state_refs, state_refs,
                        tail_ref, conv_scr, yb_scr, side[c * EVEN_SLOTS:(c + 1) * EVEN_SLOTS])
    for run in _out_pieces(x_ref, y_scr[1], wout_ref, o_ref, slice(th, 2 * th)):
        run()


def _even_prompt(x, gmix, win, consts, wout, *, seq_rows, L, n_sub):
    rows, d = x.shape
    n_seq = rows // seq_rows
    th = L * n_sub
    tt = 2 * th
    n_inner = seq_rows // tt
    n_half = rows // th
    rmap = lambda s, t: (s * n_inner + t, 0)
    nextmap = lambda s, t: (jnp.minimum(2 * (s * n_inner + t) + 2, n_half - 1), 0)
    cmap = lambda s, t: (0, 0)
    smap4 = lambda s, t: (s, 0, 0, 0)
    smap3 = lambda s, t: (s, 0, 0)
    st_specs = [pl.BlockSpec((1, A_HEADS, A_DK, A_DV), smap4), pl.BlockSpec((1, SUBLANES, A_DK), smap3),
                pl.BlockSpec((1, 1, LANES), smap3), pl.BlockSpec((1, B_STATE, B_WIDTH), smap3)]
    st_shapes = [jax.ShapeDtypeStruct((n_seq, A_HEADS, A_DK, A_DV), f32),
                 jax.ShapeDtypeStruct((n_seq, SUBLANES, A_DK), f32),
                 jax.ShapeDtypeStruct((n_seq, 1, LANES), f32),
                 jax.ShapeDtypeStruct((n_seq, B_STATE, B_WIDTH), f32)]
    in_specs = ([pl.BlockSpec((tt, d), rmap), pl.BlockSpec((th, d), nextmap), pl.BlockSpec((1, d), cmap),
                 _resident(win.shape, cmap)]
                + [pl.BlockSpec(c.shape, cmap) for c in consts] + [_resident(wout.shape, cmap)])
    p_shape = pltpu.VMEM((th, sum(EVEN_SPLITS)), f32)
    y_shape = pltpu.VMEM((th, A_WIDTH + B_WIDTH), bf16)
    return pl.pallas_call(
        functools.partial(_even_prompt_body, L=L, n_sub=n_sub),
        grid=(n_seq, n_inner), in_specs=in_specs,
        out_specs=[pl.BlockSpec((tt, d), rmap)] + st_specs + [pl.BlockSpec((SUBLANES, B_CONV_DIM), lambda s, t: (s, 0))],
        out_shape=[jax.ShapeDtypeStruct((rows, d), f32)] + st_shapes
        + [jax.ShapeDtypeStruct((n_seq * SUBLANES, B_CONV_DIM), f32)],
        scratch_shapes=[p_shape, p_shape, y_shape, y_shape, pltpu.VMEM((th, d), bf16),
                        pltpu.VMEM((SUBLANES + L, B_CONV_DIM), f32), pltpu.VMEM((L, B_WIDTH), f32)],
        compiler_params=_params(("arbitrary", "arbitrary")),
        name="even_prompt",
    )(x, x, gmix, win, *consts, wout)


def _even_sample_body(*refs, L, n_seg, n_pad):
    it = iter(refs)
    proj_refs = tuple(next(it) for _ in range(7))
    hist_ref = next(it)
    state_in_refs = tuple(next(it) for _ in range(4))
    const_refs = tuple(next(it) for _ in range(N_EVEN_CONSTS))
    y_ref = next(it)
    state_refs = tuple(next(it) for _ in range(4))
    tail_ref = next(it)
    conv_scr, yb_scr = next(it), next(it)

    conv_scr[0:SUBLANES, :] = jnp.zeros((SUBLANES, B_CONV_DIM), f32)
    n_ref = state_refs[1]
    n_ref[:, A_HEADS:, :] = jnp.zeros((n_seg, SUBLANES - A_HEADS, A_DK), f32)
    _even_chunk(0, 0, L, n_pad, n_seg, proj_refs, hist_ref, const_refs, y_ref, state_in_refs, state_refs,
                tail_ref, conv_scr, yb_scr)


def _even_sample(proj, state, consts, *, n_seg, n_pad):
    rows = proj[0].shape[0]
    L = SEQ_PAD * n_seg
    rmap = lambda o: (o, 0)
    cmap = lambda o: (0, 0)
    smap4 = lambda o: (o, 0, 0, 0)
    smap3 = lambda o: (o, 0, 0)
    st_specs = [pl.BlockSpec((n_seg, A_HEADS, A_DK, A_DV), smap4), pl.BlockSpec((n_seg, SUBLANES, A_DK), smap3),
                pl.BlockSpec((n_seg, 1, LANES), smap3),
                pl.BlockSpec((n_seg, B_HEADS // 2, 2 * B_HEADDIM, B_STATE), smap4)]
    st_shapes = [jax.ShapeDtypeStruct(s.shape, f32) for s in state[1:]]
    in_specs = ([pl.BlockSpec((L, a.shape[1]), rmap) for a in proj] + [pl.BlockSpec((L, B_CONV_DIM), rmap)]
                + st_specs + [pl.BlockSpec(c.shape, cmap) for c in consts])
    return pl.pallas_call(
        functools.partial(_even_sample_body, L=L, n_seg=n_seg, n_pad=n_pad),
        grid=(rows // L,), in_specs=in_specs,
        out_specs=[pl.BlockSpec((L, A_WIDTH + B_WIDTH), rmap)] + st_specs + [pl.BlockSpec((L, B_CONV_DIM), rmap)],
        out_shape=[jax.ShapeDtypeStruct((rows, A_WIDTH + B_WIDTH), f32)] + st_shapes
        + [jax.ShapeDtypeStruct((rows, B_CONV_DIM), f32)],
        scratch_shapes=[pltpu.VMEM((SUBLANES + L, B_CONV_DIM), f32), pltpu.VMEM((L, B_WIDTH), f32)],
        compiler_params=_params(("arbitrary",)),
        name="even_sample",
    )(*proj, *state, *consts)


def _odd_chunk(r0, rope_r0, slot, L, n_seg, proj_refs, const_refs, y_ref, S_in, S_ref, side=None):
    q_ref, k_ref, v_ref, g_ref = proj_refs
    cos_ref, sin_ref, intra_ref, cross_ref, into_ref, cdec_ref, norm_ref = const_refs
    rows = pl.ds(r0, L)
    seg = L // n_seg
    cosf = cos_ref[pl.ds(rope_r0, L), :]
    sinf = sin_ref[pl.ds(rope_r0, L), :]
    cross = cross_ref[...]
    into = into_ref[...]
    cdec = cdec_ref[...]
    for h in range(C_HEADS):
        for run in (side[h] if side else ()):
            run()
        qh = q_ref[rows, h * C_DK:(h + 1) * C_DK]
        kh = k_ref[rows, h * C_DK:(h + 1) * C_DK]
        qh = qh * cosf + pltpu.roll(qh, C_DK // 2, 1) * sinf
        kh = (kh * cosf + pltpu.roll(kh, C_DK // 2, 1) * sinf) * (C_DK ** -0.5)
        vh = v_ref[rows, h * C_DV:(h + 1) * C_DV]
        vb = vh.astype(bf16)
        qb = qh.astype(bf16)
        s = _nt(qb, kh.astype(bf16)) * intra_ref[h]
        kd = kh * into[:, h:h + 1]
        if n_seg == 1:
            S = S_in[slot, h]
            inter = _nn(qb, S.astype(bf16))
            S_ref[slot, h] = S * cdec[:, h:h + 1] + _tn(kd.astype(bf16), vb)
        else:
            parts = []
            for i in range(n_seg):
                rs = slice(i * seg, (i + 1) * seg)
                S = S_in[slot + i, h]
                parts.append(_nn(qh[rs, :].astype(bf16), S.astype(bf16)))
                S_ref[slot + i, h] = S * cdec[:, h:h + 1] + _tn(kd[rs, :].astype(bf16), vh[rs, :].astype(bf16))
            inter = jnp.concatenate(parts, axis=0)
        o = _nn(s.astype(bf16), vb) + inter * cross[:, h:h + 1]
        y = _layernorm_nogain(o) * norm_ref[:, h * C_DV:(h + 1) * C_DV]
        y = y * _silu(g_ref[rows, h * C_DV:(h + 1) * C_DV])
        y_ref[rows, h * C_DV:(h + 1) * C_DV] = y.astype(y_ref.dtype)


N_ODD_CONSTS = 7


def _odd_prompt_body(*refs, L, n_sub):
    it = iter(refs)
    x_ref, xnext_ref, gmix_ref, win_ref = next(it), next(it), next(it), next(it)
    const_refs = tuple(next(it) for _ in range(N_ODD_CONSTS))
    wout_ref = next(it)
    o_ref, S_ref = next(it), next(it)
    p_scr, y_scr = (next(it), next(it)), (next(it), next(it))
    h_scr = next(it)
    th = L * n_sub
    gmix = gmix_ref[...]

    @pl.when((pl.program_id(0) == 0) & (pl.program_id(1) == 0))
    def _():
        _project(_rmsnorm(x_ref[0:th, :], gmix).astype(bf16), win_ref, p_scr[0], ODD_SPLITS)

    @pl.when(pl.program_id(1) == 0)
    def _():
        S_ref[...] = jnp.zeros(S_ref.shape, f32)

    for half in range(2):
        ahead = xnext_ref[...] if half else x_ref[th:2 * th, :]
        h_scr[...] = _rmsnorm(ahead, gmix).astype(bf16)
        work = _proj_pieces(h_scr, win_ref, p_scr[1 - half])
        if half:
            work = work + _out_pieces(x_ref, y_scr[0], wout_ref, o_ref, slice(0, th))
        side = _spread(work, n_sub * C_HEADS)
        proj_refs = _split_views(p_scr[half], ODD_SPLITS)
        for c in range(n_sub):
            _odd_chunk(c * L, half * th + c * L, 0, L, 1, proj_refs, const_refs, y_scr[half], S_ref, S_ref,
                       side[c * C_HEADS:(c + 1) * C_HEADS])
    for run in _out_pieces(x_ref, y_scr[1], wout_ref, o_ref, slice(th, 2 * th)):
        run()


def _odd_prompt(x, gmix, win, cos, sin, consts, wout, *, seq_rows, L, n_sub):
    rows, d = x.shape
    n_seq = rows // seq_rows
    th = L * n_sub
    tt = 2 * th
    n_inner = seq_rows // tt
    n_half = rows // th
    rmap = lambda s, t: (s * n_inner + t, 0)
    nextmap = lambda s, t: (jnp.minimum(2 * (s * n_inner + t) + 2, n_half - 1), 0)
    cmap = lambda s, t: (0, 0)
    st_spec = pl.BlockSpec((1, C_HEADS, C_DK, C_DV), lambda s, t: (s, 0, 0, 0))
    rope_spec = pl.BlockSpec((tt, LANES), lambda s, t: (t, 0))
    in_specs = [pl.BlockSpec((tt, d), rmap), pl.BlockSpec((th, d), nextmap), pl.BlockSpec((1, d), cmap),
                _resident(win.shape, cmap), rope_spec, rope_spec]
    for c in consts:
        in_specs.append(pl.BlockSpec(c.shape, (lambda s, t: (0, 0, 0)) if c.ndim == 3 else cmap))
    in_specs.append(_resident(wout.shape, cmap))
    p_shape = pltpu.VMEM((th, sum(ODD_SPLITS)), f32)
    y_shape = pltpu.VMEM((th, C_WIDTH), bf16)
    return pl.pallas_call(
        functools.partial(_odd_prompt_body, L=L, n_sub=n_sub),
        grid=(n_seq, n_inner), in_specs=in_specs,
        out_specs=[pl.BlockSpec((tt, d), rmap), st_spec],
        out_shape=[jax.ShapeDtypeStruct((rows, d), f32), jax.ShapeDtypeStruct((n_seq, C_HEADS, C_DK, C_DV), f32)],
        scratch_shapes=[p_shape, p_shape, y_shape, y_shape, pltpu.VMEM((th, d), bf16)],
        compiler_params=_params(("arbitrary", "arbitrary")),
        name="odd_prompt",
    )(x, x, gmix, win, cos, sin, *consts, wout)


def _odd_sample_body(*refs, L, n_seg):
    it = iter(refs)
    proj_refs = tuple(next(it) for _ in range(4))
    S_in = next(it)
    const_refs = tuple(next(it) for _ in range(N_ODD_CONSTS))
    y_ref, S_ref = next(it), next(it)
    _odd_chunk(0, 0, 0, L, n_seg, proj_refs, const_refs, y_ref, S_in, S_ref)


def _odd_sample(proj, S0, cos, sin, consts, *, n_seg):
    rows = proj[0].shape[0]
    L = SEQ_PAD * n_seg
    rmap = lambda o: (o, 0)
    cmap = lambda o: (0, 0)
    st_spec = pl.BlockSpec((n_seg, C_HEADS, C_DK, C_DV), lambda o: (o, 0, 0, 0))
    in_specs = [pl.BlockSpec((L, a.shape[1]), rmap) for a in proj] + [st_spec]
    for c in (cos, sin) + tuple(consts):
        in_specs.append(pl.BlockSpec(c.shape, (lambda o: (0, 0, 0)) if c.ndim == 3 else cmap))
    return pl.pallas_call(
        functools.partial(_odd_sample_body, L=L, n_seg=n_seg),
        grid=(rows // L,), in_specs=in_specs,
        out_specs=[pl.BlockSpec((L, C_WIDTH), rmap), st_spec],
        out_shape=[jax.ShapeDtypeStruct((rows, C_WIDTH), f32), jax.ShapeDtypeStruct(S0.shape, f32)],
        compiler_params=_params(("arbitrary",)),
        name="odd_sample",
    )(*proj, S0, cos, sin, *consts)


def _pad_lanes(a, width=LANES):
    return jnp.pad(a, [(0, 0)] * (a.ndim - 1) + [(0, width - a.shape[-1])])


def _even_w_in_cols(w):
    sizes = [A_HEADS * A_DK, A_HEADS * A_DK, A_WIDTH, A_HEADS, A_HEADS, A_WIDTH, B_WIDTH, B_CONV_DIM, B_HEADS]
    q, k, v, ig, fg, og, z, xbc, dt = jnp.split(w, np.cumsum(sizes)[:-1].tolist(), axis=-1)
    return jnp.concatenate([q, k, v, og, z, xbc, _pad_lanes(ig), _pad_lanes(fg), _pad_lanes(dt)], axis=-1).astype(bf16)


def _retention_tables(seg, n_pad, n_seg=1):
    log_gamma = np.log1p(-np.exp2(-5.0 - np.arange(C_HEADS, dtype=np.float64)))
    t_real = seg - n_pad
    idx = np.arange(seg, dtype=np.float64) - n_pad
    real = idx >= 0
    diff = idx[:, None] - idx[None, :]
    intra = np.where((diff >= 0) & real[None, :], np.exp(log_gamma[:, None, None] * np.maximum(diff, 0.0)), 0.0)
    intra = np.stack([np.kron(np.eye(n_seg), intra[h]) for h in range(C_HEADS)])
    cross = np.where(real[:, None], np.exp(log_gamma[None, :] * (idx[:, None] + 1.0)), 0.0)
    into = np.where(real[:, None], np.exp(log_gamma[None, :] * (t_real - 1.0 - idx[:, None])), 0.0)
    cdec = np.exp(log_gamma * t_real)[None, :]
    lanes = lambda a: np.pad(a, ((0, 0), (0, LANES - a.shape[1]))).astype(np.float32)
    return intra.astype(np.float32), lanes(np.tile(cross, (n_seg, 1))), lanes(np.tile(into, (n_seg, 1))), lanes(cdec)


def _rope_tables(pos):
    half = C_DK // 2
    inv = ROPE_BASE ** (-np.arange(half, dtype=np.float64) / half)
    ang = np.asarray(pos, dtype=np.float64)[:, None] * inv[None, :]
    cos, sin = np.cos(ang), np.sin(ang)
    return (np.concatenate([cos, cos], axis=-1).astype(np.float32),
            np.concatenate([-sin, sin], axis=-1).astype(np.float32))


def kernel(x_prompt, x_sample, state_mlstm_C, state_mlstm_n, state_mlstm_m, state_ssd_conv, state_ssd_h, state_ret_S, state_ffn_conv, norm_mix_g, norm_ffn_g, norm_final_g, even_w_in, mlstm_igate_b, mlstm_fgate_b, mlstm_norm_g, ssd_conv_w, ssd_conv_b, ssd_dt_bias, ssd_A_log, ssd_D, ssd_norm_g, even_w_out, odd_w_in, ret_norm_g, odd_w_out, ffn_w_up, ffn_conv_w, ffn_conv_b, ffn_w_down):
    bsz, seq, d = x_prompt.shape
    dbsz, dseq, _ = x_sample.shape
    n_pad = SEQ_PAD - dseq
    assert norm_mix_g.shape[0] == 2 and B_CONV - 1 <= n_pad < SEQ_PAD and seq % (4 * CHUNK) == 0

    row = lambda a: a.reshape(1, -1)
    gate_bias = jnp.concatenate([_pad_lanes(row(mlstm_igate_b[0])), _pad_lanes(row(mlstm_fgate_b[0])),
                                 _pad_lanes(row(ssd_dt_bias[0]))], axis=0)
    even_consts = (gate_bias, row(mlstm_norm_g[0]), ssd_conv_w[0], row(ssd_conv_b[0]), _pad_lanes(row(ssd_A_log[0])),
                   row(jnp.repeat(ssd_D[0], B_HEADDIM)), row(ssd_norm_g[0]))
    g_mix0, g_mix1 = row(norm_mix_g[0]), row(norm_mix_g[1])
    g_ffn0, g_ffn1, g_final = row(norm_ffn_g[0]), row(norm_ffn_g[1]), row(norm_final_g)
    w_even_in, w_even_out = _even_w_in_cols(even_w_in[0]), even_w_out[0].astype(bf16)
    w_odd_in, w_odd_out = odd_w_in[0].astype(bf16), odd_w_out[0].astype(bf16)
    w_up, w_dn = ffn_w_up.astype(bf16), ffn_w_down.astype(bf16)
    ffn_cb = ffn_conv_b[:, None, :]
    ret_norm = row(ret_norm_g[0])
    cw = D_FF

    cos_p, sin_p = _rope_tables(np.arange(seq))
    xp = x_prompt.reshape(bsz * seq, d)
    xp, pC, pn, pm, pST, p_conv = _even_prompt(xp, g_mix0, w_even_in, even_consts, w_even_out,
                                               seq_rows=seq, L=CHUNK, n_sub=2)
    xp, p_f0 = _ffn(xp, None, g_ffn0, w_up[0], ffn_conv_w[0], ffn_cb[0], w_dn[0], None,
                    seq_rows=seq, tt=512, cw=cw)
    xp, pSr = _odd_prompt(xp, g_mix1, w_odd_in, cos_p, sin_p, _retention_tables(CHUNK, 0) + (ret_norm,), w_odd_out,
                          seq_rows=seq, L=CHUNK, n_sub=2)
    _, p_f1, yp = _ffn(xp, None, g_ffn1, w_up[1], ffn_conv_w[1], ffn_cb[1], w_dn[1], g_final,
                       seq_rows=seq, tt=512, cw=cw)
    pS = pST.reshape(bsz, B_STATE, B_HEADS, B_HEADDIM).transpose(0, 2, 3, 1)

    xs = jnp.pad(x_sample, ((0, 0), (n_pad, 0), (0, 0))).reshape(dbsz * SEQ_PAD, d)
    n_seg = 8
    cos_s, sin_s = _rope_tables(np.tile(PAST_LEN + np.arange(SEQ_PAD) - n_pad, n_seg))
    hist = lambda a: jnp.pad(a, ((0, 0), (n_pad - a.shape[1], dseq), (0, 0))).reshape(dbsz * SEQ_PAD, a.shape[-1])
    state = (hist(state_ssd_conv[0]), state_mlstm_C[0],
             jnp.pad(state_mlstm_n[0], ((0, 0), (0, SUBLANES - A_HEADS), (0, 0))),
             _pad_lanes(state_mlstm_m[0])[:, None, :],
             state_ssd_h[0].reshape(dbsz, B_HEADS // 2, 2 * B_HEADDIM, B_STATE))
    to_tm = lambda a: a.reshape(dbsz, SEQ_PAD, d)[:, n_pad:, :].transpose(1, 0, 2).reshape(dseq * dbsz, d)
    from_tm = lambda a: jnp.pad(a.reshape(dseq, dbsz, d).transpose(1, 0, 2),
                                ((0, 0), (n_pad, 0), (0, 0))).reshape(dbsz * SEQ_PAD, d)
    hist_ffn = state_ffn_conv.reshape(2, dbsz, (FFN_CONV - 1) * 2 * D_FF)
    tm = 256
    proj = _norm_proj(xs, g_mix0, w_even_in, EVEN_SPLITS, tm)
    y, sC, sn, sm, sS, s_conv = _even_sample(proj, state, even_consts, n_seg=n_seg, n_pad=n_pad)
    sS = sS.reshape(dbsz, B_HEADS, B_HEADDIM, B_STATE)
    xs = _proj_res(xs, y, w_even_out, tm)
    xt, s_f0 = _ffn(to_tm(xs), hist_ffn[0], g_ffn0, w_up[0], ffn_conv_w[0], ffn_cb[0], w_dn[0], None,
                    seq_rows=dseq, tt=None, cw=D_FF // 2)
    xs = from_tm(xt)
    proj = _norm_proj(xs, g_mix1, w_odd_in, ODD_SPLITS, tm)
    y, sSr = _odd_sample(proj, state_ret_S[0], cos_s, sin_s,
                         _retention_tables(SEQ_PAD, n_pad, n_seg) + (ret_norm,), n_seg=n_seg)
    xs = _proj_res(xs, y, w_odd_out, tm)
    _, s_f1, yt = _ffn(to_tm(xs), hist_ffn[1], g_ffn1, w_up[1], ffn_conv_w[1], ffn_cb[1], w_dn[1], g_final,
                       seq_rows=dseq, tt=None, cw=D_FF // 2)

    def conv_rows(tail, bs):
        return tail.reshape(bs, SUBLANES, tail.shape[-1])[:, SUBLANES - (B_CONV - 1):, :]

    def mlstm_states(C, n, m):
        return C[None], n[None, :, :A_HEADS, :], m[None, :, 0, :A_HEADS]

    p_ff = jnp.stack([p_f0, p_f1]).reshape(2, bsz, SUBLANES, 2 * D_FF)[:, :, SUBLANES - (FFN_CONV - 1):, :]
    s_ff = jnp.stack([s_f0, s_f1]).reshape(2, dbsz, FFN_CONV - 1, 2 * D_FF)
    p_states = mlstm_states(pC, pn, pm) + (conv_rows(p_conv, bsz)[None], pS[None], pSr[None], p_ff)
    s_states = mlstm_states(sC, sn, sm) + (conv_rows(s_conv, dbsz)[None], sS[None], sSr[None], s_ff)
    y_prompt = yp.reshape(bsz, seq, d)
    y_sample = yt.reshape(dseq, dbsz, d).transpose(1, 0, 2)
    return (y_prompt, y_sample) + p_states + s_states
```

```python
import functools

import numpy as np
import jax
import jax.numpy as jnp
from jax import lax
from jax.experimental import pallas as pl
from jax.experimental.pallas import tpu as pltpu

f32 = jnp.float32
bf16 = jnp.bfloat16

EPS = 1e-6
CHUNK = 128
D_MODEL = 1024
A_HEADS, A_DK, A_DV = 4, 128, 256
A_WIDTH = A_HEADS * A_DV
B_HEADS, B_HEADDIM, B_GROUPS, B_STATE, B_CONV = 16, 64, 2, 128, 4
B_WIDTH = B_HEADS * B_HEADDIM
B_CONV_DIM = B_WIDTH + 2 * B_GROUPS * B_STATE
B_HEADS_PER_GROUP = B_HEADS // B_GROUPS
B_GROUP_WIDTH = B_WIDTH // B_GROUPS
C_HEADS, C_DK, C_DV = 8, 128, 256
C_WIDTH = C_HEADS * C_DV
ROPE_BASE = 10000.0
D_FF = 2816
FFN_CONV = 3
PAST_LEN = 16384

LANES = 128
SUBLANES = 8
SEQ_PAD = SUBLANES
NEG_BIG = -1e30
VMEM_LIMIT = 56 * 1024 * 1024

EVEN_SPLITS = (A_HEADS * A_DK, A_HEADS * A_DK, A_WIDTH, A_WIDTH, B_WIDTH, B_CONV_DIM, 3 * LANES)
ODD_SPLITS = (C_HEADS * C_DK, C_HEADS * C_DK, C_WIDTH, C_WIDTH)


def _params(sem):
    return pltpu.CompilerParams(dimension_semantics=sem, vmem_limit_bytes=VMEM_LIMIT)


def _resident(shape, index_map):
    return pl.BlockSpec(shape, index_map, pipeline_mode=pl.Buffered(1))


def _nt(a, b):
    return lax.dot_general(a, b, (((1,), (1,)), ((), ())), preferred_element_type=f32)


def _tn(a, b):
    return lax.dot_general(a, b, (((0,), (0,)), ((), ())), preferred_element_type=f32)


def _nn(a, b):
    return jnp.dot(a, b, preferred_element_type=f32)


def _rmsnorm(x, g):
    return x * lax.rsqrt(jnp.mean(x * x, axis=-1, keepdims=True) + EPS) * g


def _layernorm_nogain(h):
    mu = jnp.mean(h, axis=-1, keepdims=True)
    hc = h - mu
    return hc * lax.rsqrt(jnp.mean(hc * hc, axis=-1, keepdims=True) + EPS)


def _softplus(x):
    return jnp.maximum(x, 0.0) + jnp.log1p(jnp.exp(-jnp.abs(x)))


def _silu(x):
    return x * jax.nn.sigmoid(x)


def _split3(x):
    hi = x.astype(bf16)
    r1 = x - hi.astype(f32)
    mid = r1.astype(bf16)
    lo = (r1 - mid.astype(f32)).astype(bf16)
    return hi, mid, lo


def _cumsum_rows(tril, x):
    hi, mid, lo = _split3(x)
    return _nn(tril, hi) + _nn(tril, mid) + _nn(tril, lo)


def _split_views(ref, splits):
    views, off = [], 0
    for n in splits:
        views.append(ref.at[:, off:off + n])
        off += n
    return views


def _project(h, w_ref, p_ref, splits):
    off = 0
    for n in splits:
        p_ref[:, off:off + n] = _nn(h, w_ref[:, off:off + n])
        off += n


MXU_COLS = 256


def _proj_pieces(h_ref, w_ref, dst_ref):
    def piece(c0, n):
        def run():
            dst_ref[:, c0:c0 + n] = _nn(h_ref[...], w_ref[:, c0:c0 + n])
        return run
    total = w_ref.shape[1]
    return [piece(c0, min(MXU_COLS, total - c0)) for c0 in range(0, total, MXU_COLS)]


def _out_pieces(x_ref, y_ref, w_ref, o_ref, rows):
    def piece(c0, n):
        def run():
            o_ref[rows, c0:c0 + n] = x_ref[rows, c0:c0 + n] + _nn(y_ref[...], w_ref[:, c0:c0 + n])
        return run
    total = w_ref.shape[1]
    return [piece(c0, min(MXU_COLS, total - c0)) for c0 in range(0, total, MXU_COLS)]


def _spread(work, n_slots):
    return [work[i * len(work) // n_slots:(i + 1) * len(work) // n_slots] for i in range(n_slots)]


def _norm_proj_body(x_ref, g_ref, w_ref, *o_refs, splits):
    h = _rmsnorm(x_ref[...], g_ref[...]).astype(bf16)
    off = 0
    for o_ref, n in zip(o_refs, splits):
        o_ref[...] = _nn(h, w_ref[:, off:off + n])
        off += n


def _norm_proj(x, g, w, splits, tm):
    rows, d = x.shape
    return pl.pallas_call(
        functools.partial(_norm_proj_body, splits=splits),
        grid=(rows // tm,),
        in_specs=[pl.BlockSpec((tm, d), lambda i: (i, 0)),
                  pl.BlockSpec((1, d), lambda i: (0, 0)),
                  _resident(w.shape, lambda i: (0, 0))],
        out_specs=[pl.BlockSpec((tm, n), lambda i: (i, 0)) for n in splits],
        out_shape=[jax.ShapeDtypeStruct((rows, n), f32) for n in splits],
        compiler_params=_params(("arbitrary",)),
        name="norm_proj",
    )(x, g, w)


def _proj_res_body(x_ref, y_ref, w_ref, o_ref):
    o_ref[...] = x_ref[...] + _nn(y_ref[...].astype(bf16), w_ref[...])


def _proj_res(x, y, w, tm):
    rows, d = x.shape
    k = y.shape[1]
    return pl.pallas_call(
        _proj_res_body,
        grid=(rows // tm,),
        in_specs=[pl.BlockSpec((tm, d), lambda i: (i, 0)),
                  pl.BlockSpec((tm, k), lambda i: (i, 0)),
                  _resident(w.shape, lambda i: (0, 0))],
        out_specs=pl.BlockSpec((tm, d), lambda i: (i, 0)),
        out_shape=jax.ShapeDtypeStruct((rows, d), f32),
        compiler_params=_params(("arbitrary",)),
        name="proj_res",
    )(x, y, w)


def _conv3(bias, prev2, prev1, cur, w_ref):
    y = bias + prev2 * w_ref[0:1, :]
    y = y + prev1 * w_ref[1:2, :]
    return y + cur * w_ref[2:3, :]


def _ffn_body(*refs, tt, final):
    it = iter(refs)
    x_ref, g_ref, wup_ref, cw_ref, cb_ref, wdn_ref = (next(it) for _ in range(6))
    gf_ref = next(it) if final else None
    o_ref, tail_ref = next(it), next(it)
    yn_ref = next(it) if final else None
    scr, carry = next(it), next(it)

    @pl.when(pl.program_id(1) == 0)
    def _():
        carry[...] = jnp.zeros(carry.shape, f32)

    x = x_ref[...]
    h = _rmsnorm(x, g_ref[...]).astype(bf16)
    conv = []
    for part in range(2):
        cols = slice(part * D_FF, (part + 1) * D_FF)
        u = _nn(h, wup_ref[:, cols])
        scr[0:SUBLANES, :] = carry[:, cols]
        tail_ref[:, cols] = u[tt - SUBLANES:tt, :]
        carry[:, cols] = u[tt - SUBLANES:tt, :]
        scr[SUBLANES:SUBLANES + tt, :] = u
        conv.append(_conv3(cb_ref[:, cols], scr[SUBLANES - 2:SUBLANES - 2 + tt, :],
                           scr[SUBLANES - 1:SUBLANES - 1 + tt, :], u, cw_ref.at[:, cols]))
    act = (_silu(conv[0]) * conv[1]).astype(bf16)
    out = x + _nn(act, wdn_ref[...])
    o_ref[...] = out
    if final:
        yn_ref[...] = _rmsnorm(out, gf_ref[...])


def _ffn(x, g, wup, conv_w, conv_b, wdn, g_final, *, seq_rows, tt):
    rows, d = x.shape
    final = g_final is not None
    n_seq = rows // seq_rows
    n_inner = seq_rows // tt
    rmap = lambda s, t: (s * n_inner + t, 0)
    cmap = lambda s, t: (0, 0)
    in_specs = [pl.BlockSpec((tt, d), rmap), pl.BlockSpec((1, d), cmap), _resident(wup.shape, cmap),
                pl.BlockSpec(conv_w.shape, cmap), pl.BlockSpec(conv_b.shape, cmap), _resident(wdn.shape, cmap)]
    args = [x, g, wup, conv_w, conv_b, wdn]
    out_specs = [pl.BlockSpec((tt, d), rmap), pl.BlockSpec((SUBLANES, 2 * D_FF), lambda s, t: (s, 0))]
    out_shape = [jax.ShapeDtypeStruct((rows, d), f32), jax.ShapeDtypeStruct((n_seq * SUBLANES, 2 * D_FF), f32)]
    if final:
        in_specs.append(pl.BlockSpec((1, d), cmap))
        args.append(g_final)
        out_specs.append(pl.BlockSpec((tt, d), rmap))
        out_shape.append(jax.ShapeDtypeStruct((rows, d), f32))
    return pl.pallas_call(
        functools.partial(_ffn_body, tt=tt, final=final),
        grid=(n_seq, n_inner), in_specs=in_specs, out_specs=out_specs, out_shape=out_shape,
        scratch_shapes=[pltpu.VMEM((SUBLANES + tt, D_FF), f32), pltpu.VMEM((SUBLANES, 2 * D_FF), f32)],
        compiler_params=_params(("arbitrary", "arbitrary")),
        name="ffn_prompt",
    )(*args)


def _ffn_sample_body(*refs, nb, nj, final):
    it = iter(refs)
    x_ref, hist_ref, g_ref, wup_ref, cw_ref, cb_ref, wdn_ref = (next(it) for _ in range(7))
    gf_ref = next(it) if final else None
    o_ref, tail_ref = next(it), next(it)
    yn_ref = next(it) if final else None
    h_scr, convg_scr, acc_scr = next(it), next(it), next(it)
    c = pl.program_id(0)
    tt = x_ref.shape[0]

    @pl.when(c == 0)
    def _():
        h_scr[...] = _rmsnorm(x_ref[...], g_ref[...]).astype(bf16)
        acc_scr[...] = jnp.zeros(acc_scr.shape, f32)

    u = _nn(h_scr[...], wup_ref[...])
    ext = jnp.concatenate([hist_ref[:, k, :] for k in range(FFN_CONV - 1)] + [u], axis=0)
    for k in range(FFN_CONV - 1):
        tail_ref[:, k, :] = ext[tt + k * nb:tt + (k + 1) * nb, :]
    y = _conv3(cb_ref[...], ext[0:tt, :], ext[nb:nb + tt, :], u, cw_ref)

    @pl.when(c < nj)
    def _():
        convg_scr[c] = y

    @pl.when(c >= nj)
    def _():
        act = (_silu(convg_scr[c - nj]) * y).astype(bf16)
        acc_scr[...] += _nn(act, wdn_ref[...])

    @pl.when(c == 2 * nj - 1)
    def _():
        out = x_ref[...] + acc_scr[...]
        o_ref[...] = out
        if final:
            yn_ref[...] = _rmsnorm(out, gf_ref[...])


def _ffn_sample(x, hist, layer, g, wup, conv_w, conv_b, wdn, g_final, *, cw):
    rows, d = x.shape
    nb = hist.shape[1]
    nj = D_FF // cw
    final = g_final is not None
    fixed = lambda c: (0, 0)
    colblk = lambda c: (0, c)
    in_specs = [pl.BlockSpec((rows, d), fixed),
                pl.BlockSpec((None, nb, FFN_CONV - 1, cw), lambda c: (layer, 0, 0, c)),
                pl.BlockSpec((1, d), fixed), pl.BlockSpec((d, cw), colblk), pl.BlockSpec((FFN_CONV, cw), colblk),
                pl.BlockSpec((1, cw), colblk), pl.BlockSpec((cw, d), lambda c: (jnp.maximum(c - nj, 0), 0))]
    args = [x, hist, g, wup, conv_w, conv_b, wdn]
    out_specs = [pl.BlockSpec((rows, d), fixed), pl.BlockSpec((nb, FFN_CONV - 1, cw), lambda c: (0, 0, c))]
    out_shape = [jax.ShapeDtypeStruct((rows, d), f32), jax.ShapeDtypeStruct((nb, FFN_CONV - 1, 2 * D_FF), f32)]
    if final:
        in_specs.append(pl.BlockSpec((1, d), fixed))
        args.append(g_final)
        out_specs.append(pl.BlockSpec((rows, d), fixed))
        out_shape.append(jax.ShapeDtypeStruct((rows, d), f32))
    return pl.pallas_call(
        functools.partial(_ffn_sample_body, nb=nb, nj=nj, final=final),
        grid=(2 * nj,), in_specs=in_specs, out_specs=out_specs, out_shape=out_shape,
        scratch_shapes=[pltpu.VMEM((rows, d), bf16), pltpu.VMEM((nj, rows, cw), f32), pltpu.VMEM((rows, d), f32)],
        compiler_params=_params(("arbitrary",)),
        name="ffn_sample",
    )(*args)


EVEN_SLOTS = A_HEADS + B_HEADS // 2


def _even_chunk(r0, slot, L, n_pad, n_seg, proj_refs, hist_ref, const_refs, y_ref, state_in_refs, state_refs,
                tail_ref, conv_scr, yb_scr, side=None):
    q_ref, k_ref, v_ref, og_ref, z_ref, xbc_ref, gt_ref = proj_refs
    gb_ref, anorm_ref, convw_ref, convb_ref, alog_ref, dx_ref, bnorm_ref = const_refs
    C_in, n_in, m_in, S_in = state_in_refs
    C_ref, n_ref, m_ref, S_ref = state_refs
    rows = pl.ds(r0, L)
    seg = L // n_seg
    seg_rows = [slice(i * seg, (i + 1) * seg) for i in range(n_seg)]
    seg_last = [slice((i + 1) * seg - 1, (i + 1) * seg) for i in range(n_seg)]

    def per_row(vals):
        if n_seg == 1:
            return vals[0]
        return jnp.concatenate([jnp.broadcast_to(v, (seg, v.shape[1])) for v in vals], axis=0)

    ri = lax.broadcasted_iota(jnp.int32, (L, L), 0)
    ci = lax.broadcasted_iota(jnp.int32, (L, L), 1)
    causal = ri >= ci
    if n_seg > 1:
        causal = causal & ((ri // seg) == (ci // seg))
    tril = causal.astype(bf16)
    valid = None
    if n_pad:
        valid = (lax.broadcasted_iota(jnp.int32, (L, 1), 0) % seg) >= n_pad

    gates = gt_ref[rows, :]
    li = gates[:, 0:LANES] + gb_ref[0:1, :]
    fpre = gates[:, LANES:2 * LANES] + gb_ref[1:2, :]
    lf = -_softplus(-fpre)
    dt = _softplus(gates[:, 2 * LANES:3 * LANES] + gb_ref[2:3, :])
    if n_pad:
        li = jnp.where(valid, li, NEG_BIG)
        lf = jnp.where(valid, lf, 0.0)
        dt = jnp.where(valid, dt, 0.0)
    a = dt * (-jnp.exp(alog_ref[...]))
    cums = _cumsum_rows(tril, jnp.concatenate([lf, a], axis=1))
    bcum = cums[:, 0:LANES]
    acum = cums[:, LANES:2 * LANES]
    liT, bT, aT, dtT = li.T, bcum.T, acum.T, dt.T

    m_old = [m_in[slot + i] for i in range(n_seg)]
    b_lasts = [bcum[r, :] for r in seg_last]
    m_rows, b_last = per_row(m_old), per_row(b_lasts)
    log_g = b_last - bcum + li
    m_news = [jnp.maximum(b_lasts[i] + m_old[i], jnp.max(log_g[seg_rows[i], :], axis=0, keepdims=True))
              for i in range(n_seg)]
    m_new = per_row(m_news)
    gfac = jnp.exp(log_g - m_new)
    cdecay = jnp.exp(b_last + m_rows - m_new)
    for i in range(n_seg):
        m_ref[slot + i] = m_news[i]
    for h in range(A_HEADS):
        for run in (side[h] if side else ()):
            run()
        bcol, brow, lirow = bcum[:, h:h + 1], bT[h:h + 1, :], liT[h:h + 1, :]
        m_h = m_rows[:, h:h + 1]
        logw = jnp.where(causal, bcol - brow + lirow, -jnp.inf)
        log_prev = bcol + m_h
        m_t = jnp.maximum(log_prev, jnp.max(logw, axis=-1, keepdims=True))
        w_in = jnp.exp(logw - m_t)
        w_prev = jnp.exp(log_prev - m_t)
        qh = q_ref[rows, h * A_DK:(h + 1) * A_DK]
        kh = k_ref[rows, h * A_DK:(h + 1) * A_DK] * (A_DK ** -0.5)
        vh = v_ref[rows, h * A_DV:(h + 1) * A_DV]
        vb = vh.astype(bf16)
        qb = qh.astype(bf16)
        s = _nt(qb, kh.astype(bf16)) * w_in
        kg = kh * gfac[:, h:h + 1]
        Cs = [C_in[slot + i, h] for i in range(n_seg)]
        ns = [n_in[slot + i, h:h + 1, :] for i in range(n_seg)]
        if n_seg == 1:
            inter = _nn(qb, Cs[0].astype(bf16))
            dec_h = cdecay[:, h:h + 1]
            C_ref[slot, h] = Cs[0] * dec_h + _tn(kg.astype(bf16), vb)
            n_ref[slot, h:h + 1, :] = ns[0] * dec_h + jnp.sum(kg, axis=0, keepdims=True)
        else:
            inter = jnp.concatenate([_nn(qh[seg_rows[i], :].astype(bf16), Cs[i].astype(bf16))
                                     for i in range(n_seg)], axis=0)
            for i in range(n_seg):
                rs = seg_rows[i]
                dec_h = cdecay[seg_last[i], h:h + 1]
                C_ref[slot + i, h] = Cs[i] * dec_h + _tn(kg[rs, :].astype(bf16), vh[rs, :].astype(bf16))
                n_ref[slot + i, h:h + 1, :] = ns[i] * dec_h + jnp.sum(kg[rs, :], axis=0, keepdims=True)
        num = _nn(s.astype(bf16), vb) + inter * w_prev
        den = jnp.sum(s, axis=-1, keepdims=True) + jnp.sum(qh * per_row(ns), axis=-1, keepdims=True) * w_prev
        hout = num / jnp.maximum(jnp.abs(den), jnp.exp(-m_t))
        ya = _layernorm_nogain(hout) * anorm_ref[:, h * A_DV:(h + 1) * A_DV]
        ya = ya * jax.nn.sigmoid(og_ref[rows, h * A_DV:(h + 1) * A_DV])
        y_ref[rows, h * A_DV:(h + 1) * A_DV] = ya.astype(y_ref.dtype)

    xraw = xbc_ref[rows, :]
    if hist_ref is not None:
        seg_row = lax.broadcasted_iota(jnp.int32, (L, 1), 0) % seg
        is_hist = (seg_row >= n_pad - (B_CONV - 1)) & jnp.logical_not(valid)
        xraw = jnp.where(is_hist, hist_ref[rows, :], xraw)
    conv_scr[SUBLANES:SUBLANES + L, :] = xraw
    xc = convb_ref[...] + conv_scr[SUBLANES - 3:SUBLANES - 3 + L, :] * convw_ref[0:1, :]
    xc = xc + conv_scr[SUBLANES - 2:SUBLANES - 2 + L, :] * convw_ref[1:2, :]
    xc = xc + conv_scr[SUBLANES - 1:SUBLANES - 1 + L, :] * convw_ref[2:3, :]
    xc = xc + xraw * convw_ref[3:4, :]
    if n_seg == 1:
        new_tail = conv_scr[L:L + SUBLANES, :]
        tail_ref[...] = new_tail
        conv_scr[0:SUBLANES, :] = new_tail
    else:
        tail_ref[...] = xraw
    xc = _silu(xc)

    a_last = per_row([acum[r, :] for r in seg_last])
    wtile = jnp.exp(a_last - acum) * dt
    expa = jnp.exp(acum)

    def head_matrix(cb, h):
        acol, arow, dtrow = acum[:, h:h + 1], aT[h:h + 1, :], dtT[h:h + 1, :]
        decay = jnp.where(causal, jnp.exp(jnp.where(causal, acol - arow, 0.0)), 0.0)
        return (cb * decay * dtrow).astype(bf16)

    def group_bc(g):
        Bg = xc[:, B_WIDTH + g * B_STATE:B_WIDTH + (g + 1) * B_STATE].astype(bf16)
        c0 = B_WIDTH + B_GROUPS * B_STATE + g * B_STATE
        Cg = xc[:, c0:c0 + B_STATE].astype(bf16)
        return Bg, Cg, _nt(Cg, Bg)

    low_half = lax.broadcasted_iota(jnp.int32, (L, LANES), 1) < B_HEADDIM
    first_head_rows = lax.broadcasted_iota(jnp.int32, (2 * B_HEADDIM, 1), 0) < B_HEADDIM

    def pair_lanes(t, j):
        return jnp.where(low_half, jnp.broadcast_to(t[:, 2 * j:2 * j + 1], (L, LANES)),
                         jnp.broadcast_to(t[:, 2 * j + 1:2 * j + 2], (L, LANES)))

    pairs_per_group = B_HEADS_PER_GROUP // 2
    for g in range(B_GROUPS):
        Bg, Cg, cb = group_bc(g)
        gs = slice(g * B_GROUP_WIDTH, (g + 1) * B_GROUP_WIDTH)
        if n_seg == 1:
            ST = S_in[slot, :, gs]
            inter_g = _nn(Cg, ST.astype(bf16))
        else:
            Bf = xc[:, B_WIDTH + g * B_STATE:B_WIDTH + (g + 1) * B_STATE]
            c0 = B_WIDTH + B_GROUPS * B_STATE + g * B_STATE
            Cf = xc[:, c0:c0 + B_STATE]
        xw, decs = [], []
        for jp in range(pairs_per_group):
            j = g * pairs_per_group + jp
            for run in (side[A_HEADS + j] if side else ()):
                run()
            ps = slice(j * LANES, (j + 1) * LANES)
            e_pair, w_pair = pair_lanes(expa, j), pair_lanes(wtile, j)
            mcat = jnp.concatenate([head_matrix(cb, 2 * j), head_matrix(cb, 2 * j + 1)], axis=1)
            xp = xc[:, ps]
            xbd = jnp.concatenate([jnp.where(low_half, xp, 0.0).astype(bf16),
                                   jnp.where(low_half, 0.0, xp).astype(bf16)], axis=0)
            if n_seg == 1:
                inter = inter_g[:, jp * LANES:(jp + 1) * LANES]
                xw.append((xp * w_pair).astype(bf16))
                decs.append(e_pair[L - 1:L, :])
            else:
                xwf = xp * w_pair
                parts = []
                for i in range(n_seg):
                    rs = seg_rows[i]
                    Sp = S_in[slot + i, j]
                    parts.append(_nt(Cf[rs, :].astype(bf16), Sp.astype(bf16)))
                    e_last = e_pair[seg_last[i], :]
                    dec_col = jnp.where(first_head_rows, e_last[:, 0:1], e_last[:, B_HEADDIM:B_HEADDIM + 1])
                    S_ref[slot + i, j] = Sp * dec_col + _tn(xwf[rs, :].astype(bf16), Bf[rs, :].astype(bf16))
                inter = jnp.concatenate(parts, axis=0)
            yb_scr[:, ps] = (_nn(mcat, xbd) + inter * e_pair) + dx_ref[:, ps] * xp
        if n_seg == 1:
            S_ref[slot, :, gs] = ST * jnp.concatenate(decs, axis=1) + _tn(Bg, jnp.concatenate(xw, axis=1))
    yb = yb_scr[...] * _silu(z_ref[rows, :])
    for g in range(B_GROUPS):
        gs = slice(g * B_GROUP_WIDTH, (g + 1) * B_GROUP_WIDTH)
        yg = yb[:, gs]
        yg = yg * lax.rsqrt(jnp.mean(yg * yg, axis=-1, keepdims=True) + EPS)
        y_ref[rows, A_WIDTH + g * B_GROUP_WIDTH:A_WIDTH + (g + 1) * B_GROUP_WIDTH] = (
            yg * bnorm_ref[:, gs]).astype(y_ref.dtype)


N_EVEN_CONSTS = 7


def _even_prompt_body(*refs, L, n_sub):
    it = iter(refs)
    x_ref, xnext_ref, gmix_ref, win_ref = next(it), next(it), next(it), next(it)
    const_refs = tuple(next(it) for _ in range(N_EVEN_CONSTS))
    wout_ref = next(it)
    o_ref = next(it)
    state_refs = tuple(next(it) for _ in range(4))
    tail_ref = next(it)
    p_scr, y_scr = (next(it), next(it)), (next(it), next(it))
    h_scr, conv_scr, yb_scr = (next(it) for _ in range(3))
    th = L * n_sub
    gmix = gmix_ref[...]

    @pl.when((pl.program_id(0) == 0) & (pl.program_id(1) == 0))
    def _():
        _project(_rmsnorm(x_ref[0:th, :], gmix).astype(bf16), win_ref, p_scr[0], EVEN_SPLITS)

    @pl.when(pl.program_id(1) == 0)
    def _():
        conv_scr[0:SUBLANES, :] = jnp.zeros((SUBLANES, B_CONV_DIM), f32)
        for r in state_refs:
            r[...] = jnp.zeros(r.shape, f32)

    for half in range(2):
        ahead = xnext_ref[...] if half else x_ref[th:2 * th, :]
        h_scr[...] = _rmsnorm(ahead, gmix).astype(bf16)
        work = _proj_pieces(h_scr, win_ref, p_scr[1 - half])
        if half:
            work = work + _out_pieces(x_ref, y_scr[0], wout_ref, o_ref, slice(0, th))
        side = _spread(work, n_sub * EVEN_SLOTS)
        proj_refs = _split_views(p_scr[half], EVEN_SPLITS)
        for c in range(n_sub):
            _even_chunk(c * L, 0, L, 0, 1, proj_refs, None, const_refs, y_scr[half], state_refs, state_refs,
                        tail_ref, conv_scr, yb_scr, side[c * EVEN_SLOTS:(c + 1) * EVEN_SLOTS])
    for run in _out_pieces(x_ref, y_scr[1], wout_ref, o_ref, slice(th, 2 * th)):
        run()


def _even_prompt(x, gmix, win, consts, wout, *, seq_rows, L, n_sub):
    rows, d = x.shape
    n_seq = rows // seq_rows
    th = L * n_sub
    tt = 2 * th
    n_inner = seq_rows // tt
    n_half = rows // th
    rmap = lambda s, t: (s * n_inner + t, 0)
    nextmap = lambda s, t: (jnp.minimum(2 * (s * n_inner + t) + 2, n_half - 1), 0)
    cmap = lambda s, t: (0, 0)
    smap4 = lambda s, t: (s, 0, 0, 0)
    smap3 = lambda s, t: (s, 0, 0)
    st_specs = [pl.BlockSpec((1, A_HEADS, A_DK, A_DV), smap4), pl.BlockSpec((1, SUBLANES, A_DK), smap3),
                pl.BlockSpec((1, 1, LANES), smap3), pl.BlockSpec((1, B_STATE, B_WIDTH), smap3)]
    st_shapes = [jax.ShapeDtypeStruct((n_seq, A_HEADS, A_DK, A_DV), f32),
                 jax.ShapeDtypeStruct((n_seq, SUBLANES, A_DK), f32),
                 jax.ShapeDtypeStruct((n_seq, 1, LANES), f32),
                 jax.ShapeDtypeStruct((n_seq, B_STATE, B_WIDTH), f32)]
    in_specs = ([pl.BlockSpec((tt, d), rmap), pl.BlockSpec((th, d), nextmap), pl.BlockSpec((1, d), cmap),
                 _resident(win.shape, cmap)]
                + [pl.BlockSpec(c.shape, cmap) for c in consts] + [_resident(wout.shape, cmap)])
    p_shape = pltpu.VMEM((th, sum(EVEN_SPLITS)), f32)
    y_shape = pltpu.VMEM((th, A_WIDTH + B_WIDTH), bf16)
    return pl.pallas_call(
        functools.partial(_even_prompt_body, L=L, n_sub=n_sub),
        grid=(n_seq, n_inner), in_specs=in_specs,
        out_specs=[pl.BlockSpec((tt, d), rmap)] + st_specs + [pl.BlockSpec((SUBLANES, B_CONV_DIM), lambda s, t: (s, 0))],
        out_shape=[jax.ShapeDtypeStruct((rows, d), f32)] + st_shapes
        + [jax.ShapeDtypeStruct((n_seq * SUBLANES, B_CONV_DIM), f32)],
        scratch_shapes=[p_shape, p_shape, y_shape, y_shape, pltpu.VMEM((th, d), bf16),
                        pltpu.VMEM((SUBLANES + L, B_CONV_DIM), f32), pltpu.VMEM((L, B_WIDTH), f32)],
        compiler_params=_params(("arbitrary", "arbitrary")),
        name="even_prompt",
    )(x, x, gmix, win, *consts, wout)


def _even_sample_body(*refs, L, n_seg, n_pad):
    it = iter(refs)
    proj_refs = tuple(next(it) for _ in range(7))
    hist_ref = next(it)
    state_in_refs = tuple(next(it) for _ in range(4))
    const_refs = tuple(next(it) for _ in range(N_EVEN_CONSTS))
    y_ref = next(it)
    state_refs = tuple(next(it) for _ in range(4))
    tail_ref = next(it)
    conv_scr, yb_scr = next(it), next(it)

    conv_scr[0:SUBLANES, :] = jnp.zeros((SUBLANES, B_CONV_DIM), f32)
    n_ref = state_refs[1]
    n_ref[:, A_HEADS:, :] = jnp.zeros((n_seg, SUBLANES - A_HEADS, A_DK), f32)
    _even_chunk(0, 0, L, n_pad, n_seg, proj_refs, hist_ref, const_refs, y_ref, state_in_refs, state_refs,
                tail_ref, conv_scr, yb_scr)


def _even_sample(proj, state, consts, *, n_seg, n_pad):
    rows = proj[0].shape[0]
    L = SEQ_PAD * n_seg
    rmap = lambda o: (o, 0)
    cmap = lambda o: (0, 0)
    smap4 = lambda o: (o, 0, 0, 0)
    smap3 = lambda o: (o, 0, 0)
    st_specs = [pl.BlockSpec((n_seg, A_HEADS, A_DK, A_DV), smap4), pl.BlockSpec((n_seg, SUBLANES, A_DK), smap3),
                pl.BlockSpec((n_seg, 1, LANES), smap3),
                pl.BlockSpec((n_seg, B_HEADS // 2, 2 * B_HEADDIM, B_STATE), smap4)]
    st_shapes = [jax.ShapeDtypeStruct(s.shape, f32) for s in state[1:]]
    in_specs = ([pl.BlockSpec((L, a.shape[1]), rmap) for a in proj] + [pl.BlockSpec((L, B_CONV_DIM), rmap)]
                + st_specs + [pl.BlockSpec(c.shape, cmap) for c in consts])
    return pl.pallas_call(
        functools.partial(_even_sample_body, L=L, n_seg=n_seg, n_pad=n_pad),
        grid=(rows // L,), in_specs=in_specs,
        out_specs=[pl.BlockSpec((L, A_WIDTH + B_WIDTH), rmap)] + st_specs + [pl.BlockSpec((L, B_CONV_DIM), rmap)],
        out_shape=[jax.ShapeDtypeStruct((rows, A_WIDTH + B_WIDTH), f32)] + st_shapes
        + [jax.ShapeDtypeStruct((rows, B_CONV_DIM), f32)],
        scratch_shapes=[pltpu.VMEM((SUBLANES + L, B_CONV_DIM), f32), pltpu.VMEM((L, B_WIDTH), f32)],
        compiler_params=_params(("arbitrary",)),
        name="even_sample",
    )(*proj, *state, *consts)


def _odd_chunk(r0, rope_r0, slot, L, n_seg, proj_refs, const_refs, y_ref, S_in, S_ref, side=None):
    q_ref, k_ref, v_ref, g_ref = proj_refs
    cos_ref, sin_ref, intra_ref, cross_ref, into_ref, cdec_ref, norm_ref = const_refs
    rows = pl.ds(r0, L)
    seg = L // n_seg
    cosf = cos_ref[pl.ds(rope_r0, L), :]
    sinf = sin_ref[pl.ds(rope_r0, L), :]
    cross = cross_ref[...]
    into = into_ref[...]
    cdec = cdec_ref[...]
    for h in range(C_HEADS):
        for run in (side[h] if side else ()):
            run()
        qh = q_ref[rows, h * C_DK:(h + 1) * C_DK]
        kh = k_ref[rows, h * C_DK:(h + 1) * C_DK]
        qh = qh * cosf + pltpu.roll(qh, C_DK // 2, 1) * sinf
        kh = (kh * cosf + pltpu.roll(kh, C_DK // 2, 1) * sinf) * (C_DK ** -0.5)
        vh = v_ref[rows, h * C_DV:(h + 1) * C_DV]
        vb = vh.astype(bf16)
        qb = qh.astype(bf16)
        s = _nt(qb, kh.astype(bf16)) * intra_ref[h]
        kd = kh * into[:, h:h + 1]
        if n_seg == 1:
            S = S_in[slot, h]
            inter = _nn(qb, S.astype(bf16))
            S_ref[slot, h] = S * cdec[:, h:h + 1] + _tn(kd.astype(bf16), vb)
        else:
            parts = []
            for i in range(n_seg):
                rs = slice(i * seg, (i + 1) * seg)
                S = S_in[slot + i, h]
                parts.append(_nn(qh[rs, :].astype(bf16), S.astype(bf16)))
                S_ref[slot + i, h] = S * cdec[:, h:h + 1] + _tn(kd[rs, :].astype(bf16), vh[rs, :].astype(bf16))
            inter = jnp.concatenate(parts, axis=0)
        o = _nn(s.astype(bf16), vb) + inter * cross[:, h:h + 1]
        y = _layernorm_nogain(o) * norm_ref[:, h * C_DV:(h + 1) * C_DV]
        y = y * _silu(g_ref[rows, h * C_DV:(h + 1) * C_DV])
        y_ref[rows, h * C_DV:(h + 1) * C_DV] = y.astype(y_ref.dtype)


N_ODD_CONSTS = 7


def _odd_prompt_body(*refs, L, n_sub):
    it = iter(refs)
    x_ref, xnext_ref, gmix_ref, win_ref = next(it), next(it), next(it), next(it)
    const_refs = tuple(next(it) for _ in range(N_ODD_CONSTS))
    wout_ref = next(it)
    o_ref, S_ref = next(it), next(it)
    p_scr, y_scr = (next(it), next(it)), (next(it), next(it))
    h_scr = next(it)
    th = L * n_sub
    gmix = gmix_ref[...]

    @pl.when((pl.program_id(0) == 0) & (pl.program_id(1) == 0))
    def _():
        _project(_rmsnorm(x_ref[0:th, :], gmix).astype(bf16), win_ref, p_scr[0], ODD_SPLITS)

    @pl.when(pl.program_id(1) == 0)
    def _():
        S_ref[...] = jnp.zeros(S_ref.shape, f32)

    for half in range(2):
        ahead = xnext_ref[...] if half else x_ref[th:2 * th, :]
        h_scr[...] = _rmsnorm(ahead, gmix).astype(bf16)
        work = _proj_pieces(h_scr, win_ref, p_scr[1 - half])
        if half:
            work = work + _out_pieces(x_ref, y_scr[0], wout_ref, o_ref, slice(0, th))
        side = _spread(work, n_sub * C_HEADS)
        proj_refs = _split_views(p_scr[half], ODD_SPLITS)
        for c in range(n_sub):
            _odd_chunk(c * L, half * th + c * L, 0, L, 1, proj_refs, const_refs, y_scr[half], S_ref, S_ref,
                       side[c * C_HEADS:(c + 1) * C_HEADS])
    for run in _out_pieces(x_ref, y_scr[1], wout_ref, o_ref, slice(th, 2 * th)):
        run()


def _odd_prompt(x, gmix, win, cos, sin, consts, wout, *, seq_rows, L, n_sub):
    rows, d = x.shape
    n_seq = rows // seq_rows
    th = L * n_sub
    tt = 2 * th
    n_inner = seq_rows // tt
    n_half = rows // th
    rmap = lambda s, t: (s * n_inner + t, 0)
    nextmap = lambda s, t: (jnp.minimum(2 * (s * n_inner + t) + 2, n_half - 1), 0)
    cmap = lambda s, t: (0, 0)
    st_spec = pl.BlockSpec((1, C_HEADS, C_DK, C_DV), lambda s, t: (s, 0, 0, 0))
    rope_spec = pl.BlockSpec((tt, LANES), lambda s, t: (t, 0))
    in_specs = [pl.BlockSpec((tt, d), rmap), pl.BlockSpec((th, d), nextmap), pl.BlockSpec((1, d), cmap),
                _resident(win.shape, cmap), rope_spec, rope_spec]
    for c in consts:
        in_specs.append(pl.BlockSpec(c.shape, (lambda s, t: (0, 0, 0)) if c.ndim == 3 else cmap))
    in_specs.append(_resident(wout.shape, cmap))
    p_shape = pltpu.VMEM((th, sum(ODD_SPLITS)), f32)
    y_shape = pltpu.VMEM((th, C_WIDTH), bf16)
    return pl.pallas_call(
        functools.partial(_odd_prompt_body, L=L, n_sub=n_sub),
        grid=(n_seq, n_inner), in_specs=in_specs,
        out_specs=[pl.BlockSpec((tt, d), rmap), st_spec],
        out_shape=[jax.ShapeDtypeStruct((rows, d), f32), jax.ShapeDtypeStruct((n_seq, C_HEADS, C_DK, C_DV), f32)],
        scratch_shapes=[p_shape, p_shape, y_shape, y_shape, pltpu.VMEM((th, d), bf16)],
        compiler_params=_params(("arbitrary", "arbitrary")),
        name="odd_prompt",
    )(x, x, gmix, win, cos, sin, *consts, wout)


def _odd_sample_body(*refs, L, n_seg):
    it = iter(refs)
    proj_refs = tuple(next(it) for _ in range(4))
    S_in = next(it)
    const_refs = tuple(next(it) for _ in range(N_ODD_CONSTS))
    y_ref, S_ref = next(it), next(it)
    _odd_chunk(0, 0, 0, L, n_seg, proj_refs, const_refs, y_ref, S_in, S_ref)


def _odd_sample(proj, S0, cos, sin, consts, *, n_seg):
    rows = proj[0].shape[0]
    L = SEQ_PAD * n_seg
    rmap = lambda o: (o, 0)
    cmap = lambda o: (0, 0)
    st_spec = pl.BlockSpec((n_seg, C_HEADS, C_DK, C_DV), lambda o: (o, 0, 0, 0))
    in_specs = [pl.BlockSpec((L, a.shape[1]), rmap) for a in proj] + [st_spec]
    for c in (cos, sin) + tuple(consts):
        in_specs.append(pl.BlockSpec(c.shape, (lambda o: (0, 0, 0)) if c.ndim == 3 else cmap))
    return pl.pallas_call(
        functools.partial(_odd_sample_body, L=L, n_seg=n_seg),
        grid=(rows // L,), in_specs=in_specs,
        out_specs=[pl.BlockSpec((L, C_WIDTH), rmap), st_spec],
        out_shape=[jax.ShapeDtypeStruct((rows, C_WIDTH), f32), jax.ShapeDtypeStruct(S0.shape, f32)],
        compiler_params=_params(("arbitrary",)),
        name="odd_sample",
    )(*proj, S0, cos, sin, *consts)


def _pad_lanes(a, width=LANES):
    return jnp.pad(a, [(0, 0)] * (a.ndim - 1) + [(0, width - a.shape[-1])])


def _even_w_in_cols(w):
    sizes = [A_HEADS * A_DK, A_HEADS * A_DK, A_WIDTH, A_HEADS, A_HEADS, A_WIDTH, B_WIDTH, B_CONV_DIM, B_HEADS]
    q, k, v, ig, fg, og, z, xbc, dt = jnp.split(w, np.cumsum(sizes)[:-1].tolist(), axis=-1)
    return jnp.concatenate([q, k, v, og, z, xbc, _pad_lanes(ig), _pad_lanes(fg), _pad_lanes(dt)], axis=-1).astype(bf16)


def _retention_tables(seg, n_pad, n_seg=1):
    log_gamma = np.log1p(-np.exp2(-5.0 - np.arange(C_HEADS, dtype=np.float64)))
    t_real = seg - n_pad
    idx = np.arange(seg, dtype=np.float64) - n_pad
    real = idx >= 0
    diff = idx[:, None] - idx[None, :]
    intra = np.where((diff >= 0) & real[None, :], np.exp(log_gamma[:, None, None] * np.maximum(diff, 0.0)), 0.0)
    intra = np.stack([np.kron(np.eye(n_seg), intra[h]) for h in range(C_HEADS)])
    cross = np.where(real[:, None], np.exp(log_gamma[None, :] * (idx[:, None] + 1.0)), 0.0)
    into = np.where(real[:, None], np.exp(log_gamma[None, :] * (t_real - 1.0 - idx[:, None])), 0.0)
    cdec = np.exp(log_gamma * t_real)[None, :]
    lanes = lambda a: np.pad(a, ((0, 0), (0, LANES - a.shape[1]))).astype(np.float32)
    return intra.astype(np.float32), lanes(np.tile(cross, (n_seg, 1))), lanes(np.tile(into, (n_seg, 1))), lanes(cdec)


def _rope_tables(pos):
    half = C_DK // 2
    inv = ROPE_BASE ** (-np.arange(half, dtype=np.float64) / half)
    ang = np.asarray(pos, dtype=np.float64)[:, None] * inv[None, :]
    cos, sin = np.cos(ang), np.sin(ang)
    return (np.concatenate([cos, cos], axis=-1).astype(np.float32),
            np.concatenate([-sin, sin], axis=-1).astype(np.float32))


def kernel(x_prompt, x_sample, state_mlstm_C, state_mlstm_n, state_mlstm_m, state_ssd_conv, state_ssd_h, state_ret_S, state_ffn_conv, norm_mix_g, norm_ffn_g, norm_final_g, even_w_in, mlstm_igate_b, mlstm_fgate_b, mlstm_norm_g, ssd_conv_w, ssd_conv_b, ssd_dt_bias, ssd_A_log, ssd_D, ssd_norm_g, even_w_out, odd_w_in, ret_norm_g, odd_w_out, ffn_w_up, ffn_conv_w, ffn_conv_b, ffn_w_down):
    bsz, seq, d = x_prompt.shape
    dbsz, dseq, _ = x_sample.shape
    n_pad = SEQ_PAD - dseq
    assert norm_mix_g.shape[0] == 2 and B_CONV - 1 <= n_pad < SEQ_PAD and seq % (4 * CHUNK) == 0

    row = lambda a: a.reshape(1, -1)
    gate_bias = jnp.concatenate([_pad_lanes(row(mlstm_igate_b[0])), _pad_lanes(row(mlstm_fgate_b[0])),
                                 _pad_lanes(row(ssd_dt_bias[0]))], axis=0)
    even_consts = (gate_bias, row(mlstm_norm_g[0]), ssd_conv_w[0], row(ssd_conv_b[0]), _pad_lanes(row(ssd_A_log[0])),
                   row(jnp.repeat(ssd_D[0], B_HEADDIM)), row(ssd_norm_g[0]))
    g_mix0, g_mix1 = row(norm_mix_g[0]), row(norm_mix_g[1])
    g_ffn0, g_ffn1, g_final = row(norm_ffn_g[0]), row(norm_ffn_g[1]), row(norm_final_g)
    w_even_in, w_even_out = _even_w_in_cols(even_w_in[0]), even_w_out[0].astype(bf16)
    w_odd_in, w_odd_out = odd_w_in[0].astype(bf16), odd_w_out[0].astype(bf16)
    w_up, w_dn = ffn_w_up.astype(bf16), ffn_w_down.astype(bf16)
    ffn_cb = ffn_conv_b[:, None, :]
    ret_norm = row(ret_norm_g[0])

    cos_p, sin_p = _rope_tables(np.arange(seq))
    xp = x_prompt.reshape(bsz * seq, d)
    xp, pC, pn, pm, pST, p_conv = _even_prompt(xp, g_mix0, w_even_in, even_consts, w_even_out,
                                               seq_rows=seq, L=CHUNK, n_sub=2)
    xp, p_f0 = _ffn(xp, g_ffn0, w_up[0], ffn_conv_w[0], ffn_cb[0], w_dn[0], None, seq_rows=seq, tt=512)
    xp, pSr = _odd_prompt(xp, g_mix1, w_odd_in, cos_p, sin_p, _retention_tables(CHUNK, 0) + (ret_norm,), w_odd_out,
                          seq_rows=seq, L=CHUNK, n_sub=2)
    _, p_f1, yp = _ffn(xp, g_ffn1, w_up[1], ffn_conv_w[1], ffn_cb[1], w_dn[1], g_final, seq_rows=seq, tt=512)
    pS = pST.reshape(bsz, B_STATE, B_HEADS, B_HEADDIM).transpose(0, 2, 3, 1)

    xs = jnp.pad(x_sample, ((0, 0), (n_pad, 0), (0, 0))).reshape(dbsz * SEQ_PAD, d)
    n_seg = 8
    cos_s, sin_s = _rope_tables(np.tile(PAST_LEN + np.arange(SEQ_PAD) - n_pad, n_seg))
    hist = lambda a: jnp.pad(a, ((0, 0), (n_pad - a.shape[1], dseq), (0, 0))).reshape(dbsz * SEQ_PAD, a.shape[-1])
    state = (hist(state_ssd_conv[0]), state_mlstm_C[0],
             jnp.pad(state_mlstm_n[0], ((0, 0), (0, SUBLANES - A_HEADS), (0, 0))),
             _pad_lanes(state_mlstm_m[0])[:, None, :],
             state_ssd_h[0].reshape(dbsz, B_HEADS // 2, 2 * B_HEADDIM, B_STATE))
    to_tm = lambda a: a.reshape(dbsz, SEQ_PAD, d)[:, n_pad:, :].transpose(1, 0, 2).reshape(dseq * dbsz, d)
    from_tm = lambda a: jnp.pad(a.reshape(dseq, dbsz, d).transpose(1, 0, 2),
                                ((0, 0), (n_pad, 0), (0, 0))).reshape(dbsz * SEQ_PAD, d)
    tm = 256
    proj = _norm_proj(xs, g_mix0, w_even_in, EVEN_SPLITS, tm)
    y, sC, sn, sm, sS, s_conv = _even_sample(proj, state, even_consts, n_seg=n_seg, n_pad=n_pad)
    sS = sS.reshape(dbsz, B_HEADS, B_HEADDIM, B_STATE)
    xs = _proj_res(xs, y, w_even_out, tm)
    xt, s_f0 = _ffn_sample(to_tm(xs), state_ffn_conv, 0, g_ffn0, w_up[0], ffn_conv_w[0], ffn_cb[0], w_dn[0], None,
                           cw=D_FF // 2)
    xs = from_tm(xt)
    proj = _norm_proj(xs, g_mix1, w_odd_in, ODD_SPLITS, tm)
    y, sSr = _odd_sample(proj, state_ret_S[0], cos_s, sin_s,
                         _retention_tables(SEQ_PAD, n_pad, n_seg) + (ret_norm,), n_seg=n_seg)
    xs = _proj_res(xs, y, w_odd_out, tm)
    _, s_f1, yt = _ffn_sample(to_tm(xs), state_ffn_conv, 1, g_ffn1, w_up[1], ffn_conv_w[1], ffn_cb[1], w_dn[1],
                              g_final, cw=D_FF // 2)

    def conv_rows(tail, bs):
        return tail.reshape(bs, SUBLANES, tail.shape[-1])[:, SUBLANES - (B_CONV - 1):, :]

    def mlstm_states(C, n, m):
        return C[None], n[None, :, :A_HEADS, :], m[None, :, 0, :A_HEADS]

    p_ff = jnp.stack([p_f0, p_f1]).reshape(2, bsz, SUBLANES, 2 * D_FF)[:, :, SUBLANES - (FFN_CONV - 1):, :]
    s_ff = jnp.stack([s_f0, s_f1])
    p_states = mlstm_states(pC, pn, pm) + (conv_rows(p_conv, bsz)[None], pS[None], pSr[None], p_ff)
    s_states = mlstm_states(sC, sn, sm) + (conv_rows(s_conv, dbsz)[None], sS[None], sSr[None], s_ff)
    y_prompt = yp.reshape(bsz, seq, d)
    y_sample = yt.reshape(dseq, dbsz, d).transpose(1, 0, 2)
    return (y_prompt, y_sample) + p_states + s_states
```

```python
import functools

import numpy as np
import jax
import jax.numpy as jnp
from jax import lax
from jax.experimental import pallas as pl
from jax.experimental.pallas import tpu as pltpu

f32 = jnp.float32
bf16 = jnp.bfloat16

EPS = 1e-6
CHUNK = 128
D_MODEL = 1024
A_HEADS, A_DK, A_DV = 4, 128, 256
A_WIDTH = A_HEADS * A_DV
B_HEADS, B_HEADDIM, B_GROUPS, B_STATE, B_CONV = 16, 64, 2, 128, 4
B_WIDTH = B_HEADS * B_HEADDIM
B_CONV_DIM = B_WIDTH + 2 * B_GROUPS * B_STATE
B_HEADS_PER_GROUP = B_HEADS // B_GROUPS
B_GROUP_WIDTH = B_WIDTH // B_GROUPS
C_HEADS, C_DK, C_DV = 8, 128, 256
C_WIDTH = C_HEADS * C_DV
ROPE_BASE = 10000.0
D_FF = 2816
FFN_CONV = 3
PAST_LEN = 16384

LANES = 128
SUBLANES = 8
SEQ_PAD = SUBLANES
NEG_BIG = -1e30
VMEM_LIMIT = 56 * 1024 * 1024

EVEN_SPLITS = (A_HEADS * A_DK, A_HEADS * A_DK, A_WIDTH, A_WIDTH, B_WIDTH, B_CONV_DIM, 3 * LANES)
ODD_SPLITS = (C_HEADS * C_DK, C_HEADS * C_DK, C_WIDTH, C_WIDTH)


def _params(sem):
    return pltpu.CompilerParams(dimension_semantics=sem, vmem_limit_bytes=VMEM_LIMIT)


def _resident(shape, index_map):
    return pl.BlockSpec(shape, index_map, pipeline_mode=pl.Buffered(1))


def _nt(a, b):
    return lax.dot_general(a, b, (((1,), (1,)), ((), ())), preferred_element_type=f32)


def _tn(a, b):
    return lax.dot_general(a, b, (((0,), (0,)), ((), ())), preferred_element_type=f32)


def _nn(a, b):
    return jnp.dot(a, b, preferred_element_type=f32)


def _rmsnorm(x, g):
    return x * lax.rsqrt(jnp.mean(x * x, axis=-1, keepdims=True) + EPS) * g


def _layernorm_nogain(h):
    mu = jnp.mean(h, axis=-1, keepdims=True)
    hc = h - mu
    return hc * lax.rsqrt(jnp.mean(hc * hc, axis=-1, keepdims=True) + EPS)


def _softplus(x):
    return jnp.maximum(x, 0.0) + jnp.log1p(jnp.exp(-jnp.abs(x)))


def _silu(x):
    return x * jax.nn.sigmoid(x)


def _split3(x):
    hi = x.astype(bf16)
    r1 = x - hi.astype(f32)
    mid = r1.astype(bf16)
    lo = (r1 - mid.astype(f32)).astype(bf16)
    return hi, mid, lo


def _cumsum_rows(tril, x):
    hi, mid, lo = _split3(x)
    return _nn(tril, hi) + _nn(tril, mid) + _nn(tril, lo)


def _split_views(ref, splits):
    views, off = [], 0
    for n in splits:
        views.append(ref.at[:, off:off + n])
        off += n
    return views


def _project(h, w_ref, p_ref, splits):
    off = 0
    for n in splits:
        p_ref[:, off:off + n] = _nn(h, w_ref[:, off:off + n])
        off += n


MXU_COLS = 256


def _proj_pieces(h_ref, w_ref, dst_ref):
    def piece(c0, n):
        def run():
            dst_ref[:, c0:c0 + n] = _nn(h_ref[...], w_ref[:, c0:c0 + n])
        return run
    total = w_ref.shape[1]
    return [piece(c0, min(MXU_COLS, total - c0)) for c0 in range(0, total, MXU_COLS)]


def _out_pieces(x_ref, y_ref, w_ref, o_ref, rows):
    def piece(c0, n):
        def run():
            o_ref[rows, c0:c0 + n] = x_ref[rows, c0:c0 + n] + _nn(y_ref[...], w_ref[:, c0:c0 + n])
        return run
    total = w_ref.shape[1]
    return [piece(c0, min(MXU_COLS, total - c0)) for c0 in range(0, total, MXU_COLS)]


def _spread(work, n_slots):
    return [work[i * len(work) // n_slots:(i + 1) * len(work) // n_slots] for i in range(n_slots)]


def _norm_proj_body(x_ref, g_ref, w_ref, *o_refs, splits):
    h = _rmsnorm(x_ref[...], g_ref[...]).astype(bf16)
    off = 0
    for o_ref, n in zip(o_refs, splits):
        o_ref[...] = _nn(h, w_ref[:, off:off + n])
        off += n


def _norm_proj(x, g, w, splits, tm):
    rows, d = x.shape
    return pl.pallas_call(
        functools.partial(_norm_proj_body, splits=splits),
        grid=(rows // tm,),
        in_specs=[pl.BlockSpec((tm, d), lambda i: (i, 0)),
                  pl.BlockSpec((1, d), lambda i: (0, 0)),
                  _resident(w.shape, lambda i: (0, 0))],
        out_specs=[pl.BlockSpec((tm, n), lambda i: (i, 0)) for n in splits],
        out_shape=[jax.ShapeDtypeStruct((rows, n), f32) for n in splits],
        compiler_params=_params(("arbitrary",)),
        name="norm_proj",
    )(x, g, w)


def _proj_res_body(x_ref, y_ref, w_ref, o_ref):
    o_ref[...] = x_ref[...] + _nn(y_ref[...].astype(bf16), w_ref[...])


def _proj_res(x, y, w, tm):
    rows, d = x.shape
    k = y.shape[1]
    return pl.pallas_call(
        _proj_res_body,
        grid=(rows // tm,),
        in_specs=[pl.BlockSpec((tm, d), lambda i: (i, 0)),
                  pl.BlockSpec((tm, k), lambda i: (i, 0)),
                  _resident(w.shape, lambda i: (0, 0))],
        out_specs=pl.BlockSpec((tm, d), lambda i: (i, 0)),
        out_shape=jax.ShapeDtypeStruct((rows, d), f32),
        compiler_params=_params(("arbitrary",)),
        name="proj_res",
    )(x, y, w)


def _conv3(bias, prev2, prev1, cur, w_ref):
    y = bias + prev2 * w_ref[0:1, :]
    y = y + prev1 * w_ref[1:2, :]
    return y + cur * w_ref[2:3, :]


def _ffn_body(*refs, tt, final):
    it = iter(refs)
    x_ref, g_ref, wup_ref, cw_ref, cb_ref, wdn_ref = (next(it) for _ in range(6))
    gf_ref = next(it) if final else None
    o_ref, tail_ref = next(it), next(it)
    yn_ref = next(it) if final else None
    scr, carry = next(it), next(it)

    @pl.when(pl.program_id(1) == 0)
    def _():
        carry[...] = jnp.zeros(carry.shape, f32)

    x = x_ref[...]
    h = _rmsnorm(x, g_ref[...]).astype(bf16)
    conv = []
    for part in range(2):
        cols = slice(part * D_FF, (part + 1) * D_FF)
        u = _nn(h, wup_ref[:, cols])
        scr[0:SUBLANES, :] = carry[:, cols]
        tail_ref[:, cols] = u[tt - SUBLANES:tt, :]
        carry[:, cols] = u[tt - SUBLANES:tt, :]
        scr[SUBLANES:SUBLANES + tt, :] = u
        conv.append(_conv3(cb_ref[:, cols], scr[SUBLANES - 2:SUBLANES - 2 + tt, :],
                           scr[SUBLANES - 1:SUBLANES - 1 + tt, :], u, cw_ref.at[:, cols]))
    act = (_silu(conv[0]) * conv[1]).astype(bf16)
    out = x + _nn(act, wdn_ref[...])
    o_ref[...] = out
    if final:
        yn_ref[...] = _rmsnorm(out, gf_ref[...])


def _ffn(x, g, wup, conv_w, conv_b, wdn, g_final, *, seq_rows, tt):
    rows, d = x.shape
    final = g_final is not None
    n_seq = rows // seq_rows
    n_inner = seq_rows // tt
    rmap = lambda s, t: (s * n_inner + t, 0)
    cmap = lambda s, t: (0, 0)
    in_specs = [pl.BlockSpec((tt, d), rmap), pl.BlockSpec((1, d), cmap), _resident(wup.shape, cmap),
                pl.BlockSpec(conv_w.shape, cmap), pl.BlockSpec(conv_b.shape, cmap), _resident(wdn.shape, cmap)]
    args = [x, g, wup, conv_w, conv_b, wdn]
    out_specs = [pl.BlockSpec((tt, d), rmap), pl.BlockSpec((SUBLANES, 2 * D_FF), lambda s, t: (s, 0))]
    out_shape = [jax.ShapeDtypeStruct((rows, d), f32), jax.ShapeDtypeStruct((n_seq * SUBLANES, 2 * D_FF), f32)]
    if final:
        in_specs.append(pl.BlockSpec((1, d), cmap))
        args.append(g_final)
        out_specs.append(pl.BlockSpec((tt, d), rmap))
        out_shape.append(jax.ShapeDtypeStruct((rows, d), f32))
    return pl.pallas_call(
        functools.partial(_ffn_body, tt=tt, final=final),
        grid=(n_seq, n_inner), in_specs=in_specs, out_specs=out_specs, out_shape=out_shape,
        scratch_shapes=[pltpu.VMEM((SUBLANES + tt, D_FF), f32), pltpu.VMEM((SUBLANES, 2 * D_FF), f32)],
        compiler_params=_params(("arbitrary", "arbitrary")),
        name="ffn_prompt",
    )(*args)


def _ffn_sample_body(*refs, nb, nj, final):
    it = iter(refs)
    x_ref, hist_ref, g_ref, wup_ref, cw_ref, cb_ref, wdn_ref = (next(it) for _ in range(7))
    gf_ref = next(it) if final else None
    o_ref, tail_ref = next(it), next(it)
    yn_ref = next(it) if final else None
    h_scr, convg_scr, acc_scr = next(it), next(it), next(it)
    c = pl.program_id(0)
    tt = x_ref.shape[0]

    @pl.when(c == 0)
    def _():
        h_scr[...] = _rmsnorm(x_ref[...], g_ref[...]).astype(bf16)
        acc_scr[...] = jnp.zeros(acc_scr.shape, f32)

    u = _nn(h_scr[...], wup_ref[...])
    ext = jnp.concatenate([hist_ref[:, k, :] for k in range(FFN_CONV - 1)] + [u], axis=0)
    for k in range(FFN_CONV - 1):
        tail_ref[:, k, :] = ext[tt + k * nb:tt + (k + 1) * nb, :]
    y = _conv3(cb_ref[...], ext[0:tt, :], ext[nb:nb + tt, :], u, cw_ref)

    @pl.when(c < nj)
    def _():
        convg_scr[c] = y

    @pl.when(c >= nj)
    def _():
        act = (_silu(convg_scr[c - nj]) * y).astype(bf16)
        acc_scr[...] += _nn(act, wdn_ref[...])

    @pl.when(c == 2 * nj - 1)
    def _():
        out = x_ref[...] + acc_scr[...]
        o_ref[...] = out
        if final:
            yn_ref[...] = _rmsnorm(out, gf_ref[...])


def _ffn_sample(x, hist, layer, g, wup, conv_w, conv_b, wdn, g_final, *, cw):
    rows, d = x.shape
    nb = hist.shape[1]
    nj = D_FF // cw
    final = g_final is not None
    fixed = lambda c: (0, 0)
    colblk = lambda c: (0, c)
    in_specs = [pl.BlockSpec((rows, d), fixed),
                pl.BlockSpec((None, nb, FFN_CONV - 1, cw), lambda c: (layer, 0, 0, c)),
                pl.BlockSpec((1, d), fixed), pl.BlockSpec((d, cw), colblk), pl.BlockSpec((FFN_CONV, cw), colblk),
                pl.BlockSpec((1, cw), colblk), pl.BlockSpec((cw, d), lambda c: (jnp.maximum(c - nj, 0), 0))]
    args = [x, hist, g, wup, conv_w, conv_b, wdn]
    out_specs = [pl.BlockSpec((rows, d), fixed), pl.BlockSpec((nb, FFN_CONV - 1, cw), lambda c: (0, 0, c))]
    out_shape = [jax.ShapeDtypeStruct((rows, d), f32), jax.ShapeDtypeStruct((nb, FFN_CONV - 1, 2 * D_FF), f32)]
    if final:
        in_specs.append(pl.BlockSpec((1, d), fixed))
        args.append(g_final)
        out_specs.append(pl.BlockSpec((rows, d), fixed))
        out_shape.append(jax.ShapeDtypeStruct((rows, d), f32))
    return pl.pallas_call(
        functools.partial(_ffn_sample_body, nb=nb, nj=nj, final=final),
        grid=(2 * nj,), in_specs=in_specs, out_specs=out_specs, out_shape=out_shape,
        scratch_shapes=[pltpu.VMEM((rows, d), bf16), pltpu.VMEM((nj, rows, cw), f32), pltpu.VMEM((rows, d), f32)],
        compiler_params=_params(("arbitrary",)),
        name="ffn_sample",
    )(*args)


EVEN_SLOTS = A_HEADS + B_HEADS // 2


def _even_chunk(r0, slot, L, n_pad, n_seg, proj_refs, hist_ref, const_refs, y_ref, state_in_refs, state_refs,
                tail_ref, conv_scr, yb_scr, side=None):
    q_ref, k_ref, v_ref, og_ref, z_ref, xbc_ref, gt_ref = proj_refs
    gb_ref, anorm_ref, convw_ref, convb_ref, alog_ref, dx_ref, bnorm_ref = const_refs
    C_in, n_in, m_in, S_in = state_in_refs
    C_ref, n_ref, m_ref, S_ref = state_refs
    rows = pl.ds(r0, L)
    seg = L // n_seg
    seg_rows = [slice(i * seg, (i + 1) * seg) for i in range(n_seg)]
    seg_last = [slice((i + 1) * seg - 1, (i + 1) * seg) for i in range(n_seg)]

    def per_row(vals):
        if n_seg == 1:
            return vals[0]
        return jnp.concatenate([jnp.broadcast_to(v, (seg, v.shape[1])) for v in vals], axis=0)

    ri = lax.broadcasted_iota(jnp.int32, (L, L), 0)
    ci = lax.broadcasted_iota(jnp.int32, (L, L), 1)
    causal = ri >= ci
    if n_seg > 1:
        causal = causal & ((ri // seg) == (ci // seg))
    tril = causal.astype(bf16)
    valid = None
    if n_pad:
        valid = (lax.broadcasted_iota(jnp.int32, (L, 1), 0) % seg) >= n_pad

    gates = gt_ref[rows, :]
    li = gates[:, 0:LANES] + gb_ref[0:1, :]
    fpre = gates[:, LANES:2 * LANES] + gb_ref[1:2, :]
    lf = -_softplus(-fpre)
    dt = _softplus(gates[:, 2 * LANES:3 * LANES] + gb_ref[2:3, :])
    if n_pad:
        li = jnp.where(valid, li, NEG_BIG)
        lf = jnp.where(valid, lf, 0.0)
        dt = jnp.where(valid, dt, 0.0)
    a = dt * (-jnp.exp(alog_ref[...]))
    cums = _cumsum_rows(tril, jnp.concatenate([lf, a], axis=1))
    bcum = cums[:, 0:LANES]
    acum = cums[:, LANES:2 * LANES]
    liT, bT, aT, dtT = li.T, bcum.T, acum.T, dt.T

    m_old = [m_in[slot + i] for i in range(n_seg)]
    b_lasts = [bcum[r, :] for r in seg_last]
    m_rows, b_last = per_row(m_old), per_row(b_lasts)
    log_g = b_last - bcum + li
    m_news = [jnp.maximum(b_lasts[i] + m_old[i], jnp.max(log_g[seg_rows[i], :], axis=0, keepdims=True))
              for i in range(n_seg)]
    m_new = per_row(m_news)
    gfac = jnp.exp(log_g - m_new)
    cdecay = jnp.exp(b_last + m_rows - m_new)
    for i in range(n_seg):
        m_ref[slot + i] = m_news[i]
    for h in range(A_HEADS):
        for run in (side[h] if side else ()):
            run()
        bcol, brow, lirow = bcum[:, h:h + 1], bT[h:h + 1, :], liT[h:h + 1, :]
        m_h = m_rows[:, h:h + 1]
        logw = jnp.where(causal, bcol - brow + lirow, -jnp.inf)
        log_prev = bcol + m_h
        m_t = jnp.maximum(log_prev, jnp.max(logw, axis=-1, keepdims=True))
        w_in = jnp.exp(logw - m_t)
        w_prev = jnp.exp(log_prev - m_t)
        qh = q_ref[rows, h * A_DK:(h + 1) * A_DK]
        kh = k_ref[rows, h * A_DK:(h + 1) * A_DK] * (A_DK ** -0.5)
        vh = v_ref[rows, h * A_DV:(h + 1) * A_DV]
        vb = vh.astype(bf16)
        qb = qh.astype(bf16)
        s = _nt(qb, kh.astype(bf16)) * w_in
        kg = kh * gfac[:, h:h + 1]
        Cs = [C_in[slot + i, h] for i in range(n_seg)]
        ns = [n_in[slot + i, h:h + 1, :] for i in range(n_seg)]
        if n_seg == 1:
            inter = _nn(qb, Cs[0].astype(bf16))
            dec_h = cdecay[:, h:h + 1]
            C_ref[slot, h] = Cs[0] * dec_h + _tn(kg.astype(bf16), vb)
            n_ref[slot, h:h + 1, :] = ns[0] * dec_h + jnp.sum(kg, axis=0, keepdims=True)
        else:
            inter = jnp.concatenate([_nn(qh[seg_rows[i], :].astype(bf16), Cs[i].astype(bf16))
                                     for i in range(n_seg)], axis=0)
            for i in range(n_seg):
                rs = seg_rows[i]
                dec_h = cdecay[seg_last[i], h:h + 1]
                C_ref[slot + i, h] = Cs[i] * dec_h + _tn(kg[rs, :].astype(bf16), vh[rs, :].astype(bf16))
                n_ref[slot + i, h:h + 1, :] = ns[i] * dec_h + jnp.sum(kg[rs, :], axis=0, keepdims=True)
        num = _nn(s.astype(bf16), vb) + inter * w_prev
        den = jnp.sum(s, axis=-1, keepdims=True) + jnp.sum(qh * per_row(ns), axis=-1, keepdims=True) * w_prev
        hout = num / jnp.maximum(jnp.abs(den), jnp.exp(-m_t))
        ya = _layernorm_nogain(hout) * anorm_ref[:, h * A_DV:(h + 1) * A_DV]
        ya = ya * jax.nn.sigmoid(og_ref[rows, h * A_DV:(h + 1) * A_DV])
        y_ref[rows, h * A_DV:(h + 1) * A_DV] = ya.astype(y_ref.dtype)

    xraw = xbc_ref[rows, :]
    if hist_ref is not None:
        seg_row = lax.broadcasted_iota(jnp.int32, (L, 1), 0) % seg
        is_hist = (seg_row >= n_pad - (B_CONV - 1)) & jnp.logical_not(valid)
        xraw = jnp.where(is_hist, hist_ref[rows, :], xraw)
    conv_scr[SUBLANES:SUBLANES + L, :] = xraw
    xc = convb_ref[...] + conv_scr[SUBLANES - 3:SUBLANES - 3 + L, :] * convw_ref[0:1, :]
    xc = xc + conv_scr[SUBLANES - 2:SUBLANES - 2 + L, :] * convw_ref[1:2, :]
    xc = xc + conv_scr[SUBLANES - 1:SUBLANES - 1 + L, :] * convw_ref[2:3, :]
    xc = xc + xraw * convw_ref[3:4, :]
    if n_seg == 1:
        new_tail = conv_scr[L:L + SUBLANES, :]
        tail_ref[...] = new_tail
        conv_scr[0:SUBLANES, :] = new_tail
    else:
        tail_ref[...] = xraw
    xc = _silu(xc)

    a_last = per_row([acum[r, :] for r in seg_last])
    wtile = jnp.exp(a_last - acum) * dt
    expa = jnp.exp(acum)

    def head_matrix(cb, h):
        acol, arow, dtrow = acum[:, h:h + 1], aT[h:h + 1, :], dtT[h:h + 1, :]
        decay = jnp.where(causal, jnp.exp(jnp.where(causal, acol - arow, 0.0)), 0.0)
        return (cb * decay * dtrow).astype(bf16)

    def group_bc(g):
        Bg = xc[:, B_WIDTH + g * B_STATE:B_WIDTH + (g + 1) * B_STATE].astype(bf16)
        c0 = B_WIDTH + B_GROUPS * B_STATE + g * B_STATE
        Cg = xc[:, c0:c0 + B_STATE].astype(bf16)
        return Bg, Cg, _nt(Cg, Bg)

    low_half = lax.broadcasted_iota(jnp.int32, (L, LANES), 1) < B_HEADDIM
    first_head_rows = lax.broadcasted_iota(jnp.int32, (2 * B_HEADDIM, 1), 0) < B_HEADDIM

    def pair_lanes(t, j):
        return jnp.where(low_half, jnp.broadcast_to(t[:, 2 * j:2 * j + 1], (L, LANES)),
                         jnp.broadcast_to(t[:, 2 * j + 1:2 * j + 2], (L, LANES)))

    pairs_per_group = B_HEADS_PER_GROUP // 2
    for g in range(B_GROUPS):
        Bg, Cg, cb = group_bc(g)
        gs = slice(g * B_GROUP_WIDTH, (g + 1) * B_GROUP_WIDTH)
        if n_seg == 1:
            ST = S_in[slot, :, gs]
            inter_g = _nn(Cg, ST.astype(bf16))
        else:
            Bf = xc[:, B_WIDTH + g * B_STATE:B_WIDTH + (g + 1) * B_STATE]
            c0 = B_WIDTH + B_GROUPS * B_STATE + g * B_STATE
            Cf = xc[:, c0:c0 + B_STATE]
        xw, decs = [], []
        for jp in range(pairs_per_group):
            j = g * pairs_per_group + jp
            for run in (side[A_HEADS + j] if side else ()):
                run()
            ps = slice(j * LANES, (j + 1) * LANES)
            e_pair, w_pair = pair_lanes(expa, j), pair_lanes(wtile, j)
            mcat = jnp.concatenate([head_matrix(cb, 2 * j), head_matrix(cb, 2 * j + 1)], axis=1)
            xp = xc[:, ps]
            xbd = jnp.concatenate([jnp.where(low_half, xp, 0.0).astype(bf16),
                                   jnp.where(low_half, 0.0, xp).astype(bf16)], axis=0)
            if n_seg == 1:
                inter = inter_g[:, jp * LANES:(jp + 1) * LANES]
                xw.append((xp * w_pair).astype(bf16))
                decs.append(e_pair[L - 1:L, :])
            else:
                xwf = xp * w_pair
                parts = []
                for i in range(n_seg):
                    rs = seg_rows[i]
                    Sp = S_in[slot + i, j]
                    parts.append(_nt(Cf[rs, :].astype(bf16), Sp.astype(bf16)))
                    e_last = e_pair[seg_last[i], :]
                    dec_col = jnp.where(first_head_rows, e_last[:, 0:1], e_last[:, B_HEADDIM:B_HEADDIM + 1])
                    S_ref[slot + i, j] = Sp * dec_col + _tn(xwf[rs, :].astype(bf16), Bf[rs, :].astype(bf16))
                inter = jnp.concatenate(parts, axis=0)
            yb_scr[:, ps] = (_nn(mcat, xbd) + inter * e_pair) + dx_ref[:, ps] * xp
        if n_seg == 1:
            S_ref[slot, :, gs] = ST * jnp.concatenate(decs, axis=1) + _tn(Bg, jnp.concatenate(xw, axis=1))
    yb = yb_scr[...] * _silu(z_ref[rows, :])
    for g in range(B_GROUPS):
        gs = slice(g * B_GROUP_WIDTH, (g + 1) * B_GROUP_WIDTH)
        yg = yb[:, gs]
        yg = yg * lax.rsqrt(jnp.mean(yg * yg, axis=-1, keepdims=True) + EPS)
        y_ref[rows, A_WIDTH + g * B_GROUP_WIDTH:A_WIDTH + (g + 1) * B_GROUP_WIDTH] = (
            yg * bnorm_ref[:, gs]).astype(y_ref.dtype)


N_EVEN_CONSTS = 7


def _even_prompt_body(*refs, L, n_sub):
    it = iter(refs)
    x_ref, xnext_ref, gmix_ref, win_ref = next(it), next(it), next(it), next(it)
    const_refs = tuple(next(it) for _ in range(N_EVEN_CONSTS))
    wout_ref = next(it)
    o_ref = next(it)
    state_refs = tuple(next(it) for _ in range(4))
    tail_ref = next(it)
    p_scr, y_scr = (next(it), next(it)), (next(it), next(it))
    h_scr, conv_scr, yb_scr = (next(it) for _ in range(3))
    th = L * n_sub
    gmix = gmix_ref[...]

    @pl.when((pl.program_id(0) == 0) & (pl.program_id(1) == 0))
    def _():
        _project(_rmsnorm(x_ref[0:th, :], gmix).astype(bf16), win_ref, p_scr[0], EVEN_SPLITS)

    @pl.when(pl.program_id(1) == 0)
    def _():
        conv_scr[0:SUBLANES, :] = jnp.zeros((SUBLANES, B_CONV_DIM), f32)
        for r in state_refs:
            r[...] = jnp.zeros(r.shape, f32)

    for half in range(2):
        ahead = xnext_ref[...] if half else x_ref[th:2 * th, :]
        h_scr[...] = _rmsnorm(ahead, gmix).astype(bf16)
        work = _proj_pieces(h_scr, win_ref, p_scr[1 - half])
        if half:
            work = work + _out_pieces(x_ref, y_scr[0], wout_ref, o_ref, slice(0, th))
        side = _spread(work, n_sub * EVEN_SLOTS)
        proj_refs = _split_views(p_scr[half], EVEN_SPLITS)
        for c in range(n_sub):
            _even_chunk(c * L, 0, L, 0, 1, proj_refs, None, const_refs, y_scr[half], state_refs, state_refs,
                        tail_ref, conv_scr, yb_scr, side[c * EVEN_SLOTS:(c + 1) * EVEN_SLOTS])
    for run in _out_pieces(x_ref, y_scr[1], wout_ref, o_ref, slice(th, 2 * th)):
        run()


def _even_prompt(x, gmix, win, consts, wout, *, seq_rows, L, n_sub):
    rows, d = x.shape
    n_seq = rows // seq_rows
    th = L * n_sub
    tt = 2 * th
    n_inner = seq_rows // tt
    n_half = rows // th
    rmap = lambda s, t: (s * n_inner + t, 0)
    nextmap = lambda s, t: (jnp.minimum(2 * (s * n_inner + t) + 2, n_half - 1), 0)
    cmap = lambda s, t: (0, 0)
    smap4 = lambda s, t: (s, 0, 0, 0)
    smap3 = lambda s, t: (s, 0, 0)
    st_specs = [pl.BlockSpec((1, A_HEADS, A_DK, A_DV), smap4), pl.BlockSpec((1, SUBLANES, A_DK), smap3),
                pl.BlockSpec((1, 1, LANES), smap3), pl.BlockSpec((1, B_STATE, B_WIDTH), smap3)]
    st_shapes = [jax.ShapeDtypeStruct((n_seq, A_HEADS, A_DK, A_DV), f32),
                 jax.ShapeDtypeStruct((n_seq, SUBLANES, A_DK), f32),
                 jax.ShapeDtypeStruct((n_seq, 1, LANES), f32),
                 jax.ShapeDtypeStruct((n_seq, B_STATE, B_WIDTH), f32)]
    in_specs = ([pl.BlockSpec((tt, d), rmap), pl.BlockSpec((th, d), nextmap), pl.BlockSpec((1, d), cmap),
                 _resident(win.shape, cmap)]
                + [pl.BlockSpec(c.shape, cmap) for c in consts] + [_resident(wout.shape, cmap)])
    p_shape = pltpu.VMEM((th, sum(EVEN_SPLITS)), f32)
    y_shape = pltpu.VMEM((th, A_WIDTH + B_WIDTH), bf16)
    return pl.pallas_call(
        functools.partial(_even_prompt_body, L=L, n_sub=n_sub),
        grid=(n_seq, n_inner), in_specs=in_specs,
        out_specs=[pl.BlockSpec((tt, d), rmap)] + st_specs + [pl.BlockSpec((SUBLANES, B_CONV_DIM), lambda s, t: (s, 0))],
        out_shape=[jax.ShapeDtypeStruct((rows, d), f32)] + st_shapes
        + [jax.ShapeDtypeStruct((n_seq * SUBLANES, B_CONV_DIM), f32)],
        scratch_shapes=[p_shape, p_shape, y_shape, y_shape, pltpu.VMEM((th, d), bf16),
                        pltpu.VMEM((SUBLANES + L, B_CONV_DIM), f32), pltpu.VMEM((L, B_WIDTH), f32)],
        compiler_params=_params(("arbitrary", "arbitrary")),
        name="even_prompt",
    )(x, x, gmix, win, *consts, wout)


def _even_sample_body(*refs, L, n_seg, n_pad):
    it = iter(refs)
    proj_refs = tuple(next(it) for _ in range(7))
    hist_ref = next(it)
    state_in_refs = tuple(next(it) for _ in range(4))
    const_refs = tuple(next(it) for _ in range(N_EVEN_CONSTS))
    y_ref = next(it)
    state_refs = tuple(next(it) for _ in range(4))
    tail_ref = next(it)
    conv_scr, yb_scr = next(it), next(it)

    conv_scr[0:SUBLANES, :] = jnp.zeros((SUBLANES, B_CONV_DIM), f32)
    n_ref = state_refs[1]
    n_ref[:, A_HEADS:, :] = jnp.zeros((n_seg, SUBLANES - A_HEADS, A_DK), f32)
    _even_chunk(0, 0, L, n_pad, n_seg, proj_refs, hist_ref, const_refs, y_ref, state_in_refs, state_refs,
                tail_ref, conv_scr, yb_scr)


def _even_sample(proj, state, consts, *, n_seg, n_pad):
    rows = proj[0].shape[0]
    L = SEQ_PAD * n_seg
    rmap = lambda o: (o, 0)
    cmap = lambda o: (0, 0)
    smap4 = lambda o: (o, 0, 0, 0)
    smap3 = lambda o: (o, 0, 0)
    st_specs = [pl.BlockSpec((n_seg, A_HEADS, A_DK, A_DV), smap4), pl.BlockSpec((n_seg, SUBLANES, A_DK), smap3),
                pl.BlockSpec((n_seg, 1, LANES), smap3),
                pl.BlockSpec((n_seg, B_HEADS // 2, 2 * B_HEADDIM, B_STATE), smap4)]
    st_shapes = [jax.ShapeDtypeStruct(s.shape, f32) for s in state[1:]]
    in_specs = ([pl.BlockSpec((L, a.shape[1]), rmap) for a in proj] + [pl.BlockSpec((L, B_CONV_DIM), rmap)]
                + st_specs + [pl.BlockSpec(c.shape, cmap) for c in consts])
    return pl.pallas_call(
        functools.partial(_even_sample_body, L=L, n_seg=n_seg, n_pad=n_pad),
        grid=(rows // L,), in_specs=in_specs,
        out_specs=[pl.BlockSpec((L, A_WIDTH + B_WIDTH), rmap)] + st_specs + [pl.BlockSpec((L, B_CONV_DIM), rmap)],
        out_shape=[jax.ShapeDtypeStruct((rows, A_WIDTH + B_WIDTH), f32)] + st_shapes
        + [jax.ShapeDtypeStruct((rows, B_CONV_DIM), f32)],
        scratch_shapes=[pltpu.VMEM((SUBLANES + L, B_CONV_DIM), f32), pltpu.VMEM((L, B_WIDTH), f32)],
        compiler_params=_params(("arbitrary",)),
        name="even_sample",
    )(*proj, *state, *consts)


def _odd_chunk(r0, rope_r0, slot, L, n_seg, proj_refs, const_refs, y_ref, S_in, S_ref, side=None):
    q_ref, k_ref, v_ref, g_ref = proj_refs
    cos_ref, sin_ref, intra_ref, cross_ref, into_ref, cdec_ref, norm_ref = const_refs
    rows = pl.ds(r0, L)
    seg = L // n_seg
    cosf = cos_ref[pl.ds(rope_r0, L), :]
    sinf = sin_ref[pl.ds(rope_r0, L), :]
    cross = cross_ref[...]
    into = into_ref[...]
    cdec = cdec_ref[...]
    for h in range(C_HEADS):
        for run in (side[h] if side else ()):
            run()
        qh = q_ref[rows, h * C_DK:(h + 1) * C_DK]
        kh = k_ref[rows, h * C_DK:(h + 1) * C_DK]
        qh = qh * cosf + pltpu.roll(qh, C_DK // 2, 1) * sinf
        kh = (kh * cosf + pltpu.roll(kh, C_DK // 2, 1) * sinf) * (C_DK ** -0.5)
        vh = v_ref[rows, h * C_DV:(h + 1) * C_DV]
        vb = vh.astype(bf16)
        qb = qh.astype(bf16)
        s = _nt(qb, kh.astype(bf16)) * intra_ref[h]
        kd = kh * into[:, h:h + 1]
        if n_seg == 1:
            S = S_in[slot, h]
            inter = _nn(qb, S.astype(bf16))
            S_ref[slot, h] = S * cdec[:, h:h + 1] + _tn(kd.astype(bf16), vb)
        else:
            parts = []
            for i in range(n_seg):
                rs = slice(i * seg, (i + 1) * seg)
                S = S_in[slot + i, h]
                parts.append(_nn(qh[rs, :].astype(bf16), S.astype(bf16)))
                S_ref[slot + i, h] = S * cdec[:, h:h + 1] + _tn(kd[rs, :].astype(bf16), vh[rs, :].astype(bf16))
            inter = jnp.concatenate(parts, axis=0)
        o = _nn(s.astype(bf16), vb) + inter * cross[:, h:h + 1]
        y = _layernorm_nogain(o) * norm_ref[:, h * C_DV:(h + 1) * C_DV]
        y = y * _silu(g_ref[rows, h * C_DV:(h + 1) * C_DV])
        y_ref[rows, h * C_DV:(h + 1) * C_DV] = y.astype(y_ref.dtype)


N_ODD_CONSTS = 7


def _odd_prompt_body(*refs, L, n_sub):
    it = iter(refs)
    x_ref, xnext_ref, gmix_ref, win_ref = next(it), next(it), next(it), next(it)
    const_refs = tuple(next(it) for _ in range(N_ODD_CONSTS))
    wout_ref = next(it)
    o_ref, S_ref = next(it), next(it)
    p_scr, y_scr = (next(it), next(it)), (next(it), next(it))
    h_scr = next(it)
    th = L * n_sub
    gmix = gmix_ref[...]

    @pl.when((pl.program_id(0) == 0) & (pl.program_id(1) == 0))
    def _():
        _project(_rmsnorm(x_ref[0:th, :], gmix).astype(bf16), win_ref, p_scr[0], ODD_SPLITS)

    @pl.when(pl.program_id(1) == 0)
    def _():
        S_ref[...] = jnp.zeros(S_ref.shape, f32)

    for half in range(2):
        ahead = xnext_ref[...] if half else x_ref[th:2 * th, :]
        h_scr[...] = _rmsnorm(ahead, gmix).astype(bf16)
        work = _proj_pieces(h_scr, win_ref, p_scr[1 - half])
        if half:
            work = work + _out_pieces(x_ref, y_scr[0], wout_ref, o_ref, slice(0, th))
        side = _spread(work, n_sub * C_HEADS)
        proj_refs = _split_views(p_scr[half], ODD_SPLITS)
        for c in range(n_sub):
            _odd_chunk(c * L, half * th + c * L, 0, L, 1, proj_refs, const_refs, y_scr[half], S_ref, S_ref,
                       side[c * C_HEADS:(c + 1) * C_HEADS])
    for run in _out_pieces(x_ref, y_scr[1], wout_ref, o_ref, slice(th, 2 * th)):
        run()


def _odd_prompt(x, gmix, win, cos, sin, consts, wout, *, seq_rows, L, n_sub):
    rows, d = x.shape
    n_seq = rows // seq_rows
    th = L * n_sub
    tt = 2 * th
    n_inner = seq_rows // tt
    n_half = rows // th
    rmap = lambda s, t: (s * n_inner + t, 0)
    nextmap = lambda s, t: (jnp.minimum(2 * (s * n_inner + t) + 2, n_half - 1), 0)
    cmap = lambda s, t: (0, 0)
    st_spec = pl.BlockSpec((1, C_HEADS, C_DK, C_DV), lambda s, t: (s, 0, 0, 0))
    rope_spec = pl.BlockSpec((tt, LANES), lambda s, t: (t, 0))
    in_specs = [pl.BlockSpec((tt, d), rmap), pl.BlockSpec((th, d), nextmap), pl.BlockSpec((1, d), cmap),
                _resident(win.shape, cmap), rope_spec, rope_spec]
    for c in consts:
        in_specs.append(pl.BlockSpec(c.shape, (lambda s, t: (0, 0, 0)) if c.ndim == 3 else cmap))
    in_specs.append(_resident(wout.shape, cmap))
    p_shape = pltpu.VMEM((th, sum(ODD_SPLITS)), f32)
    y_shape = pltpu.VMEM((th, C_WIDTH), bf16)
    return pl.pallas_call(
        functools.partial(_odd_prompt_body, L=L, n_sub=n_sub),
        grid=(n_seq, n_inner), in_specs=in_specs,
        out_specs=[pl.BlockSpec((tt, d), rmap), st_spec],
        out_shape=[jax.ShapeDtypeStruct((rows, d), f32), jax.ShapeDtypeStruct((n_seq, C_HEADS, C_DK, C_DV), f32)],
        scratch_shapes=[p_shape, p_shape, y_shape, y_shape, pltpu.VMEM((th, d), bf16)],
        compiler_params=_params(("arbitrary", "arbitrary")),
        name="odd_prompt",
    )(x, x, gmix, win, cos, sin, *consts, wout)


def _odd_sample_body(*refs, L, n_seg):
    it = iter(refs)
    proj_refs = tuple(next(it) for _ in range(4))
    S_in = next(it)
    const_refs = tuple(next(it) for _ in range(N_ODD_CONSTS))
    y_ref, S_ref = next(it), next(it)
    _odd_chunk(0, 0, 0, L, n_seg, proj_refs, const_refs, y_ref, S_in, S_ref)


def _odd_sample(proj, S0, cos, sin, consts, *, n_seg):
    rows = proj[0].shape[0]
    L = SEQ_PAD * n_seg
    rmap = lambda o: (o, 0)
    cmap = lambda o: (0, 0)
    st_spec = pl.BlockSpec((n_seg, C_HEADS, C_DK, C_DV), lambda o: (o, 0, 0, 0))
    in_specs = [pl.BlockSpec((L, a.shape[1]), rmap) for a in proj] + [st_spec]
    for c in (cos, sin) + tuple(consts):
        in_specs.append(pl.BlockSpec(c.shape, (lambda o: (0, 0, 0)) if c.ndim == 3 else cmap))
    return pl.pallas_call(
        functools.partial(_odd_sample_body, L=L, n_seg=n_seg),
        grid=(rows // L,), in_specs=in_specs,
        out_specs=[pl.BlockSpec((L, C_WIDTH), rmap), st_spec],
        out_shape=[jax.ShapeDtypeStruct((rows, C_WIDTH), f32), jax.ShapeDtypeStruct(S0.shape, f32)],
        compiler_params=_params(("arbitrary",)),
        name="odd_sample",
    )(*proj, S0, cos, sin, *consts)


def _pad_lanes(a, width=LANES):
    return jnp.pad(a, [(0, 0)] * (a.ndim - 1) + [(0, width - a.shape[-1])])


def _even_w_in_cols(w):
    sizes = [A_HEADS * A_DK, A_HEADS * A_DK, A_WIDTH, A_HEADS, A_HEADS, A_WIDTH, B_WIDTH, B_CONV_DIM, B_HEADS]
    q, k, v, ig, fg, og, z, xbc, dt = jnp.split(w, np.cumsum(sizes)[:-1].tolist(), axis=-1)
    return jnp.concatenate([q, k, v, og, z, xbc, _pad_lanes(ig), _pad_lanes(fg), _pad_lanes(dt)], axis=-1).astype(bf16)


def _retention_tables(seg, n_pad, n_seg=1):
    log_gamma = np.log1p(-np.exp2(-5.0 - np.arange(C_HEADS, dtype=np.float64)))
    t_real = seg - n_pad
    idx = np.arange(seg, dtype=np.float64) - n_pad
    real = idx >= 0
    diff = idx[:, None] - idx[None, :]
    intra = np.where((diff >= 0) & real[None, :], np.exp(log_gamma[:, None, None] * np.maximum(diff, 0.0)), 0.0)
    intra = np.stack([np.kron(np.eye(n_seg), intra[h]) for h in range(C_HEADS)])
    cross = np.where(real[:, None], np.exp(log_gamma[None, :] * (idx[:, None] + 1.0)), 0.0)
    into = np.where(real[:, None], np.exp(log_gamma[None, :] * (t_real - 1.0 - idx[:, None])), 0.0)
    cdec = np.exp(log_gamma * t_real)[None, :]
    lanes = lambda a: np.pad(a, ((0, 0), (0, LANES - a.shape[1]))).astype(np.float32)
    return intra.astype(np.float32), lanes(np.tile(cross, (n_seg, 1))), lanes(np.tile(into, (n_seg, 1))), lanes(cdec)


def _rope_tables(pos):
    half = C_DK // 2
    inv = ROPE_BASE ** (-np.arange(half, dtype=np.float64) / half)
    ang = np.asarray(pos, dtype=np.float64)[:, None] * inv[None, :]
    cos, sin = np.cos(ang), np.sin(ang)
    return (np.concatenate([cos, cos], axis=-1).astype(np.float32),
            np.concatenate([-sin, sin], axis=-1).astype(np.float32))


def kernel(x_prompt, x_sample, state_mlstm_C, state_mlstm_n, state_mlstm_m, state_ssd_conv, state_ssd_h, state_ret_S, state_ffn_conv, norm_mix_g, norm_ffn_g, norm_final_g, even_w_in, mlstm_igate_b, mlstm_fgate_b, mlstm_norm_g, ssd_conv_w, ssd_conv_b, ssd_dt_bias, ssd_A_log, ssd_D, ssd_norm_g, even_w_out, odd_w_in, ret_norm_g, odd_w_out, ffn_w_up, ffn_conv_w, ffn_conv_b, ffn_w_down):
    bsz, seq, d = x_prompt.shape
    dbsz, dseq, _ = x_sample.shape
    n_pad = SEQ_PAD - dseq
    assert norm_mix_g.shape[0] == 2 and B_CONV - 1 <= n_pad < SEQ_PAD and seq % (4 * CHUNK) == 0

    row = lambda a: a.reshape(1, -1)
    gate_bias = jnp.concatenate([_pad_lanes(row(mlstm_igate_b[0])), _pad_lanes(row(mlstm_fgate_b[0])),
                                 _pad_lanes(row(ssd_dt_bias[0]))], axis=0)
    even_consts = (gate_bias, row(mlstm_norm_g[0]), ssd_conv_w[0], row(ssd_conv_b[0]), _pad_lanes(row(ssd_A_log[0])),
                   row(jnp.repeat(ssd_D[0], B_HEADDIM)), row(ssd_norm_g[0]))
    g_mix0, g_mix1 = row(norm_mix_g[0]), row(norm_mix_g[1])
    g_ffn0, g_ffn1, g_final = row(norm_ffn_g[0]), row(norm_ffn_g[1]), row(norm_final_g)
    w_even_in, w_even_out = _even_w_in_cols(even_w_in[0]), even_w_out[0].astype(bf16)
    w_odd_in, w_odd_out = odd_w_in[0].astype(bf16), odd_w_out[0].astype(bf16)
    w_up = [ffn_w_up[layer].astype(bf16) for layer in range(2)]
    w_dn = [ffn_w_down[layer].astype(bf16) for layer in range(2)]
    ffn_cb = ffn_conv_b[:, None, :]
    ret_norm = row(ret_norm_g[0])

    cos_p, sin_p = _rope_tables(np.arange(seq))
    xp = x_prompt.reshape(bsz * seq, d)
    xp, pC, pn, pm, pST, p_conv = _even_prompt(xp, g_mix0, w_even_in, even_consts, w_even_out,
                                               seq_rows=seq, L=CHUNK, n_sub=2)
    xp, p_f0 = _ffn(xp, g_ffn0, w_up[0], ffn_conv_w[0], ffn_cb[0], w_dn[0], None, seq_rows=seq, tt=512)
    xp, pSr = _odd_prompt(xp, g_mix1, w_odd_in, cos_p, sin_p, _retention_tables(CHUNK, 0) + (ret_norm,), w_odd_out,
                          seq_rows=seq, L=CHUNK, n_sub=2)
    _, p_f1, yp = _ffn(xp, g_ffn1, w_up[1], ffn_conv_w[1], ffn_cb[1], w_dn[1], g_final, seq_rows=seq, tt=512)
    pS = pST.reshape(bsz, B_STATE, B_HEADS, B_HEADDIM).transpose(0, 2, 3, 1)

    xs = jnp.pad(x_sample, ((0, 0), (n_pad, 0), (0, 0))).reshape(dbsz * SEQ_PAD, d)
    n_seg = 8
    cos_s, sin_s = _rope_tables(np.tile(PAST_LEN + np.arange(SEQ_PAD) - n_pad, n_seg))
    hist = lambda a: jnp.pad(a, ((0, 0), (n_pad - a.shape[1], dseq), (0, 0))).reshape(dbsz * SEQ_PAD, a.shape[-1])
    state = (hist(state_ssd_conv[0]), state_mlstm_C[0],
             jnp.pad(state_mlstm_n[0], ((0, 0), (0, SUBLANES - A_HEADS), (0, 0))),
             _pad_lanes(state_mlstm_m[0])[:, None, :],
             state_ssd_h[0].reshape(dbsz, B_HEADS // 2, 2 * B_HEADDIM, B_STATE))
    to_tm = lambda a: a.reshape(dbsz, SEQ_PAD, d)[:, n_pad:, :].transpose(1, 0, 2).reshape(dseq * dbsz, d)
    from_tm = lambda a: jnp.pad(a.reshape(dseq, dbsz, d).transpose(1, 0, 2),
                                ((0, 0), (n_pad, 0), (0, 0))).reshape(dbsz * SEQ_PAD, d)
    tm = 256
    proj = _norm_proj(xs, g_mix0, w_even_in, EVEN_SPLITS, tm)
    y, sC, sn, sm, sS, s_conv = _even_sample(proj, state, even_consts, n_seg=n_seg, n_pad=n_pad)
    sS = sS.reshape(dbsz, B_HEADS, B_HEADDIM, B_STATE)
    xs = _proj_res(xs, y, w_even_out, tm)
    xt, s_f0 = _ffn_sample(to_tm(xs), state_ffn_conv, 0, g_ffn0, w_up[0], ffn_conv_w[0], ffn_cb[0], w_dn[0], None,
                           cw=D_FF // 2)
    xs = from_tm(xt)
    proj = _norm_proj(xs, g_mix1, w_odd_in, ODD_SPLITS, tm)
    y, sSr = _odd_sample(proj, state_ret_S[0], cos_s, sin_s,
                         _retention_tables(SEQ_PAD, n_pad, n_seg) + (ret_norm,), n_seg=n_seg)
    xs = _proj_res(xs, y, w_odd_out, tm)
    _, s_f1, yt = _ffn_sample(to_tm(xs), state_ffn_conv, 1, g_ffn1, w_up[1], ffn_conv_w[1], ffn_cb[1], w_dn[1],
                              g_final, cw=D_FF // 2)

    def conv_rows(tail, bs):
        return tail.reshape(bs, SUBLANES, tail.shape[-1])[:, SUBLANES - (B_CONV - 1):, :]

    def mlstm_states(C, n, m):
        return C[None], n[None, :, :A_HEADS, :], m[None, :, 0, :A_HEADS]

    p_ff = jnp.stack([p_f0, p_f1]).reshape(2, bsz, SUBLANES, 2 * D_FF)[:, :, SUBLANES - (FFN_CONV - 1):, :]
    s_ff = jnp.stack([s_f0, s_f1])
    p_states = mlstm_states(pC, pn, pm) + (conv_rows(p_conv, bsz)[None], pS[None], pSr[None], p_ff)
    s_states = mlstm_states(sC, sn, sm) + (conv_rows(s_conv, dbsz)[None], sS[None], sSr[None], s_ff)
    y_prompt = yp.reshape(bsz, seq, d)
    y_sample = yt.reshape(dseq, dbsz, d).transpose(1, 0, 2)
    return (y_prompt, y_sample) + p_states + s_states
```

```python
import functools

import numpy as np
import jax
import jax.numpy as jnp
from jax import lax
from jax.experimental import pallas as pl
from jax.experimental.pallas import tpu as pltpu

f32 = jnp.float32
bf16 = jnp.bfloat16

EPS = 1e-6
CHUNK = 128
D_MODEL = 1024
A_HEADS, A_DK, A_DV = 4, 128, 256
A_WIDTH = A_HEADS * A_DV
B_HEADS, B_HEADDIM, B_GROUPS, B_STATE, B_CONV = 16, 64, 2, 128, 4
B_WIDTH = B_HEADS * B_HEADDIM
B_CONV_DIM = B_WIDTH + 2 * B_GROUPS * B_STATE
B_HEADS_PER_GROUP = B_HEADS // B_GROUPS
B_GROUP_WIDTH = B_WIDTH // B_GROUPS
C_HEADS, C_DK, C_DV = 8, 128, 256
C_WIDTH = C_HEADS * C_DV
ROPE_BASE = 10000.0
D_FF = 2816
FFN_CONV = 3
PAST_LEN = 16384

LANES = 128
SUBLANES = 8
SEQ_PAD = SUBLANES
NEG_BIG = -1e30
VMEM_LIMIT = 56 * 1024 * 1024

EVEN_SPLITS = (A_HEADS * A_DK, A_HEADS * A_DK, A_WIDTH, A_WIDTH, B_WIDTH, B_CONV_DIM, 3 * LANES)
ODD_SPLITS = (C_HEADS * C_DK, C_HEADS * C_DK, C_WIDTH, C_WIDTH)


def _params(sem):
    return pltpu.CompilerParams(dimension_semantics=sem, vmem_limit_bytes=VMEM_LIMIT)


def _resident(shape, index_map):
    return pl.BlockSpec(shape, index_map, pipeline_mode=pl.Buffered(1))


def _nt(a, b):
    return lax.dot_general(a, b, (((1,), (1,)), ((), ())), preferred_element_type=f32)


def _tn(a, b):
    return lax.dot_general(a, b, (((0,), (0,)), ((), ())), preferred_element_type=f32)


def _nn(a, b):
    return jnp.dot(a, b, preferred_element_type=f32)


def _rmsnorm(x, g):
    return x * lax.rsqrt(jnp.mean(x * x, axis=-1, keepdims=True) + EPS) * g


def _layernorm_nogain(h):
    mu = jnp.mean(h, axis=-1, keepdims=True)
    hc = h - mu
    return hc * lax.rsqrt(jnp.mean(hc * hc, axis=-1, keepdims=True) + EPS)


def _softplus(x):
    return jnp.maximum(x, 0.0) + jnp.log1p(jnp.exp(-jnp.abs(x)))


def _silu(x):
    return x * jax.nn.sigmoid(x)


def _split3(x):
    hi = x.astype(bf16)
    r1 = x - hi.astype(f32)
    mid = r1.astype(bf16)
    lo = (r1 - mid.astype(f32)).astype(bf16)
    return hi, mid, lo


def _cumsum_rows(tril, x):
    hi, mid, lo = _split3(x)
    return _nn(tril, hi) + _nn(tril, mid) + _nn(tril, lo)


def _split_views(ref, splits):
    views, off = [], 0
    for n in splits:
        views.append(ref.at[:, off:off + n])
        off += n
    return views


def _project(h, w_ref, p_ref, splits):
    off = 0
    for n in splits:
        p_ref[:, off:off + n] = _nn(h, w_ref[:, off:off + n])
        off += n


MXU_COLS = 256


def _proj_pieces(h_ref, w_ref, dst_ref):
    def piece(c0, n):
        def run():
            dst_ref[:, c0:c0 + n] = _nn(h_ref[...], w_ref[:, c0:c0 + n])
        return run
    total = w_ref.shape[1]
    return [piece(c0, min(MXU_COLS, total - c0)) for c0 in range(0, total, MXU_COLS)]


def _out_pieces(x_ref, y_ref, w_ref, o_ref, rows):
    def piece(c0, n):
        def run():
            o_ref[rows, c0:c0 + n] = x_ref[rows, c0:c0 + n] + _nn(y_ref[...], w_ref[:, c0:c0 + n])
        return run
    total = w_ref.shape[1]
    return [piece(c0, min(MXU_COLS, total - c0)) for c0 in range(0, total, MXU_COLS)]


def _spread(work, n_slots):
    return [work[i * len(work) // n_slots:(i + 1) * len(work) // n_slots] for i in range(n_slots)]


def _norm_proj_body(x_ref, g_ref, w_ref, *o_refs, splits):
    h = _rmsnorm(x_ref[...], g_ref[...]).astype(bf16)
    off = 0
    for o_ref, n in zip(o_refs, splits):
        o_ref[...] = _nn(h, w_ref[:, off:off + n])
        off += n


def _norm_proj(x, g, w, splits, tm):
    rows, d = x.shape
    return pl.pallas_call(
        functools.partial(_norm_proj_body, splits=splits),
        grid=(rows // tm,),
        in_specs=[pl.BlockSpec((tm, d), lambda i: (i, 0)),
                  pl.BlockSpec((1, d), lambda i: (0, 0)),
                  _resident(w.shape, lambda i: (0, 0))],
        out_specs=[pl.BlockSpec((tm, n), lambda i: (i, 0)) for n in splits],
        out_shape=[jax.ShapeDtypeStruct((rows, n), f32) for n in splits],
        compiler_params=_params(("arbitrary",)),
        name="norm_proj",
    )(x, g, w)


def _proj_res_body(x_ref, y_ref, w_ref, o_ref):
    o_ref[...] = x_ref[...] + _nn(y_ref[...].astype(bf16), w_ref[...])


def _proj_res(x, y, w, tm):
    rows, d = x.shape
    k = y.shape[1]
    return pl.pallas_call(
        _proj_res_body,
        grid=(rows // tm,),
        in_specs=[pl.BlockSpec((tm, d), lambda i: (i, 0)),
                  pl.BlockSpec((tm, k), lambda i: (i, 0)),
                  _resident(w.shape, lambda i: (0, 0))],
        out_specs=pl.BlockSpec((tm, d), lambda i: (i, 0)),
        out_shape=jax.ShapeDtypeStruct((rows, d), f32),
        compiler_params=_params(("arbitrary",)),
        name="proj_res",
    )(x, y, w)


def _conv3(bias, prev2, prev1, cur, w_ref):
    y = bias + prev2 * w_ref[0:1, :]
    y = y + prev1 * w_ref[1:2, :]
    return y + cur * w_ref[2:3, :]


def _ffn_body(*refs, tt, final):
    it = iter(refs)
    x_ref, g_ref, wup_ref, cw_ref, cb_ref, wdn_ref = (next(it) for _ in range(6))
    gf_ref = next(it) if final else None
    o_ref, tail_ref = next(it), next(it)
    yn_ref = next(it) if final else None
    scr, carry = next(it), next(it)

    @pl.when(pl.program_id(1) == 0)
    def _():
        carry[...] = jnp.zeros(carry.shape, f32)

    x = x_ref[...]
    h = _rmsnorm(x, g_ref[...]).astype(bf16)
    conv = []
    for part in range(2):
        cols = slice(part * D_FF, (part + 1) * D_FF)
        u = _nn(h, wup_ref[:, cols])
        scr[0:SUBLANES, :] = carry[:, cols]
        tail_ref[:, cols] = u[tt - SUBLANES:tt, :]
        carry[:, cols] = u[tt - SUBLANES:tt, :]
        scr[SUBLANES:SUBLANES + tt, :] = u
        conv.append(_conv3(cb_ref[:, cols], scr[SUBLANES - 2:SUBLANES - 2 + tt, :],
                           scr[SUBLANES - 1:SUBLANES - 1 + tt, :], u, cw_ref.at[:, cols]))
    act = (_silu(conv[0]) * conv[1]).astype(bf16)
    out = x + _nn(act, wdn_ref[...])
    o_ref[...] = out
    if final:
        yn_ref[...] = _rmsnorm(out, gf_ref[...])


def _ffn(x, layer, g, wup, conv_w, conv_b, wdn, g_final, *, seq_rows, tt):
    rows, d = x.shape
    final = g_final is not None
    n_seq = rows // seq_rows
    n_inner = seq_rows // tt
    rmap = lambda s, t: (s * n_inner + t, 0)
    cmap = lambda s, t: (0, 0)
    lmap = lambda s, t: (layer, 0, 0)
    in_specs = [pl.BlockSpec((tt, d), rmap), pl.BlockSpec((1, d), cmap), _resident((None,) + wup.shape[1:], lmap),
                pl.BlockSpec(conv_w.shape, cmap), pl.BlockSpec(conv_b.shape, cmap),
                _resident((None,) + wdn.shape[1:], lmap)]
    args = [x, g, wup, conv_w, conv_b, wdn]
    out_specs = [pl.BlockSpec((tt, d), rmap), pl.BlockSpec((SUBLANES, 2 * D_FF), lambda s, t: (s, 0))]
    out_shape = [jax.ShapeDtypeStruct((rows, d), f32), jax.ShapeDtypeStruct((n_seq * SUBLANES, 2 * D_FF), f32)]
    if final:
        in_specs.append(pl.BlockSpec((1, d), cmap))
        args.append(g_final)
        out_specs.append(pl.BlockSpec((tt, d), rmap))
        out_shape.append(jax.ShapeDtypeStruct((rows, d), f32))
    return pl.pallas_call(
        functools.partial(_ffn_body, tt=tt, final=final),
        grid=(n_seq, n_inner), in_specs=in_specs, out_specs=out_specs, out_shape=out_shape,
        scratch_shapes=[pltpu.VMEM((SUBLANES + tt, D_FF), f32), pltpu.VMEM((SUBLANES, 2 * D_FF), f32)],
        compiler_params=_params(("arbitrary", "arbitrary")),
        name="ffn_prompt",
    )(*args)


def _ffn_sample_body(*refs, nb, nj, final):
    it = iter(refs)
    x_ref, hist_ref, g_ref, wup_ref, cw_ref, cb_ref, wdn_ref = (next(it) for _ in range(7))
    gf_ref = next(it) if final else None
    o_ref, tail_ref = next(it), next(it)
    yn_ref = next(it) if final else None
    h_scr, convg_scr, acc_scr = next(it), next(it), next(it)
    c = pl.program_id(0)
    tt = x_ref.shape[0]

    @pl.when(c == 0)
    def _():
        h_scr[...] = _rmsnorm(x_ref[...], g_ref[...]).astype(bf16)
        acc_scr[...] = jnp.zeros(acc_scr.shape, f32)

    u = _nn(h_scr[...], wup_ref[...])
    ext = jnp.concatenate([hist_ref[:, k, :] for k in range(FFN_CONV - 1)] + [u], axis=0)
    for k in range(FFN_CONV - 1):
        tail_ref[:, k, :] = ext[tt + k * nb:tt + (k + 1) * nb, :]
    y = _conv3(cb_ref[...], ext[0:tt, :], ext[nb:nb + tt, :], u, cw_ref)

    @pl.when(c < nj)
    def _():
        convg_scr[c] = y

    @pl.when(c >= nj)
    def _():
        act = (_silu(convg_scr[c - nj]) * y).astype(bf16)
        acc_scr[...] += _nn(act, wdn_ref[...])

    @pl.when(c == 2 * nj - 1)
    def _():
        out = x_ref[...] + acc_scr[...]
        o_ref[...] = out
        if final:
            yn_ref[...] = _rmsnorm(out, gf_ref[...])


def _ffn_sample(x, hist, layer, g, wup, conv_w, conv_b, wdn, g_final, *, cw):
    rows, d = x.shape
    nb = hist.shape[1]
    nj = D_FF // cw
    final = g_final is not None
    fixed = lambda c: (0, 0)
    colblk = lambda c: (0, c)
    in_specs = [pl.BlockSpec((rows, d), fixed),
                pl.BlockSpec((None, nb, FFN_CONV - 1, cw), lambda c: (layer, 0, 0, c)),
                pl.BlockSpec((1, d), fixed), pl.BlockSpec((None, d, cw), lambda c: (layer, 0, c)),
                pl.BlockSpec((FFN_CONV, cw), colblk), pl.BlockSpec((1, cw), colblk),
                pl.BlockSpec((None, cw, d), lambda c: (layer, jnp.maximum(c - nj, 0), 0))]
    args = [x, hist, g, wup, conv_w, conv_b, wdn]
    out_specs = [pl.BlockSpec((rows, d), fixed), pl.BlockSpec((nb, FFN_CONV - 1, cw), lambda c: (0, 0, c))]
    out_shape = [jax.ShapeDtypeStruct((rows, d), f32), jax.ShapeDtypeStruct((nb, FFN_CONV - 1, 2 * D_FF), f32)]
    if final:
        in_specs.append(pl.BlockSpec((1, d), fixed))
        args.append(g_final)
        out_specs.append(pl.BlockSpec((rows, d), fixed))
        out_shape.append(jax.ShapeDtypeStruct((rows, d), f32))
    return pl.pallas_call(
        functools.partial(_ffn_sample_body, nb=nb, nj=nj, final=final),
        grid=(2 * nj,), in_specs=in_specs, out_specs=out_specs, out_shape=out_shape,
        scratch_shapes=[pltpu.VMEM((rows, d), bf16), pltpu.VMEM((nj, rows, cw), f32), pltpu.VMEM((rows, d), f32)],
        compiler_params=_params(("arbitrary",)),
        name="ffn_sample",
    )(*args)


EVEN_SLOTS = A_HEADS + B_HEADS // 2


def _even_chunk(r0, slot, L, n_pad, n_seg, proj_refs, hist_ref, const_refs, y_ref, state_in_refs, state_refs,
                tail_ref, conv_scr, yb_scr, side=None):
    q_ref, k_ref, v_ref, og_ref, z_ref, xbc_ref, gt_ref = proj_refs
    gb_ref, anorm_ref, convw_ref, convb_ref, alog_ref, dx_ref, bnorm_ref = const_refs
    C_in, n_in, m_in, S_in = state_in_refs
    C_ref, n_ref, m_ref, S_ref = state_refs
    rows = pl.ds(r0, L)
    seg = L // n_seg
    seg_rows = [slice(i * seg, (i + 1) * seg) for i in range(n_seg)]
    seg_last = [slice((i + 1) * seg - 1, (i + 1) * seg) for i in range(n_seg)]

    def per_row(vals):
        if n_seg == 1:
            return vals[0]
        return jnp.concatenate([jnp.broadcast_to(v, (seg, v.shape[1])) for v in vals], axis=0)

    ri = lax.broadcasted_iota(jnp.int32, (L, L), 0)
    ci = lax.broadcasted_iota(jnp.int32, (L, L), 1)
    causal = ri >= ci
    if n_seg > 1:
        causal = causal & ((ri // seg) == (ci // seg))
    tril = causal.astype(bf16)
    valid = None
    if n_pad:
        valid = (lax.broadcasted_iota(jnp.int32, (L, 1), 0) % seg) >= n_pad

    gates = gt_ref[rows, :]
    li = gates[:, 0:LANES] + gb_ref[0:1, :]
    fpre = gates[:, LANES:2 * LANES] + gb_ref[1:2, :]
    lf = -_softplus(-fpre)
    dt = _softplus(gates[:, 2 * LANES:3 * LANES] + gb_ref[2:3, :])
    if n_pad:
        li = jnp.where(valid, li, NEG_BIG)
        lf = jnp.where(valid, lf, 0.0)
        dt = jnp.where(valid, dt, 0.0)
    a = dt * (-jnp.exp(alog_ref[...]))
    cums = _cumsum_rows(tril, jnp.concatenate([lf, a], axis=1))
    bcum = cums[:, 0:LANES]
    acum = cums[:, LANES:2 * LANES]
    liT, bT, aT, dtT = li.T, bcum.T, acum.T, dt.T

    m_old = [m_in[slot + i] for i in range(n_seg)]
    b_lasts = [bcum[r, :] for r in seg_last]
    m_rows, b_last = per_row(m_old), per_row(b_lasts)
    log_g = b_last - bcum + li
    m_news = [jnp.maximum(b_lasts[i] + m_old[i], jnp.max(log_g[seg_rows[i], :], axis=0, keepdims=True))
              for i in range(n_seg)]
    m_new = per_row(m_news)
    gfac = jnp.exp(log_g - m_new)
    cdecay = jnp.exp(b_last + m_rows - m_new)
    for i in range(n_seg):
        m_ref[slot + i] = m_news[i]
    for h in range(A_HEADS):
        for run in (side[h] if side else ()):
            run()
        bcol, brow, lirow = bcum[:, h:h + 1], bT[h:h + 1, :], liT[h:h + 1, :]
        m_h = m_rows[:, h:h + 1]
        logw = jnp.where(causal, bcol - brow + lirow, -jnp.inf)
        log_prev = bcol + m_h
        m_t = jnp.maximum(log_prev, jnp.max(logw, axis=-1, keepdims=True))
        w_in = jnp.exp(logw - m_t)
        w_prev = jnp.exp(log_prev - m_t)
        qh = q_ref[rows, h * A_DK:(h + 1) * A_DK]
        kh = k_ref[rows, h * A_DK:(h + 1) * A_DK] * (A_DK ** -0.5)
        vh = v_ref[rows, h * A_DV:(h + 1) * A_DV]
        vb = vh.astype(bf16)
        qb = qh.astype(bf16)
        s = _nt(qb, kh.astype(bf16)) * w_in
        kg = kh * gfac[:, h:h + 1]
        Cs = [C_in[slot + i, h] for i in range(n_seg)]
        ns = [n_in[slot + i, h:h + 1, :] for i in range(n_seg)]
        if n_seg == 1:
            inter = _nn(qb, Cs[0].astype(bf16))
            dec_h = cdecay[:, h:h + 1]
            C_ref[slot, h] = Cs[0] * dec_h + _tn(kg.astype(bf16), vb)
            n_ref[slot, h:h + 1, :] = ns[0] * dec_h + jnp.sum(kg, axis=0, keepdims=True)
        else:
            inter = jnp.concatenate([_nn(qh[seg_rows[i], :].astype(bf16), Cs[i].astype(bf16))
                                     for i in range(n_seg)], axis=0)
            for i in range(n_seg):
                rs = seg_rows[i]
                dec_h = cdecay[seg_last[i], h:h + 1]
                C_ref[slot + i, h] = Cs[i] * dec_h + _tn(kg[rs, :].astype(bf16), vh[rs, :].astype(bf16))
                n_ref[slot + i, h:h + 1, :] = ns[i] * dec_h + jnp.sum(kg[rs, :], axis=0, keepdims=True)
        num = _nn(s.astype(bf16), vb) + inter * w_prev
        den = jnp.sum(s, axis=-1, keepdims=True) + jnp.sum(qh * per_row(ns), axis=-1, keepdims=True) * w_prev
        hout = num / jnp.maximum(jnp.abs(den), jnp.exp(-m_t))
        ya = _layernorm_nogain(hout) * anorm_ref[:, h * A_DV:(h + 1) * A_DV]
        ya = ya * jax.nn.sigmoid(og_ref[rows, h * A_DV:(h + 1) * A_DV])
        y_ref[rows, h * A_DV:(h + 1) * A_DV] = ya.astype(y_ref.dtype)

    xraw = xbc_ref[rows, :]
    if hist_ref is not None:
        seg_row = lax.broadcasted_iota(jnp.int32, (L, 1), 0) % seg
        is_hist = (seg_row >= n_pad - (B_CONV - 1)) & jnp.logical_not(valid)
        xraw = jnp.where(is_hist, hist_ref[rows, :], xraw)
    conv_scr[SUBLANES:SUBLANES + L, :] = xraw
    xc = convb_ref[...] + conv_scr[SUBLANES - 3:SUBLANES - 3 + L, :] * convw_ref[0:1, :]
    xc = xc + conv_scr[SUBLANES - 2:SUBLANES - 2 + L, :] * convw_ref[1:2, :]
    xc = xc + conv_scr[SUBLANES - 1:SUBLANES - 1 + L, :] * convw_ref[2:3, :]
    xc = xc + xraw * convw_ref[3:4, :]
    if n_seg == 1:
        new_tail = conv_scr[L:L + SUBLANES, :]
        tail_ref[...] = new_tail
        conv_scr[0:SUBLANES, :] = new_tail
    else:
        tail_ref[...] = xraw
    xc = _silu(xc)

    a_last = per_row([acum[r, :] for r in seg_last])
    wtile = jnp.exp(a_last - acum) * dt
    expa = jnp.exp(acum)

    def head_matrix(cb, h):
        acol, arow, dtrow = acum[:, h:h + 1], aT[h:h + 1, :], dtT[h:h + 1, :]
        decay = jnp.where(causal, jnp.exp(jnp.where(causal, acol - arow, 0.0)), 0.0)
        return (cb * decay * dtrow).astype(bf16)

    def group_bc(g):
        Bg = xc[:, B_WIDTH + g * B_STATE:B_WIDTH + (g + 1) * B_STATE].astype(bf16)
        c0 = B_WIDTH + B_GROUPS * B_STATE + g * B_STATE
        Cg = xc[:, c0:c0 + B_STATE].astype(bf16)
        return Bg, Cg, _nt(Cg, Bg)

    low_half = lax.broadcasted_iota(jnp.int32, (L, LANES), 1) < B_HEADDIM
    first_head_rows = lax.broadcasted_iota(jnp.int32, (2 * B_HEADDIM, 1), 0) < B_HEADDIM

    def pair_lanes(t, j):
        return jnp.where(low_half, jnp.broadcast_to(t[:, 2 * j:2 * j + 1], (L, LANES)),
                         jnp.broadcast_to(t[:, 2 * j + 1:2 * j + 2], (L, LANES)))

    pairs_per_group = B_HEADS_PER_GROUP // 2
    for g in range(B_GROUPS):
        Bg, Cg, cb = group_bc(g)
        gs = slice(g * B_GROUP_WIDTH, (g + 1) * B_GROUP_WIDTH)
        if n_seg == 1:
            ST = S_in[slot, :, gs]
            inter_g = _nn(Cg, ST.astype(bf16))
        else:
            Bf = xc[:, B_WIDTH + g * B_STATE:B_WIDTH + (g + 1) * B_STATE]
            c0 = B_WIDTH + B_GROUPS * B_STATE + g * B_STATE
            Cf = xc[:, c0:c0 + B_STATE]
        xw, decs = [], []
        for jp in range(pairs_per_group):
            j = g * pairs_per_group + jp
            for run in (side[A_HEADS + j] if side else ()):
                run()
            ps = slice(j * LANES, (j + 1) * LANES)
            e_pair, w_pair = pair_lanes(expa, j), pair_lanes(wtile, j)
            mcat = jnp.concatenate([head_matrix(cb, 2 * j), head_matrix(cb, 2 * j + 1)], axis=1)
            xp = xc[:, ps]
            xbd = jnp.concatenate([jnp.where(low_half, xp, 0.0).astype(bf16),
                                   jnp.where(low_half, 0.0, xp).astype(bf16)], axis=0)
            if n_seg == 1:
                inter = inter_g[:, jp * LANES:(jp + 1) * LANES]
                xw.append((xp * w_pair).astype(bf16))
                decs.append(e_pair[L - 1:L, :])
            else:
                xwf = xp * w_pair
                parts = []
                for i in range(n_seg):
                    rs = seg_rows[i]
                    Sp = S_in[slot + i, j]
                    parts.append(_nt(Cf[rs, :].astype(bf16), Sp.astype(bf16)))
                    e_last = e_pair[seg_last[i], :]
                    dec_col = jnp.where(first_head_rows, e_last[:, 0:1], e_last[:, B_HEADDIM:B_HEADDIM + 1])
                    S_ref[slot + i, j] = Sp * dec_col + _tn(xwf[rs, :].astype(bf16), Bf[rs, :].astype(bf16))
                inter = jnp.concatenate(parts, axis=0)
            yb_scr[:, ps] = (_nn(mcat, xbd) + inter * e_pair) + dx_ref[:, ps] * xp
        if n_seg == 1:
            S_ref[slot, :, gs] = ST * jnp.concatenate(decs, axis=1) + _tn(Bg, jnp.concatenate(xw, axis=1))
    yb = yb_scr[...] * _silu(z_ref[rows, :])
    for g in range(B_GROUPS):
        gs = slice(g * B_GROUP_WIDTH, (g + 1) * B_GROUP_WIDTH)
        yg = yb[:, gs]
        yg = yg * lax.rsqrt(jnp.mean(yg * yg, axis=-1, keepdims=True) + EPS)
        y_ref[rows, A_WIDTH + g * B_GROUP_WIDTH:A_WIDTH + (g + 1) * B_GROUP_WIDTH] = (
            yg * bnorm_ref[:, gs]).astype(y_ref.dtype)


N_EVEN_CONSTS = 7


def _even_prompt_body(*refs, L, n_sub):
    it = iter(refs)
    x_ref, xnext_ref, gmix_ref, win_ref = next(it), next(it), next(it), next(it)
    const_refs = tuple(next(it) for _ in range(N_EVEN_CONSTS))
    wout_ref = next(it)
    o_ref = next(it)
    state_refs = tuple(next(it) for _ in range(4))
    tail_ref = next(it)
    p_scr, y_scr = (next(it), next(it)), (next(it), next(it))
    h_scr, conv_scr, yb_scr = (next(it) for _ in range(3))
    th = L * n_sub
    gmix = gmix_ref[...]

    @pl.when((pl.program_id(0) == 0) & (pl.program_id(1) == 0))
    def _():
        _project(_rmsnorm(x_ref[0:th, :], gmix).astype(bf16), win_ref, p_scr[0], EVEN_SPLITS)

    @pl.when(pl.program_id(1) == 0)
    def _():
        conv_scr[0:SUBLANES, :] = jnp.zeros((SUBLANES, B_CONV_DIM), f32)
        for r in state_refs:
            r[...] = jnp.zeros(r.shape, f32)

    for half in range(2):
        ahead = xnext_ref[...] if half else x_ref[th:2 * th, :]
        h_scr[...] = _rmsnorm(ahead, gmix).astype(bf16)
        work = _proj_pieces(h_scr, win_ref, p_scr[1 - half])
        if half:
            work = work + _out_pieces(x_ref, y_scr[0], wout_ref, o_ref, slice(0, th))
        side = _spread(work, n_sub * EVEN_SLOTS)
        proj_refs = _split_views(p_scr[half], EVEN_SPLITS)
        for c in range(n_sub):
            _even_chunk(c * L, 0, L, 0, 1, proj_refs, None, const_refs, y_scr[half], state_refs, state_refs,
                        tail_ref, conv_scr, yb_scr, side[c * EVEN_SLOTS:(c + 1) * EVEN_SLOTS])
    for run in _out_pieces(x_ref, y_scr[1], wout_ref, o_ref, slice(th, 2 * th)):
        run()


def _even_prompt(x, gmix, win, consts, wout, *, seq_rows, L, n_sub):
    rows, d = x.shape
    n_seq = rows // seq_rows
    th = L * n_sub
    tt = 2 * th
    n_inner = seq_rows // tt
    n_half = rows // th
    rmap = lambda s, t: (s * n_inner + t, 0)
    nextmap = lambda s, t: (jnp.minimum(2 * (s * n_inner + t) + 2, n_half - 1), 0)
    cmap = lambda s, t: (0, 0)
    smap4 = lambda s, t: (s, 0, 0, 0)
    smap3 = lambda s, t: (s, 0, 0)
    st_specs = [pl.BlockSpec((1, A_HEADS, A_DK, A_DV), smap4), pl.BlockSpec((1, SUBLANES, A_DK), smap3),
                pl.BlockSpec((1, 1, LANES), smap3), pl.BlockSpec((1, B_STATE, B_WIDTH), smap3)]
    st_shapes = [jax.ShapeDtypeStruct((n_seq, A_HEADS, A_DK, A_DV), f32),
                 jax.ShapeDtypeStruct((n_seq, SUBLANES, A_DK), f32),
                 jax.ShapeDtypeStruct((n_seq, 1, LANES), f32),
                 jax.ShapeDtypeStruct((n_seq, B_STATE, B_WIDTH), f32)]
    in_specs = ([pl.BlockSpec((tt, d), rmap), pl.BlockSpec((th, d), nextmap), pl.BlockSpec((1, d), cmap),
                 _resident(win.shape, cmap)]
                + [pl.BlockSpec(c.shape, cmap) for c in consts] + [_resident(wout.shape, cmap)])
    p_shape = pltpu.VMEM((th, sum(EVEN_SPLITS)), f32)
    y_shape = pltpu.VMEM((th, A_WIDTH + B_WIDTH), bf16)
    return pl.pallas_call(
        functools.partial(_even_prompt_body, L=L, n_sub=n_sub),
        grid=(n_seq, n_inner), in_specs=in_specs,
        out_specs=[pl.BlockSpec((tt, d), rmap)] + st_specs + [pl.BlockSpec((SUBLANES, B_CONV_DIM), lambda s, t: (s, 0))],
        out_shape=[jax.ShapeDtypeStruct((rows, d), f32)] + st_shapes
        + [jax.ShapeDtypeStruct((n_seq * SUBLANES, B_CONV_DIM), f32)],
        scratch_shapes=[p_shape, p_shape, y_shape, y_shape, pltpu.VMEM((th, d), bf16),
                        pltpu.VMEM((SUBLANES + L, B_CONV_DIM), f32), pltpu.VMEM((L, B_WIDTH), f32)],
        compiler_params=_params(("arbitrary", "arbitrary")),
        name="even_prompt",
    )(x, x, gmix, win, *consts, wout)


def _even_sample_body(*refs, L, n_seg, n_pad):
    it = iter(refs)
    proj_refs = tuple(next(it) for _ in range(7))
    hist_ref = next(it)
    state_in_refs = tuple(next(it) for _ in range(4))
    const_refs = tuple(next(it) for _ in range(N_EVEN_CONSTS))
    y_ref = next(it)
    state_refs = tuple(next(it) for _ in range(4))
    tail_ref = next(it)
    conv_scr, yb_scr = next(it), next(it)

    conv_scr[0:SUBLANES, :] = jnp.zeros((SUBLANES, B_CONV_DIM), f32)
    n_ref = state_refs[1]
    n_ref[:, A_HEADS:, :] = jnp.zeros((n_seg, SUBLANES - A_HEADS, A_DK), f32)
    _even_chunk(0, 0, L, n_pad, n_seg, proj_refs, hist_ref, const_refs, y_ref, state_in_refs, state_refs,
                tail_ref, conv_scr, yb_scr)


def _even_sample(proj, state, consts, *, n_seg, n_pad):
    rows = proj[0].shape[0]
    L = SEQ_PAD * n_seg
    rmap = lambda o: (o, 0)
    cmap = lambda o: (0, 0)
    smap4 = lambda o: (o, 0, 0, 0)
    smap3 = lambda o: (o, 0, 0)
    st_specs = [pl.BlockSpec((n_seg, A_HEADS, A_DK, A_DV), smap4), pl.BlockSpec((n_seg, SUBLANES, A_DK), smap3),
                pl.BlockSpec((n_seg, 1, LANES), smap3),
                pl.BlockSpec((n_seg, B_HEADS // 2, 2 * B_HEADDIM, B_STATE), smap4)]
    st_shapes = [jax.ShapeDtypeStruct(s.shape, f32) for s in state[1:]]
    in_specs = ([pl.BlockSpec((L, a.shape[1]), rmap) for a in proj] + [pl.BlockSpec((L, B_CONV_DIM), rmap)]
                + st_specs + [pl.BlockSpec(c.shape, cmap) for c in consts])
    return pl.pallas_call(
        functools.partial(_even_sample_body, L=L, n_seg=n_seg, n_pad=n_pad),
        grid=(rows // L,), in_specs=in_specs,
        out_specs=[pl.BlockSpec((L, A_WIDTH + B_WIDTH), rmap)] + st_specs + [pl.BlockSpec((L, B_CONV_DIM), rmap)],
        out_shape=[jax.ShapeDtypeStruct((rows, A_WIDTH + B_WIDTH), f32)] + st_shapes
        + [jax.ShapeDtypeStruct((rows, B_CONV_DIM), f32)],
        scratch_shapes=[pltpu.VMEM((SUBLANES + L, B_CONV_DIM), f32), pltpu.VMEM((L, B_WIDTH), f32)],
        compiler_params=_params(("arbitrary",)),
        name="even_sample",
    )(*proj, *state, *consts)


def _odd_chunk(r0, rope_r0, slot, L, n_seg, proj_refs, const_refs, y_ref, S_in, S_ref, side=None):
    q_ref, k_ref, v_ref, g_ref = proj_refs
    cos_ref, sin_ref, intra_ref, cross_ref, into_ref, cdec_ref, norm_ref = const_refs
    rows = pl.ds(r0, L)
    seg = L // n_seg
    cosf = cos_ref[pl.ds(rope_r0, L), :]
    sinf = sin_ref[pl.ds(rope_r0, L), :]
    cross = cross_ref[...]
    into = into_ref[...]
    cdec = cdec_ref[...]
    for h in range(C_HEADS):
        for run in (side[h] if side else ()):
            run()
        qh = q_ref[rows, h * C_DK:(h + 1) * C_DK]
        kh = k_ref[rows, h * C_DK:(h + 1) * C_DK]
        qh = qh * cosf + pltpu.roll(qh, C_DK // 2, 1) * sinf
        kh = (kh * cosf + pltpu.roll(kh, C_DK // 2, 1) * sinf) * (C_DK ** -0.5)
        vh = v_ref[rows, h * C_DV:(h + 1) * C_DV]
        vb = vh.astype(bf16)
        qb = qh.astype(bf16)
        s = _nt(qb, kh.astype(bf16)) * intra_ref[h]
        kd = kh * into[:, h:h + 1]
        if n_seg == 1:
            S = S_in[slot, h]
            inter = _nn(qb, S.astype(bf16))
            S_ref[slot, h] = S * cdec[:, h:h + 1] + _tn(kd.astype(bf16), vb)
        else:
            parts = []
            for i in range(n_seg):
                rs = slice(i * seg, (i + 1) * seg)
                S = S_in[slot + i, h]
                parts.append(_nn(qh[rs, :].astype(bf16), S.astype(bf16)))
                S_ref[slot + i, h] = S * cdec[:, h:h + 1] + _tn(kd[rs, :].astype(bf16), vh[rs, :].astype(bf16))
            inter = jnp.concatenate(parts, axis=0)
        o = _nn(s.astype(bf16), vb) + inter * cross[:, h:h + 1]
        y = _layernorm_nogain(o) * norm_ref[:, h * C_DV:(h + 1) * C_DV]
        y = y * _silu(g_ref[rows, h * C_DV:(h + 1) * C_DV])
        y_ref[rows, h * C_DV:(h + 1) * C_DV] = y.astype(y_ref.dtype)


N_ODD_CONSTS = 7


def _odd_prompt_body(*refs, L, n_sub):
    it = iter(refs)
    x_ref, xnext_ref, gmix_ref, win_ref = next(it), next(it), next(it), next(it)
    const_refs = tuple(next(it) for _ in range(N_ODD_CONSTS))
    wout_ref = next(it)
    o_ref, S_ref = next(it), next(it)
    p_scr, y_scr = (next(it), next(it)), (next(it), next(it))
    h_scr = next(it)
    th = L * n_sub
    gmix = gmix_ref[...]

    @pl.when((pl.program_id(0) == 0) & (pl.program_id(1) == 0))
    def _():
        _project(_rmsnorm(x_ref[0:th, :], gmix).astype(bf16), win_ref, p_scr[0], ODD_SPLITS)

    @pl.when(pl.program_id(1) == 0)
    def _():
        S_ref[...] = jnp.zeros(S_ref.shape, f32)

    for half in range(2):
        ahead = xnext_ref[...] if half else x_ref[th:2 * th, :]
        h_scr[...] = _rmsnorm(ahead, gmix).astype(bf16)
        work = _proj_pieces(h_scr, win_ref, p_scr[1 - half])
        if half:
            work = work + _out_pieces(x_ref, y_scr[0], wout_ref, o_ref, slice(0, th))
        side = _spread(work, n_sub * C_HEADS)
        proj_refs = _split_views(p_scr[half], ODD_SPLITS)
        for c in range(n_sub):
            _odd_chunk(c * L, half * th + c * L, 0, L, 1, proj_refs, const_refs, y_scr[half], S_ref, S_ref,
                       side[c * C_HEADS:(c + 1) * C_HEADS])
    for run in _out_pieces(x_ref, y_scr[1], wout_ref, o_ref, slice(th, 2 * th)):
        run()


def _odd_prompt(x, gmix, win, cos, sin, consts, wout, *, seq_rows, L, n_sub):
    rows, d = x.shape
    n_seq = rows // seq_rows
    th = L * n_sub
    tt = 2 * th
    n_inner = seq_rows // tt
    n_half = rows // th
    rmap = lambda s, t: (s * n_inner + t, 0)
    nextmap = lambda s, t: (jnp.minimum(2 * (s * n_inner + t) + 2, n_half - 1), 0)
    cmap = lambda s, t: (0, 0)
    st_spec = pl.BlockSpec((1, C_HEADS, C_DK, C_DV), lambda s, t: (s, 0, 0, 0))
    rope_spec = pl.BlockSpec((tt, LANES), lambda s, t: (t, 0))
    in_specs = [pl.BlockSpec((tt, d), rmap), pl.BlockSpec((th, d), nextmap), pl.BlockSpec((1, d), cmap),
                _resident(win.shape, cmap), rope_spec, rope_spec]
    for c in consts:
        in_specs.append(pl.BlockSpec(c.shape, (lambda s, t: (0, 0, 0)) if c.ndim == 3 else cmap))
    in_specs.append(_resident(wout.shape, cmap))
    p_shape = pltpu.VMEM((th, sum(ODD_SPLITS)), f32)
    y_shape = pltpu.VMEM((th, C_WIDTH), bf16)
    return pl.pallas_call(
        functools.partial(_odd_prompt_body, L=L, n_sub=n_sub),
        grid=(n_seq, n_inner), in_specs=in_specs,
        out_specs=[pl.BlockSpec((tt, d), rmap), st_spec],
        out_shape=[jax.ShapeDtypeStruct((rows, d), f32), jax.ShapeDtypeStruct((n_seq, C_HEADS, C_DK, C_DV), f32)],
        scratch_shapes=[p_shape, p_shape, y_shape, y_shape, pltpu.VMEM((th, d), bf16)],
        compiler_params=_params(("arbitrary", "arbitrary")),
        name="odd_prompt",
    )(x, x, gmix, win, cos, sin, *consts, wout)


def _odd_sample_body(*refs, L, n_seg):
    it = iter(refs)
    proj_refs = tuple(next(it) for _ in range(4))
    S_in = next(it)
    const_refs = tuple(next(it) for _ in range(N_ODD_CONSTS))
    y_ref, S_ref = next(it), next(it)
    _odd_chunk(0, 0, 0, L, n_seg, proj_refs, const_refs, y_ref, S_in, S_ref)


def _odd_sample(proj, S0, cos, sin, consts, *, n_seg):
    rows = proj[0].shape[0]
    L = SEQ_PAD * n_seg
    rmap = lambda o: (o, 0)
    cmap = lambda o: (0, 0)
    st_spec = pl.BlockSpec((n_seg, C_HEADS, C_DK, C_DV), lambda o: (o, 0, 0, 0))
    in_specs = [pl.BlockSpec((L, a.shape[1]), rmap) for a in proj] + [st_spec]
    for c in (cos, sin) + tuple(consts):
        in_specs.append(pl.BlockSpec(c.shape, (lambda o: (0, 0, 0)) if c.ndim == 3 else cmap))
    return pl.pallas_call(
        functools.partial(_odd_sample_body, L=L, n_seg=n_seg),
        grid=(rows // L,), in_specs=in_specs,
        out_specs=[pl.BlockSpec((L, C_WIDTH), rmap), st_spec],
        out_shape=[jax.ShapeDtypeStruct((rows, C_WIDTH), f32), jax.ShapeDtypeStruct(S0.shape, f32)],
        compiler_params=_params(("arbitrary",)),
        name="odd_sample",
    )(*proj, S0, cos, sin, *consts)


def _pad_lanes(a, width=LANES):
    return jnp.pad(a, [(0, 0)] * (a.ndim - 1) + [(0, width - a.shape[-1])])


def _even_w_in_cols(w):
    sizes = [A_HEADS * A_DK, A_HEADS * A_DK, A_WIDTH, A_HEADS, A_HEADS, A_WIDTH, B_WIDTH, B_CONV_DIM, B_HEADS]
    q, k, v, ig, fg, og, z, xbc, dt = jnp.split(w, np.cumsum(sizes)[:-1].tolist(), axis=-1)
    return jnp.concatenate([q, k, v, og, z, xbc, _pad_lanes(ig), _pad_lanes(fg), _pad_lanes(dt)], axis=-1).astype(bf16)


def _retention_tables(seg, n_pad, n_seg=1):
    log_gamma = np.log1p(-np.exp2(-5.0 - np.arange(C_HEADS, dtype=np.float64)))
    t_real = seg - n_pad
    idx = np.arange(seg, dtype=np.float64) - n_pad
    real = idx >= 0
    diff = idx[:, None] - idx[None, :]
    intra = np.where((diff >= 0) & real[None, :], np.exp(log_gamma[:, None, None] * np.maximum(diff, 0.0)), 0.0)
    intra = np.stack([np.kron(np.eye(n_seg), intra[h]) for h in range(C_HEADS)])
    cross = np.where(real[:, None], np.exp(log_gamma[None, :] * (idx[:, None] + 1.0)), 0.0)
    into = np.where(real[:, None], np.exp(log_gamma[None, :] * (t_real - 1.0 - idx[:, None])), 0.0)
    cdec = np.exp(log_gamma * t_real)[None, :]
    lanes = lambda a: np.pad(a, ((0, 0), (0, LANES - a.shape[1]))).astype(np.float32)
    return intra.astype(np.float32), lanes(np.tile(cross, (n_seg, 1))), lanes(np.tile(into, (n_seg, 1))), lanes(cdec)


def _rope_tables(pos):
    half = C_DK // 2
    inv = ROPE_BASE ** (-np.arange(half, dtype=np.float64) / half)
    ang = np.asarray(pos, dtype=np.float64)[:, None] * inv[None, :]
    cos, sin = np.cos(ang), np.sin(ang)
    return (np.concatenate([cos, cos], axis=-1).astype(np.float32),
            np.concatenate([-sin, sin], axis=-1).astype(np.float32))


def kernel(x_prompt, x_sample, state_mlstm_C, state_mlstm_n, state_mlstm_m, state_ssd_conv, state_ssd_h, state_ret_S, state_ffn_conv, norm_mix_g, norm_ffn_g, norm_final_g, even_w_in, mlstm_igate_b, mlstm_fgate_b, mlstm_norm_g, ssd_conv_w, ssd_conv_b, ssd_dt_bias, ssd_A_log, ssd_D, ssd_norm_g, even_w_out, odd_w_in, ret_norm_g, odd_w_out, ffn_w_up, ffn_conv_w, ffn_conv_b, ffn_w_down):
    bsz, seq, d = x_prompt.shape
    dbsz, dseq, _ = x_sample.shape
    n_pad = SEQ_PAD - dseq
    assert norm_mix_g.shape[0] == 2 and B_CONV - 1 <= n_pad < SEQ_PAD and seq % (4 * CHUNK) == 0

    row = lambda a: a.reshape(1, -1)
    gate_bias = jnp.concatenate([_pad_lanes(row(mlstm_igate_b[0])), _pad_lanes(row(mlstm_fgate_b[0])),
                                 _pad_lanes(row(ssd_dt_bias[0]))], axis=0)
    even_consts = (gate_bias, row(mlstm_norm_g[0]), ssd_conv_w[0], row(ssd_conv_b[0]), _pad_lanes(row(ssd_A_log[0])),
                   row(jnp.repeat(ssd_D[0], B_HEADDIM)), row(ssd_norm_g[0]))
    g_mix0, g_mix1 = row(norm_mix_g[0]), row(norm_mix_g[1])
    g_ffn0, g_ffn1, g_final = row(norm_ffn_g[0]), row(norm_ffn_g[1]), row(norm_final_g)
    w_even_in, w_even_out = _even_w_in_cols(even_w_in[0]), even_w_out[0].astype(bf16)
    w_odd_in, w_odd_out = odd_w_in[0].astype(bf16), odd_w_out[0].astype(bf16)
    w_up, w_dn = ffn_w_up.astype(bf16), ffn_w_down.astype(bf16)
    ffn_cb = ffn_conv_b[:, None, :]
    ret_norm = row(ret_norm_g[0])

    cos_p, sin_p = _rope_tables(np.arange(seq))
    xp = x_prompt.reshape(bsz * seq, d)
    xp, pC, pn, pm, pST, p_conv = _even_prompt(xp, g_mix0, w_even_in, even_consts, w_even_out,
                                               seq_rows=seq, L=CHUNK, n_sub=2)
    xp, p_f0 = _ffn(xp, 0, g_ffn0, w_up, ffn_conv_w[0], ffn_cb[0], w_dn, None, seq_rows=seq, tt=512)
    xp, pSr = _odd_prompt(xp, g_mix1, w_odd_in, cos_p, sin_p, _retention_tables(CHUNK, 0) + (ret_norm,), w_odd_out,
                          seq_rows=seq, L=CHUNK, n_sub=2)
    _, p_f1, yp = _ffn(xp, 1, g_ffn1, w_up, ffn_conv_w[1], ffn_cb[1], w_dn, g_final, seq_rows=seq, tt=512)
    pS = pST.reshape(bsz, B_STATE, B_HEADS, B_HEADDIM).transpose(0, 2, 3, 1)

    xs = jnp.pad(x_sample, ((0, 0), (n_pad, 0), (0, 0))).reshape(dbsz * SEQ_PAD, d)
    n_seg = 8
    cos_s, sin_s = _rope_tables(np.tile(PAST_LEN + np.arange(SEQ_PAD) - n_pad, n_seg))
    hist = lambda a: jnp.pad(a, ((0, 0), (n_pad - a.shape[1], dseq), (0, 0))).reshape(dbsz * SEQ_PAD, a.shape[-1])
    state = (hist(state_ssd_conv[0]), state_mlstm_C[0],
             jnp.pad(state_mlstm_n[0], ((0, 0), (0, SUBLANES - A_HEADS), (0, 0))),
             _pad_lanes(state_mlstm_m[0])[:, None, :],
             state_ssd_h[0].reshape(dbsz, B_HEADS // 2, 2 * B_HEADDIM, B_STATE))
    to_tm = lambda a: a.reshape(dbsz, SEQ_PAD, d)[:, n_pad:, :].transpose(1, 0, 2).reshape(dseq * dbsz, d)
    from_tm = lambda a: jnp.pad(a.reshape(dseq, dbsz, d).transpose(1, 0, 2),
                                ((0, 0), (n_pad, 0), (0, 0))).reshape(dbsz * SEQ_PAD, d)
    tm = 512
    proj = _norm_proj(xs, g_mix0, w_even_in, EVEN_SPLITS, tm)
    y, sC, sn, sm, sS, s_conv = _even_sample(proj, state, even_consts, n_seg=n_seg, n_pad=n_pad)
    sS = sS.reshape(dbsz, B_HEADS, B_HEADDIM, B_STATE)
    xs = _proj_res(xs, y, w_even_out, tm)
    xt, s_f0 = _ffn_sample(to_tm(xs), state_ffn_conv, 0, g_ffn0, w_up, ffn_conv_w[0], ffn_cb[0], w_dn, None,
                           cw=D_FF // 2)
    xs = from_tm(xt)
    proj = _norm_proj(xs, g_mix1, w_odd_in, ODD_SPLITS, tm)
    y, sSr = _odd_sample(proj, state_ret_S[0], cos_s, sin_s,
                         _retention_tables(SEQ_PAD, n_pad, n_seg) + (ret_norm,), n_seg=n_seg)
    xs = _proj_res(xs, y, w_odd_out, tm)
    _, s_f1, yt = _ffn_sample(to_tm(xs), state_ffn_conv, 1, g_ffn1, w_up, ffn_conv_w[1], ffn_cb[1], w_dn,
                              g_final, cw=D_FF // 2)

    def conv_rows(tail, bs):
        return tail.reshape(bs, SUBLANES, tail.shape[-1])[:, SUBLANES - (B_CONV - 1):, :]

    def mlstm_states(C, n, m):
        return C[None], n[None, :, :A_HEADS, :], m[None, :, 0, :A_HEADS]

    p_ff = jnp.stack([p_f0, p_f1]).reshape(2, bsz, SUBLANES, 2 * D_FF)[:, :, SUBLANES - (FFN_CONV - 1):, :]
    s_ff = jnp.stack([s_f0, s_f1])
    p_states = mlstm_states(pC, pn, pm) + (conv_rows(p_conv, bsz)[None], pS[None], pSr[None], p_ff)
    s_states = mlstm_states(sC, sn, sm) + (conv_rows(s_conv, dbsz)[None], sS[None], sSr[None], s_ff)
    y_prompt = yp.reshape(bsz, seq, d)
    y_sample = yt.reshape(dseq, dbsz, d).transpose(1, 0, 2)
    return (y_prompt, y_sample) + p_states + s_states
```

```python
import functools

import numpy as np
import jax
import jax.numpy as jnp
from jax import lax
from jax.experimental import pallas as pl
from jax.experimental.pallas import tpu as pltpu

f32 = jnp.float32
bf16 = jnp.bfloat16

EPS = 1e-6
CHUNK = 128
D_MODEL = 1024
A_HEADS, A_DK, A_DV = 4, 128, 256
A_WIDTH = A_HEADS * A_DV
B_HEADS, B_HEADDIM, B_GROUPS, B_STATE, B_CONV = 16, 64, 2, 128, 4
B_WIDTH = B_HEADS * B_HEADDIM
B_CONV_DIM = B_WIDTH + 2 * B_GROUPS * B_STATE
B_HEADS_PER_GROUP = B_HEADS // B_GROUPS
B_GROUP_WIDTH = B_WIDTH // B_GROUPS
C_HEADS, C_DK, C_DV = 8, 128, 256
C_WIDTH = C_HEADS * C_DV
ROPE_BASE = 10000.0
D_FF = 2816
FFN_CONV = 3
PAST_LEN = 16384

LANES = 128
SUBLANES = 8
SEQ_PAD = SUBLANES
NEG_BIG = -1e30
VMEM_LIMIT = 56 * 1024 * 1024

PROMPT_CHUNKS_PER_HALF = 2
PROMPT_FFN_ROWS = 512
SAMPLE_SEQS_PER_STEP = 8
SAMPLE_ROW_TILE = 512
SAMPLE_FFN_COLS = D_FF // 2

EVEN_SPLITS = (A_HEADS * A_DK, A_HEADS * A_DK, A_WIDTH, A_WIDTH, B_WIDTH, B_CONV_DIM, 3 * LANES)
ODD_SPLITS = (C_HEADS * C_DK, C_HEADS * C_DK, C_WIDTH, C_WIDTH)


def _params(sem):
    return pltpu.CompilerParams(dimension_semantics=sem, vmem_limit_bytes=VMEM_LIMIT)


def _resident(shape, index_map):
    return pl.BlockSpec(shape, index_map, pipeline_mode=pl.Buffered(1))


def _nt(a, b):
    return lax.dot_general(a, b, (((1,), (1,)), ((), ())), preferred_element_type=f32)


def _tn(a, b):
    return lax.dot_general(a, b, (((0,), (0,)), ((), ())), preferred_element_type=f32)


def _nn(a, b):
    return jnp.dot(a, b, preferred_element_type=f32)


def _rmsnorm(x, g):
    return x * lax.rsqrt(jnp.mean(x * x, axis=-1, keepdims=True) + EPS) * g


def _layernorm_nogain(h):
    mu = jnp.mean(h, axis=-1, keepdims=True)
    hc = h - mu
    return hc * lax.rsqrt(jnp.mean(hc * hc, axis=-1, keepdims=True) + EPS)


def _softplus(x):
    return jnp.maximum(x, 0.0) + jnp.log1p(jnp.exp(-jnp.abs(x)))


def _silu(x):
    return x * jax.nn.sigmoid(x)


def _split3(x):
    hi = x.astype(bf16)
    r1 = x - hi.astype(f32)
    mid = r1.astype(bf16)
    lo = (r1 - mid.astype(f32)).astype(bf16)
    return hi, mid, lo


def _cumsum_rows(tril, x):
    hi, mid, lo = _split3(x)
    return _nn(tril, hi) + _nn(tril, mid) + _nn(tril, lo)


def _split_views(ref, splits):
    views, off = [], 0
    for n in splits:
        views.append(ref.at[:, off:off + n])
        off += n
    return views


def _project(h, w_ref, p_ref, splits):
    off = 0
    for n in splits:
        p_ref[:, off:off + n] = _nn(h, w_ref[:, off:off + n])
        off += n


MXU_COLS = 256


def _proj_pieces(h_ref, w_ref, dst_ref):
    def piece(c0, n):
        def run():
            dst_ref[:, c0:c0 + n] = _nn(h_ref[...], w_ref[:, c0:c0 + n])
        return run
    total = w_ref.shape[1]
    return [piece(c0, min(MXU_COLS, total - c0)) for c0 in range(0, total, MXU_COLS)]


def _out_pieces(x_ref, y_ref, w_ref, o_ref, rows):
    def piece(c0, n):
        def run():
            o_ref[rows, c0:c0 + n] = x_ref[rows, c0:c0 + n] + _nn(y_ref[...], w_ref[:, c0:c0 + n])
        return run
    total = w_ref.shape[1]
    return [piece(c0, min(MXU_COLS, total - c0)) for c0 in range(0, total, MXU_COLS)]


def _spread(work, n_slots):
    return [work[i * len(work) // n_slots:(i + 1) * len(work) // n_slots] for i in range(n_slots)]


def _norm_proj_body(x_ref, g_ref, w_ref, *o_refs, splits):
    h = _rmsnorm(x_ref[...], g_ref[...]).astype(bf16)
    off = 0
    for o_ref, n in zip(o_refs, splits):
        o_ref[...] = _nn(h, w_ref[:, off:off + n])
        off += n


def _norm_proj(x, g, w, splits, tm):
    rows, d = x.shape
    return pl.pallas_call(
        functools.partial(_norm_proj_body, splits=splits),
        grid=(rows // tm,),
        in_specs=[pl.BlockSpec((tm, d), lambda i: (i, 0)),
                  pl.BlockSpec((1, d), lambda i: (0, 0)),
                  _resident(w.shape, lambda i: (0, 0))],
        out_specs=[pl.BlockSpec((tm, n), lambda i: (i, 0)) for n in splits],
        out_shape=[jax.ShapeDtypeStruct((rows, n), f32) for n in splits],
        compiler_params=_params(("arbitrary",)),
        name="norm_proj",
    )(x, g, w)


def _proj_res_body(x_ref, y_ref, w_ref, o_ref):
    o_ref[...] = x_ref[...] + _nn(y_ref[...].astype(bf16), w_ref[...])


def _proj_res(x, y, w, tm):
    rows, d = x.shape
    k = y.shape[1]
    return pl.pallas_call(
        _proj_res_body,
        grid=(rows // tm,),
        in_specs=[pl.BlockSpec((tm, d), lambda i: (i, 0)),
                  pl.BlockSpec((tm, k), lambda i: (i, 0)),
                  _resident(w.shape, lambda i: (0, 0))],
        out_specs=pl.BlockSpec((tm, d), lambda i: (i, 0)),
        out_shape=jax.ShapeDtypeStruct((rows, d), f32),
        compiler_params=_params(("arbitrary",)),
        name="proj_res",
    )(x, y, w)


def _conv3(bias, prev2, prev1, cur, w_ref):
    y = bias + prev2 * w_ref[0:1, :]
    y = y + prev1 * w_ref[1:2, :]
    return y + cur * w_ref[2:3, :]


def _ffn_body(*refs, tt, final):
    it = iter(refs)
    x_ref, g_ref, wup_ref, cw_ref, cb_ref, wdn_ref = (next(it) for _ in range(6))
    gf_ref = next(it) if final else None
    o_ref, tail_ref = next(it), next(it)
    yn_ref = next(it) if final else None
    scr, carry = next(it), next(it)

    @pl.when(pl.program_id(1) == 0)
    def _():
        carry[...] = jnp.zeros(carry.shape, f32)

    x = x_ref[...]
    h = _rmsnorm(x, g_ref[...]).astype(bf16)
    conv = []
    for part in range(2):
        cols = slice(part * D_FF, (part + 1) * D_FF)
        u = _nn(h, wup_ref[:, cols])
        scr[0:SUBLANES, :] = carry[:, cols]
        tail_ref[:, cols] = u[tt - SUBLANES:tt, :]
        carry[:, cols] = u[tt - SUBLANES:tt, :]
        scr[SUBLANES:SUBLANES + tt, :] = u
        conv.append(_conv3(cb_ref[:, cols], scr[SUBLANES - 2:SUBLANES - 2 + tt, :],
                           scr[SUBLANES - 1:SUBLANES - 1 + tt, :], u, cw_ref.at[:, cols]))
    act = (_silu(conv[0]) * conv[1]).astype(bf16)
    out = x + _nn(act, wdn_ref[...])
    o_ref[...] = out
    if final:
        yn_ref[...] = _rmsnorm(out, gf_ref[...])


def _ffn(x, layer, g, wup, conv_w, conv_b, wdn, g_final, *, seq_rows, tt):
    rows, d = x.shape
    final = g_final is not None
    n_seq = rows // seq_rows
    n_inner = seq_rows // tt
    rmap = lambda s, t: (s * n_inner + t, 0)
    cmap = lambda s, t: (0, 0)
    lmap = lambda s, t: (layer, 0, 0)
    in_specs = [pl.BlockSpec((tt, d), rmap), pl.BlockSpec((1, d), cmap), _resident((None,) + wup.shape[1:], lmap),
                pl.BlockSpec(conv_w.shape, cmap), pl.BlockSpec(conv_b.shape, cmap),
                _resident((None,) + wdn.shape[1:], lmap)]
    args = [x, g, wup, conv_w, conv_b, wdn]
    out_specs = [pl.BlockSpec((tt, d), rmap), pl.BlockSpec((SUBLANES, 2 * D_FF), lambda s, t: (s, 0))]
    out_shape = [jax.ShapeDtypeStruct((rows, d), f32), jax.ShapeDtypeStruct((n_seq * SUBLANES, 2 * D_FF), f32)]
    if final:
        in_specs.append(pl.BlockSpec((1, d), cmap))
        args.append(g_final)
        out_specs.append(pl.BlockSpec((tt, d), rmap))
        out_shape.append(jax.ShapeDtypeStruct((rows, d), f32))
    return pl.pallas_call(
        functools.partial(_ffn_body, tt=tt, final=final),
        grid=(n_seq, n_inner), in_specs=in_specs, out_specs=out_specs, out_shape=out_shape,
        scratch_shapes=[pltpu.VMEM((SUBLANES + tt, D_FF), f32), pltpu.VMEM((SUBLANES, 2 * D_FF), f32)],
        compiler_params=_params(("arbitrary", "arbitrary")),
        name="ffn_prompt",
    )(*args)


def _ffn_sample_body(*refs, nb, nj, final):
    it = iter(refs)
    x_ref, hist_ref, g_ref, wup_ref, cw_ref, cb_ref, wdn_ref = (next(it) for _ in range(7))
    gf_ref = next(it) if final else None
    o_ref, tail_ref = next(it), next(it)
    yn_ref = next(it) if final else None
    h_scr, convg_scr, acc_scr = next(it), next(it), next(it)
    c = pl.program_id(0)
    tt = x_ref.shape[0]

    @pl.when(c == 0)
    def _():
        h_scr[...] = _rmsnorm(x_ref[...], g_ref[...]).astype(bf16)
        acc_scr[...] = jnp.zeros(acc_scr.shape, f32)

    u = _nn(h_scr[...], wup_ref[...])
    ext = jnp.concatenate([hist_ref[:, k, :] for k in range(FFN_CONV - 1)] + [u], axis=0)
    for k in range(FFN_CONV - 1):
        tail_ref[:, k, :] = ext[tt + k * nb:tt + (k + 1) * nb, :]
    y = _conv3(cb_ref[...], ext[0:tt, :], ext[nb:nb + tt, :], u, cw_ref)

    @pl.when(c < nj)
    def _():
        convg_scr[c] = y

    @pl.when(c >= nj)
    def _():
        act = (_silu(convg_scr[c - nj]) * y).astype(bf16)
        acc_scr[...] += _nn(act, wdn_ref[...])

    @pl.when(c == 2 * nj - 1)
    def _():
        out = x_ref[...] + acc_scr[...]
        o_ref[...] = out
        if final:
            yn_ref[...] = _rmsnorm(out, gf_ref[...])


def _ffn_sample(x, hist, layer, g, wup, conv_w, conv_b, wdn, g_final, *, cw):
    rows, d = x.shape
    nb = hist.shape[1]
    nj = D_FF // cw
    final = g_final is not None
    fixed = lambda c: (0, 0)
    colblk = lambda c: (0, c)
    in_specs = [pl.BlockSpec((rows, d), fixed),
                pl.BlockSpec((None, nb, FFN_CONV - 1, cw), lambda c: (layer, 0, 0, c)),
                pl.BlockSpec((1, d), fixed), pl.BlockSpec((None, d, cw), lambda c: (layer, 0, c)),
                pl.BlockSpec((FFN_CONV, cw), colblk), pl.BlockSpec((1, cw), colblk),
                pl.BlockSpec((None, cw, d), lambda c: (layer, jnp.maximum(c - nj, 0), 0))]
    args = [x, hist, g, wup, conv_w, conv_b, wdn]
    out_specs = [pl.BlockSpec((rows, d), fixed), pl.BlockSpec((nb, FFN_CONV - 1, cw), lambda c: (0, 0, c))]
    out_shape = [jax.ShapeDtypeStruct((rows, d), f32), jax.ShapeDtypeStruct((nb, FFN_CONV - 1, 2 * D_FF), f32)]
    if final:
        in_specs.append(pl.BlockSpec((1, d), fixed))
        args.append(g_final)
        out_specs.append(pl.BlockSpec((rows, d), fixed))
        out_shape.append(jax.ShapeDtypeStruct((rows, d), f32))
    return pl.pallas_call(
        functools.partial(_ffn_sample_body, nb=nb, nj=nj, final=final),
        grid=(2 * nj,), in_specs=in_specs, out_specs=out_specs, out_shape=out_shape,
        scratch_shapes=[pltpu.VMEM((rows, d), bf16), pltpu.VMEM((nj, rows, cw), f32), pltpu.VMEM((rows, d), f32)],
        compiler_params=_params(("arbitrary",)),
        name="ffn_sample",
    )(*args)


EVEN_SLOTS = A_HEADS + B_HEADS // 2


def _even_chunk(r0, slot, L, n_pad, n_seg, proj_refs, hist_ref, const_refs, y_ref, state_in_refs, state_refs,
                tail_ref, conv_scr, yb_scr, side=None):
    q_ref, k_ref, v_ref, og_ref, z_ref, xbc_ref, gt_ref = proj_refs
    gb_ref, anorm_ref, convw_ref, convb_ref, alog_ref, dx_ref, bnorm_ref = const_refs
    C_in, n_in, m_in, S_in = state_in_refs
    C_ref, n_ref, m_ref, S_ref = state_refs
    rows = pl.ds(r0, L)
    seg = L // n_seg
    seg_rows = [slice(i * seg, (i + 1) * seg) for i in range(n_seg)]
    seg_last = [slice((i + 1) * seg - 1, (i + 1) * seg) for i in range(n_seg)]

    def per_row(vals):
        if n_seg == 1:
            return vals[0]
        return jnp.concatenate([jnp.broadcast_to(v, (seg, v.shape[1])) for v in vals], axis=0)

    ri = lax.broadcasted_iota(jnp.int32, (L, L), 0)
    ci = lax.broadcasted_iota(jnp.int32, (L, L), 1)
    causal = ri >= ci
    if n_seg > 1:
        causal = causal & ((ri // seg) == (ci // seg))
    tril = causal.astype(bf16)
    valid = None
    if n_pad:
        valid = (lax.broadcasted_iota(jnp.int32, (L, 1), 0) % seg) >= n_pad

    gates = gt_ref[rows, :]
    li = gates[:, 0:LANES] + gb_ref[0:1, :]
    fpre = gates[:, LANES:2 * LANES] + gb_ref[1:2, :]
    lf = -_softplus(-fpre)
    dt = _softplus(gates[:, 2 * LANES:3 * LANES] + gb_ref[2:3, :])
    if n_pad:
        li = jnp.where(valid, li, NEG_BIG)
        lf = jnp.where(valid, lf, 0.0)
        dt = jnp.where(valid, dt, 0.0)
    a = dt * (-jnp.exp(alog_ref[...]))
    cums = _cumsum_rows(tril, jnp.concatenate([lf, a], axis=1))
    bcum = cums[:, 0:LANES]
    acum = cums[:, LANES:2 * LANES]
    liT, bT, aT, dtT = li.T, bcum.T, acum.T, dt.T

    m_old = [m_in[slot + i] for i in range(n_seg)]
    b_lasts = [bcum[r, :] for r in seg_last]
    m_rows, b_last = per_row(m_old), per_row(b_lasts)
    log_g = b_last - bcum + li
    m_news = [jnp.maximum(b_lasts[i] + m_old[i], jnp.max(log_g[seg_rows[i], :], axis=0, keepdims=True))
              for i in range(n_seg)]
    m_new = per_row(m_news)
    gfac = jnp.exp(log_g - m_new)
    cdecay = jnp.exp(b_last + m_rows - m_new)
    for i in range(n_seg):
        m_ref[slot + i] = m_news[i]
    for h in range(A_HEADS):
        for run in (side[h] if side else ()):
            run()
        bcol, brow, lirow = bcum[:, h:h + 1], bT[h:h + 1, :], liT[h:h + 1, :]
        m_h = m_rows[:, h:h + 1]
        logw = jnp.where(causal, bcol - brow + lirow, -jnp.inf)
        log_prev = bcol + m_h
        m_t = jnp.maximum(log_prev, jnp.max(logw, axis=-1, keepdims=True))
        w_in = jnp.exp(logw - m_t)
        w_prev = jnp.exp(log_prev - m_t)
        qh = q_ref[rows, h * A_DK:(h + 1) * A_DK]
        kh = k_ref[rows, h * A_DK:(h + 1) * A_DK] * (A_DK ** -0.5)
        vh = v_ref[rows, h * A_DV:(h + 1) * A_DV]
        vb = vh.astype(bf16)
        qb = qh.astype(bf16)
        s = _nt(qb, kh.astype(bf16)) * w_in
        kg = kh * gfac[:, h:h + 1]
        Cs = [C_in[slot + i, h] for i in range(n_seg)]
        ns = [n_in[slot + i, h:h + 1, :] for i in range(n_seg)]
        if n_seg == 1:
            inter = _nn(qb, Cs[0].astype(bf16))
            dec_h = cdecay[:, h:h + 1]
            C_ref[slot, h] = Cs[0] * dec_h + _tn(kg.astype(bf16), vb)
            n_ref[slot, h:h + 1, :] = ns[0] * dec_h + jnp.sum(kg, axis=0, keepdims=True)
        else:
            inter = jnp.concatenate([_nn(qh[seg_rows[i], :].astype(bf16), Cs[i].astype(bf16))
                                     for i in range(n_seg)], axis=0)
            for i in range(n_seg):
                rs = seg_rows[i]
                dec_h = cdecay[seg_last[i], h:h + 1]
                C_ref[slot + i, h] = Cs[i] * dec_h + _tn(kg[rs, :].astype(bf16), vh[rs, :].astype(bf16))
                n_ref[slot + i, h:h + 1, :] = ns[i] * dec_h + jnp.sum(kg[rs, :], axis=0, keepdims=True)
        num = _nn(s.astype(bf16), vb) + inter * w_prev
        den = jnp.sum(s, axis=-1, keepdims=True) + jnp.sum(qh * per_row(ns), axis=-1, keepdims=True) * w_prev
        hout = num / jnp.maximum(jnp.abs(den), jnp.exp(-m_t))
        ya = _layernorm_nogain(hout) * anorm_ref[:, h * A_DV:(h + 1) * A_DV]
        ya = ya * jax.nn.sigmoid(og_ref[rows, h * A_DV:(h + 1) * A_DV])
        y_ref[rows, h * A_DV:(h + 1) * A_DV] = ya.astype(y_ref.dtype)

    xraw = xbc_ref[rows, :]
    if hist_ref is not None:
        seg_row = lax.broadcasted_iota(jnp.int32, (L, 1), 0) % seg
        is_hist = (seg_row >= n_pad - (B_CONV - 1)) & jnp.logical_not(valid)
        xraw = jnp.where(is_hist, hist_ref[rows, :], xraw)
    conv_scr[SUBLANES:SUBLANES + L, :] = xraw
    xc = convb_ref[...] + conv_scr[SUBLANES - 3:SUBLANES - 3 + L, :] * convw_ref[0:1, :]
    xc = xc + conv_scr[SUBLANES - 2:SUBLANES - 2 + L, :] * convw_ref[1:2, :]
    xc = xc + conv_scr[SUBLANES - 1:SUBLANES - 1 + L, :] * convw_ref[2:3, :]
    xc = xc + xraw * convw_ref[3:4, :]
    if n_seg == 1:
        new_tail = conv_scr[L:L + SUBLANES, :]
        tail_ref[...] = new_tail
        conv_scr[0:SUBLANES, :] = new_tail
    else:
        tail_ref[...] = xraw
    xc = _silu(xc)

    a_last = per_row([acum[r, :] for r in seg_last])
    wtile = jnp.exp(a_last - acum) * dt
    expa = jnp.exp(acum)

    def head_matrix(cb, h):
        acol, arow, dtrow = acum[:, h:h + 1], aT[h:h + 1, :], dtT[h:h + 1, :]
        decay = jnp.where(causal, jnp.exp(jnp.where(causal, acol - arow, 0.0)), 0.0)
        return (cb * decay * dtrow).astype(bf16)

    def group_bc(g):
        Bg = xc[:, B_WIDTH + g * B_STATE:B_WIDTH + (g + 1) * B_STATE].astype(bf16)
        c0 = B_WIDTH + B_GROUPS * B_STATE + g * B_STATE
        Cg = xc[:, c0:c0 + B_STATE].astype(bf16)
        return Bg, Cg, _nt(Cg, Bg)

    low_half = lax.broadcasted_iota(jnp.int32, (L, LANES), 1) < B_HEADDIM
    first_head_rows = lax.broadcasted_iota(jnp.int32, (2 * B_HEADDIM, 1), 0) < B_HEADDIM

    def pair_lanes(t, j):
        return jnp.where(low_half, jnp.broadcast_to(t[:, 2 * j:2 * j + 1], (L, LANES)),
                         jnp.broadcast_to(t[:, 2 * j + 1:2 * j + 2], (L, LANES)))

    pairs_per_group = B_HEADS_PER_GROUP // 2
    for g in range(B_GROUPS):
        Bg, Cg, cb = group_bc(g)
        gs = slice(g * B_GROUP_WIDTH, (g + 1) * B_GROUP_WIDTH)
        if n_seg == 1:
            ST = S_in[slot, :, gs]
            inter_g = _nn(Cg, ST.astype(bf16))
        else:
            Bf = xc[:, B_WIDTH + g * B_STATE:B_WIDTH + (g + 1) * B_STATE]
            c0 = B_WIDTH + B_GROUPS * B_STATE + g * B_STATE
            Cf = xc[:, c0:c0 + B_STATE]
        xw, decs = [], []
        for jp in range(pairs_per_group):
            j = g * pairs_per_group + jp
            for run in (side[A_HEADS + j] if side else ()):
                run()
            ps = slice(j * LANES, (j + 1) * LANES)
            e_pair, w_pair = pair_lanes(expa, j), pair_lanes(wtile, j)
            mcat = jnp.concatenate([head_matrix(cb, 2 * j), head_matrix(cb, 2 * j + 1)], axis=1)
            xp = xc[:, ps]
            xbd = jnp.concatenate([jnp.where(low_half, xp, 0.0).astype(bf16),
                                   jnp.where(low_half, 0.0, xp).astype(bf16)], axis=0)
            if n_seg == 1:
                inter = inter_g[:, jp * LANES:(jp + 1) * LANES]
                xw.append((xp * w_pair).astype(bf16))
                decs.append(e_pair[L - 1:L, :])
            else:
                xwf = xp * w_pair
                parts = []
                for i in range(n_seg):
                    rs = seg_rows[i]
                    Sp = S_in[slot + i, j]
                    parts.append(_nt(Cf[rs, :].astype(bf16), Sp.astype(bf16)))
                    e_last = e_pair[seg_last[i], :]
                    dec_col = jnp.where(first_head_rows, e_last[:, 0:1], e_last[:, B_HEADDIM:B_HEADDIM + 1])
                    S_ref[slot + i, j] = Sp * dec_col + _tn(xwf[rs, :].astype(bf16), Bf[rs, :].astype(bf16))
                inter = jnp.concatenate(parts, axis=0)
            yb_scr[:, ps] = (_nn(mcat, xbd) + inter * e_pair) + dx_ref[:, ps] * xp
        if n_seg == 1:
            S_ref[slot, :, gs] = ST * jnp.concatenate(decs, axis=1) + _tn(Bg, jnp.concatenate(xw, axis=1))
    yb = yb_scr[...] * _silu(z_ref[rows, :])
    for g in range(B_GROUPS):
        gs = slice(g * B_GROUP_WIDTH, (g + 1) * B_GROUP_WIDTH)
        yg = yb[:, gs]
        yg = yg * lax.rsqrt(jnp.mean(yg * yg, axis=-1, keepdims=True) + EPS)
        y_ref[rows, A_WIDTH + g * B_GROUP_WIDTH:A_WIDTH + (g + 1) * B_GROUP_WIDTH] = (
            yg * bnorm_ref[:, gs]).astype(y_ref.dtype)


N_EVEN_CONSTS = 7


def _even_prompt_body(*refs, L, n_sub):
    it = iter(refs)
    x_ref, xnext_ref, gmix_ref, win_ref = next(it), next(it), next(it), next(it)
    const_refs = tuple(next(it) for _ in range(N_EVEN_CONSTS))
    wout_ref = next(it)
    o_ref = next(it)
    state_refs = tuple(next(it) for _ in range(4))
    tail_ref = next(it)
    p_scr, y_scr = (next(it), next(it)), (next(it), next(it))
    h_scr, conv_scr, yb_scr = (next(it) for _ in range(3))
    th = L * n_sub
    gmix = gmix_ref[...]

    @pl.when((pl.program_id(0) == 0) & (pl.program_id(1) == 0))
    def _():
        _project(_rmsnorm(x_ref[0:th, :], gmix).astype(bf16), win_ref, p_scr[0], EVEN_SPLITS)

    @pl.when(pl.program_id(1) == 0)
    def _():
        conv_scr[0:SUBLANES, :] = jnp.zeros((SUBLANES, B_CONV_DIM), f32)
        for r in state_refs:
            r[...] = jnp.zeros(r.shape, f32)

    for half in range(2):
        ahead = xnext_ref[...] if half else x_ref[th:2 * th, :]
        h_scr[...] = _rmsnorm(ahead, gmix).astype(bf16)
        work = _proj_pieces(h_scr, win_ref, p_scr[1 - half])
        if half:
            work = work + _out_pieces(x_ref, y_scr[0], wout_ref, o_ref, slice(0, th))
        side = _spread(work, n_sub * EVEN_SLOTS)
        proj_refs = _split_views(p_scr[half], EVEN_SPLITS)
        for c in range(n_sub):
            _even_chunk(c * L, 0, L, 0, 1, proj_refs, None, const_refs, y_scr[half], state_refs, state_refs,
                        tail_ref, conv_scr, yb_scr, side[c * EVEN_SLOTS:(c + 1) * EVEN_SLOTS])
    for run in _out_pieces(x_ref, y_scr[1], wout_ref, o_ref, slice(th, 2 * th)):
        run()


def _even_prompt(x, gmix, win, consts, wout, *, seq_rows, L, n_sub):
    rows, d = x.shape
    n_seq = rows // seq_rows
    th = L * n_sub
    tt = 2 * th
    n_inner = seq_rows // tt
    n_half = rows // th
    rmap = lambda s, t: (s * n_inner + t, 0)
    nextmap = lambda s, t: (jnp.minimum(2 * (s * n_inner + t) + 2, n_half - 1), 0)
    cmap = lambda s, t: (0, 0)
    smap4 = lambda s, t: (s, 0, 0, 0)
    smap3 = lambda s, t: (s, 0, 0)
    st_specs = [pl.BlockSpec((1, A_HEADS, A_DK, A_DV), smap4), pl.BlockSpec((1, SUBLANES, A_DK), smap3),
                pl.BlockSpec((1, 1, LANES), smap3), pl.BlockSpec((1, B_STATE, B_WIDTH), smap3)]
    st_shapes = [jax.ShapeDtypeStruct((n_seq, A_HEADS, A_DK, A_DV), f32),
                 jax.ShapeDtypeStruct((n_seq, SUBLANES, A_DK), f32),
                 jax.ShapeDtypeStruct((n_seq, 1, LANES), f32),
                 jax.ShapeDtypeStruct((n_seq, B_STATE, B_WIDTH), f32)]
    in_specs = ([pl.BlockSpec((tt, d), rmap), pl.BlockSpec((th, d), nextmap), pl.BlockSpec((1, d), cmap),
                 _resident(win.shape, cmap)]
                + [pl.BlockSpec(c.shape, cmap) for c in consts] + [_resident(wout.shape, cmap)])
    p_shape = pltpu.VMEM((th, sum(EVEN_SPLITS)), f32)
    y_shape = pltpu.VMEM((th, A_WIDTH + B_WIDTH), bf16)
    return pl.pallas_call(
        functools.partial(_even_prompt_body, L=L, n_sub=n_sub),
        grid=(n_seq, n_inner), in_specs=in_specs,
        out_specs=[pl.BlockSpec((tt, d), rmap)] + st_specs + [pl.BlockSpec((SUBLANES, B_CONV_DIM), lambda s, t: (s, 0))],
        out_shape=[jax.ShapeDtypeStruct((rows, d), f32)] + st_shapes
        + [jax.ShapeDtypeStruct((n_seq * SUBLANES, B_CONV_DIM), f32)],
        scratch_shapes=[p_shape, p_shape, y_shape, y_shape, pltpu.VMEM((th, d), bf16),
                        pltpu.VMEM((SUBLANES + L, B_CONV_DIM), f32), pltpu.VMEM((L, B_WIDTH), f32)],
        compiler_params=_params(("arbitrary", "arbitrary")),
        name="even_prompt",
    )(x, x, gmix, win, *consts, wout)


def _even_sample_body(*refs, L, n_seg, n_pad):
    it = iter(refs)
    proj_refs = tuple(next(it) for _ in range(7))
    hist_ref = next(it)
    state_in_refs = tuple(next(it) for _ in range(4))
    const_refs = tuple(next(it) for _ in range(N_EVEN_CONSTS))
    y_ref = next(it)
    state_refs = tuple(next(it) for _ in range(4))
    tail_ref = next(it)
    conv_scr, yb_scr = next(it), next(it)

    conv_scr[0:SUBLANES, :] = jnp.zeros((SUBLANES, B_CONV_DIM), f32)
    n_ref = state_refs[1]
    n_ref[:, A_HEADS:, :] = jnp.zeros((n_seg, SUBLANES - A_HEADS, A_DK), f32)
    _even_chunk(0, 0, L, n_pad, n_seg, proj_refs, hist_ref, const_refs, y_ref, state_in_refs, state_refs,
                tail_ref, conv_scr, yb_scr)


def _even_sample(proj, state, consts, *, n_seg, n_pad):
    rows = proj[0].shape[0]
    L = SEQ_PAD * n_seg
    rmap = lambda o: (o, 0)
    cmap = lambda o: (0, 0)
    smap4 = lambda o: (o, 0, 0, 0)
    smap3 = lambda o: (o, 0, 0)
    st_specs = [pl.BlockSpec((n_seg, A_HEADS, A_DK, A_DV), smap4), pl.BlockSpec((n_seg, SUBLANES, A_DK), smap3),
                pl.BlockSpec((n_seg, 1, LANES), smap3),
                pl.BlockSpec((n_seg, B_HEADS // 2, 2 * B_HEADDIM, B_STATE), smap4)]
    st_shapes = [jax.ShapeDtypeStruct(s.shape, f32) for s in state[1:]]
    in_specs = ([pl.BlockSpec((L, a.shape[1]), rmap) for a in proj] + [pl.BlockSpec((L, B_CONV_DIM), rmap)]
                + st_specs + [pl.BlockSpec(c.shape, cmap) for c in consts])
    return pl.pallas_call(
        functools.partial(_even_sample_body, L=L, n_seg=n_seg, n_pad=n_pad),
        grid=(rows // L,), in_specs=in_specs,
        out_specs=[pl.BlockSpec((L, A_WIDTH + B_WIDTH), rmap)] + st_specs + [pl.BlockSpec((L, B_CONV_DIM), rmap)],
        out_shape=[jax.ShapeDtypeStruct((rows, A_WIDTH + B_WIDTH), f32)] + st_shapes
        + [jax.ShapeDtypeStruct((rows, B_CONV_DIM), f32)],
        scratch_shapes=[pltpu.VMEM((SUBLANES + L, B_CONV_DIM), f32), pltpu.VMEM((L, B_WIDTH), f32)],
        compiler_params=_params(("arbitrary",)),
        name="even_sample",
    )(*proj, *state, *consts)


def _odd_chunk(r0, rope_r0, slot, L, n_seg, proj_refs, const_refs, y_ref, S_in, S_ref, side=None):
    q_ref, k_ref, v_ref, g_ref = proj_refs
    cos_ref, sin_ref, intra_ref, cross_ref, into_ref, cdec_ref, norm_ref = const_refs
    rows = pl.ds(r0, L)
    seg = L // n_seg
    cosf = cos_ref[pl.ds(rope_r0, L), :]
    sinf = sin_ref[pl.ds(rope_r0, L), :]
    cross = cross_ref[...]
    into = into_ref[...]
    cdec = cdec_ref[...]
    for h in range(C_HEADS):
        for run in (side[h] if side else ()):
            run()
        qh = q_ref[rows, h * C_DK:(h + 1) * C_DK]
        kh = k_ref[rows, h * C_DK:(h + 1) * C_DK]
        qh = qh * cosf + pltpu.roll(qh, C_DK // 2, 1) * sinf
        kh = (kh * cosf + pltpu.roll(kh, C_DK // 2, 1) * sinf) * (C_DK ** -0.5)
        vh = v_ref[rows, h * C_DV:(h + 1) * C_DV]
        vb = vh.astype(bf16)
        qb = qh.astype(bf16)
        s = _nt(qb, kh.astype(bf16)) * intra_ref[h]
        kd = kh * into[:, h:h + 1]
        if n_seg == 1:
            S = S_in[slot, h]
            inter = _nn(qb, S.astype(bf16))
            S_ref[slot, h] = S * cdec[:, h:h + 1] + _tn(kd.astype(bf16), vb)
        else:
            parts = []
            for i in range(n_seg):
                rs = slice(i * seg, (i + 1) * seg)
                S = S_in[slot + i, h]
                parts.append(_nn(qh[rs, :].astype(bf16), S.astype(bf16)))
                S_ref[slot + i, h] = S * cdec[:, h:h + 1] + _tn(kd[rs, :].astype(bf16), vh[rs, :].astype(bf16))
            inter = jnp.concatenate(parts, axis=0)
        o = _nn(s.astype(bf16), vb) + inter * cross[:, h:h + 1]
        y = _layernorm_nogain(o) * norm_ref[:, h * C_DV:(h + 1) * C_DV]
        y = y * _silu(g_ref[rows, h * C_DV:(h + 1) * C_DV])
        y_ref[rows, h * C_DV:(h + 1) * C_DV] = y.astype(y_ref.dtype)


N_ODD_CONSTS = 7


def _odd_prompt_body(*refs, L, n_sub):
    it = iter(refs)
    x_ref, xnext_ref, gmix_ref, win_ref = next(it), next(it), next(it), next(it)
    const_refs = tuple(next(it) for _ in range(N_ODD_CONSTS))
    wout_ref = next(it)
    o_ref, S_ref = next(it), next(it)
    p_scr, y_scr = (next(it), next(it)), (next(it), next(it))
    h_scr = next(it)
    th = L * n_sub
    gmix = gmix_ref[...]

    @pl.when((pl.program_id(0) == 0) & (pl.program_id(1) == 0))
    def _():
        _project(_rmsnorm(x_ref[0:th, :], gmix).astype(bf16), win_ref, p_scr[0], ODD_SPLITS)

    @pl.when(pl.program_id(1) == 0)
    def _():
        S_ref[...] = jnp.zeros(S_ref.shape, f32)

    for half in range(2):
        ahead = xnext_ref[...] if half else x_ref[th:2 * th, :]
        h_scr[...] = _rmsnorm(ahead, gmix).astype(bf16)
        work = _proj_pieces(h_scr, win_ref, p_scr[1 - half])
        if half:
            work = work + _out_pieces(x_ref, y_scr[0], wout_ref, o_ref, slice(0, th))
        side = _spread(work, n_sub * C_HEADS)
        proj_refs = _split_views(p_scr[half], ODD_SPLITS)
        for c in range(n_sub):
            _odd_chunk(c * L, half * th + c * L, 0, L, 1, proj_refs, const_refs, y_scr[half], S_ref, S_ref,
                       side[c * C_HEADS:(c + 1) * C_HEADS])
    for run in _out_pieces(x_ref, y_scr[1], wout_ref, o_ref, slice(th, 2 * th)):
        run()


def _odd_prompt(x, gmix, win, cos, sin, consts, wout, *, seq_rows, L, n_sub):
    rows, d = x.shape
    n_seq = rows // seq_rows
    th = L * n_sub
    tt = 2 * th
    n_inner = seq_rows // tt
    n_half = rows // th
    rmap = lambda s, t: (s * n_inner + t, 0)
    nextmap = lambda s, t: (jnp.minimum(2 * (s * n_inner + t) + 2, n_half - 1), 0)
    cmap = lambda s, t: (0, 0)
    st_spec = pl.BlockSpec((1, C_HEADS, C_DK, C_DV), lambda s, t: (s, 0, 0, 0))
    rope_spec = pl.BlockSpec((tt, LANES), lambda s, t: (t, 0))
    in_specs = [pl.BlockSpec((tt, d), rmap), pl.BlockSpec((th, d), nextmap), pl.BlockSpec((1, d), cmap),
                _resident(win.shape, cmap), rope_spec, rope_spec]
    for c in consts:
        in_specs.append(pl.BlockSpec(c.shape, (lambda s, t: (0, 0, 0)) if c.ndim == 3 else cmap))
    in_specs.append(_resident(wout.shape, cmap))
    p_shape = pltpu.VMEM((th, sum(ODD_SPLITS)), f32)
    y_shape = pltpu.VMEM((th, C_WIDTH), bf16)
    return pl.pallas_call(
        functools.partial(_odd_prompt_body, L=L, n_sub=n_sub),
        grid=(n_seq, n_inner), in_specs=in_specs,
        out_specs=[pl.BlockSpec((tt, d), rmap), st_spec],
        out_shape=[jax.ShapeDtypeStruct((rows, d), f32), jax.ShapeDtypeStruct((n_seq, C_HEADS, C_DK, C_DV), f32)],
        scratch_shapes=[p_shape, p_shape, y_shape, y_shape, pltpu.VMEM((th, d), bf16)],
        compiler_params=_params(("arbitrary", "arbitrary")),
        name="odd_prompt",
    )(x, x, gmix, win, cos, sin, *consts, wout)


def _odd_sample_body(*refs, L, n_seg):
    it = iter(refs)
    proj_refs = tuple(next(it) for _ in range(4))
    S_in = next(it)
    const_refs = tuple(next(it) for _ in range(N_ODD_CONSTS))
    y_ref, S_ref = next(it), next(it)
    _odd_chunk(0, 0, 0, L, n_seg, proj_refs, const_refs, y_ref, S_in, S_ref)


def _odd_sample(proj, S0, cos, sin, consts, *, n_seg):
    rows = proj[0].shape[0]
    L = SEQ_PAD * n_seg
    rmap = lambda o: (o, 0)
    cmap = lambda o: (0, 0)
    st_spec = pl.BlockSpec((n_seg, C_HEADS, C_DK, C_DV), lambda o: (o, 0, 0, 0))
    in_specs = [pl.BlockSpec((L, a.shape[1]), rmap) for a in proj] + [st_spec]
    for c in (cos, sin) + tuple(consts):
        in_specs.append(pl.BlockSpec(c.shape, (lambda o: (0, 0, 0)) if c.ndim == 3 else cmap))
    return pl.pallas_call(
        functools.partial(_odd_sample_body, L=L, n_seg=n_seg),
        grid=(rows // L,), in_specs=in_specs,
        out_specs=[pl.BlockSpec((L, C_WIDTH), rmap), st_spec],
        out_shape=[jax.ShapeDtypeStruct((rows, C_WIDTH), f32), jax.ShapeDtypeStruct(S0.shape, f32)],
        compiler_params=_params(("arbitrary",)),
        name="odd_sample",
    )(*proj, S0, cos, sin, *consts)


def _pad_lanes(a, width=LANES):
    return jnp.pad(a, [(0, 0)] * (a.ndim - 1) + [(0, width - a.shape[-1])])


def _even_w_in_cols(w):
    sizes = [A_HEADS * A_DK, A_HEADS * A_DK, A_WIDTH, A_HEADS, A_HEADS, A_WIDTH, B_WIDTH, B_CONV_DIM, B_HEADS]
    q, k, v, ig, fg, og, z, xbc, dt = jnp.split(w, np.cumsum(sizes)[:-1].tolist(), axis=-1)
    return jnp.concatenate([q, k, v, og, z, xbc, _pad_lanes(ig), _pad_lanes(fg), _pad_lanes(dt)], axis=-1).astype(bf16)


def _retention_tables(seg, n_pad, n_seg=1):
    log_gamma = np.log1p(-np.exp2(-5.0 - np.arange(C_HEADS, dtype=np.float64)))
    t_real = seg - n_pad
    idx = np.arange(seg, dtype=np.float64) - n_pad
    real = idx >= 0
    diff = idx[:, None] - idx[None, :]
    intra = np.where((diff >= 0) & real[None, :], np.exp(log_gamma[:, None, None] * np.maximum(diff, 0.0)), 0.0)
    intra = np.stack([np.kron(np.eye(n_seg), intra[h]) for h in range(C_HEADS)])
    cross = np.where(real[:, None], np.exp(log_gamma[None, :] * (idx[:, None] + 1.0)), 0.0)
    into = np.where(real[:, None], np.exp(log_gamma[None, :] * (t_real - 1.0 - idx[:, None])), 0.0)
    cdec = np.exp(log_gamma * t_real)[None, :]
    lanes = lambda a: np.pad(a, ((0, 0), (0, LANES - a.shape[1]))).astype(np.float32)
    return intra.astype(np.float32), lanes(np.tile(cross, (n_seg, 1))), lanes(np.tile(into, (n_seg, 1))), lanes(cdec)


def _rope_tables(pos):
    half = C_DK // 2
    inv = ROPE_BASE ** (-np.arange(half, dtype=np.float64) / half)
    ang = np.asarray(pos, dtype=np.float64)[:, None] * inv[None, :]
    cos, sin = np.cos(ang), np.sin(ang)
    return (np.concatenate([cos, cos], axis=-1).astype(np.float32),
            np.concatenate([-sin, sin], axis=-1).astype(np.float32))


def kernel(x_prompt, x_sample, state_mlstm_C, state_mlstm_n, state_mlstm_m, state_ssd_conv, state_ssd_h, state_ret_S, state_ffn_conv, norm_mix_g, norm_ffn_g, norm_final_g, even_w_in, mlstm_igate_b, mlstm_fgate_b, mlstm_norm_g, ssd_conv_w, ssd_conv_b, ssd_dt_bias, ssd_A_log, ssd_D, ssd_norm_g, even_w_out, odd_w_in, ret_norm_g, odd_w_out, ffn_w_up, ffn_conv_w, ffn_conv_b, ffn_w_down):
    bsz, seq, d = x_prompt.shape
    dbsz, dseq, _ = x_sample.shape
    n_pad = SEQ_PAD - dseq
    assert norm_mix_g.shape[0] == 2 and d == D_MODEL and B_CONV - 1 <= n_pad < SEQ_PAD
    assert seq % (2 * PROMPT_CHUNKS_PER_HALF * CHUNK) == 0 and seq % PROMPT_FFN_ROWS == 0
    assert dbsz % SAMPLE_SEQS_PER_STEP == 0 and (dbsz * SEQ_PAD) % SAMPLE_ROW_TILE == 0

    row = lambda a: a.reshape(1, -1)
    gate_bias = jnp.concatenate([_pad_lanes(row(mlstm_igate_b[0])), _pad_lanes(row(mlstm_fgate_b[0])),
                                 _pad_lanes(row(ssd_dt_bias[0]))], axis=0)
    even_consts = (gate_bias, row(mlstm_norm_g[0]), ssd_conv_w[0], row(ssd_conv_b[0]), _pad_lanes(row(ssd_A_log[0])),
                   row(jnp.repeat(ssd_D[0], B_HEADDIM)), row(ssd_norm_g[0]))
    g_mix0, g_mix1 = row(norm_mix_g[0]), row(norm_mix_g[1])
    g_ffn0, g_ffn1, g_final = row(norm_ffn_g[0]), row(norm_ffn_g[1]), row(norm_final_g)
    w_even_in, w_even_out = _even_w_in_cols(even_w_in[0]), even_w_out[0].astype(bf16)
    w_odd_in, w_odd_out = odd_w_in[0].astype(bf16), odd_w_out[0].astype(bf16)
    w_up, w_dn = ffn_w_up.astype(bf16), ffn_w_down.astype(bf16)
    ffn_cb = ffn_conv_b[:, None, :]
    ret_norm = row(ret_norm_g[0])

    cos_p, sin_p = _rope_tables(np.arange(seq))
    xp = x_prompt.reshape(bsz * seq, d)
    xp, pC, pn, pm, pST, p_conv = _even_prompt(xp, g_mix0, w_even_in, even_consts, w_even_out,
                                               seq_rows=seq, L=CHUNK, n_sub=PROMPT_CHUNKS_PER_HALF)
    xp, p_f0 = _ffn(xp, 0, g_ffn0, w_up, ffn_conv_w[0], ffn_cb[0], w_dn, None, seq_rows=seq, tt=PROMPT_FFN_ROWS)
    xp, pSr = _odd_prompt(xp, g_mix1, w_odd_in, cos_p, sin_p, _retention_tables(CHUNK, 0) + (ret_norm,), w_odd_out,
                          seq_rows=seq, L=CHUNK, n_sub=PROMPT_CHUNKS_PER_HALF)
    _, p_f1, yp = _ffn(xp, 1, g_ffn1, w_up, ffn_conv_w[1], ffn_cb[1], w_dn, g_final,
                       seq_rows=seq, tt=PROMPT_FFN_ROWS)
    pS = pST.reshape(bsz, B_STATE, B_HEADS, B_HEADDIM).transpose(0, 2, 3, 1)

    xs = jnp.pad(x_sample, ((0, 0), (n_pad, 0), (0, 0))).reshape(dbsz * SEQ_PAD, d)
    n_seg = SAMPLE_SEQS_PER_STEP
    cos_s, sin_s = _rope_tables(np.tile(PAST_LEN + np.arange(SEQ_PAD) - n_pad, n_seg))
    hist = lambda a: jnp.pad(a, ((0, 0), (n_pad - a.shape[1], dseq), (0, 0))).reshape(dbsz * SEQ_PAD, a.shape[-1])
    state = (hist(state_ssd_conv[0]), state_mlstm_C[0],
             jnp.pad(state_mlstm_n[0], ((0, 0), (0, SUBLANES - A_HEADS), (0, 0))),
             _pad_lanes(state_mlstm_m[0])[:, None, :],
             state_ssd_h[0].reshape(dbsz, B_HEADS // 2, 2 * B_HEADDIM, B_STATE))
    to_tm = lambda a: a.reshape(dbsz, SEQ_PAD, d)[:, n_pad:, :].transpose(1, 0, 2).reshape(dseq * dbsz, d)
    from_tm = lambda a: jnp.pad(a.reshape(dseq, dbsz, d).transpose(1, 0, 2),
                                ((0, 0), (n_pad, 0), (0, 0))).reshape(dbsz * SEQ_PAD, d)
    tm = SAMPLE_ROW_TILE
    proj = _norm_proj(xs, g_mix0, w_even_in, EVEN_SPLITS, tm)
    y, sC, sn, sm, sS, s_conv = _even_sample(proj, state, even_consts, n_seg=n_seg, n_pad=n_pad)
    sS = sS.reshape(dbsz, B_HEADS, B_HEADDIM, B_STATE)
    xs = _proj_res(xs, y, w_even_out, tm)
    xt, s_f0 = _ffn_sample(to_tm(xs), state_ffn_conv, 0, g_ffn0, w_up, ffn_conv_w[0], ffn_cb[0], w_dn, None,
                           cw=SAMPLE_FFN_COLS)
    xs = from_tm(xt)
    proj = _norm_proj(xs, g_mix1, w_odd_in, ODD_SPLITS, tm)
    y, sSr = _odd_sample(proj, state_ret_S[0], cos_s, sin_s,
                         _retention_tables(SEQ_PAD, n_pad, n_seg) + (ret_norm,), n_seg=n_seg)
    xs = _proj_res(xs, y, w_odd_out, tm)
    _, s_f1, yt = _ffn_sample(to_tm(xs), state_ffn_conv, 1, g_ffn1, w_up, ffn_conv_w[1], ffn_cb[1], w_dn,
                              g_final, cw=SAMPLE_FFN_COLS)

    def conv_rows(tail, bs):
        return tail.reshape(bs, SUBLANES, tail.shape[-1])[:, SUBLANES - (B_CONV - 1):, :]

    def mlstm_states(C, n, m):
        return C[None], n[None, :, :A_HEADS, :], m[None, :, 0, :A_HEADS]

    p_ff = jnp.stack([p_f0, p_f1]).reshape(2, bsz, SUBLANES, 2 * D_FF)[:, :, SUBLANES - (FFN_CONV - 1):, :]
    s_ff = jnp.stack([s_f0, s_f1])
    p_states = mlstm_states(pC, pn, pm) + (conv_rows(p_conv, bsz)[None], pS[None], pSr[None], p_ff)
    s_states = mlstm_states(sC, sn, sm) + (conv_rows(s_conv, dbsz)[None], sS[None], sSr[None], s_ff)
    y_prompt = yp.reshape(bsz, seq, d)
    y_sample = yt.reshape(dseq, dbsz, d).transpose(1, 0, 2)
    return (y_prompt, y_sample) + p_states + s_states
```

```python
import functools

import numpy as np
import jax
import jax.numpy as jnp
from jax import lax
from jax.experimental import pallas as pl
from jax.experimental.pallas import tpu as pltpu

f32 = jnp.float32
bf16 = jnp.bfloat16

EPS = 1e-6
CHUNK = 128
D_MODEL = 1024
A_HEADS, A_DK, A_DV = 4, 128, 256
A_WIDTH = A_HEADS * A_DV
B_HEADS, B_HEADDIM, B_GROUPS, B_STATE, B_CONV = 16, 64, 2, 128, 4
B_WIDTH = B_HEADS * B_HEADDIM
B_CONV_DIM = B_WIDTH + 2 * B_GROUPS * B_STATE
B_HEADS_PER_GROUP = B_HEADS // B_GROUPS
B_GROUP_WIDTH = B_WIDTH // B_GROUPS
C_HEADS, C_DK, C_DV = 8, 128, 256
C_WIDTH = C_HEADS * C_DV
ROPE_BASE = 10000.0
D_FF = 2816
FFN_CONV = 3
PAST_LEN = 16384

LANES = 128
SUBLANES = 8
SEQ_PAD = SUBLANES
NEG_BIG = -1e30
VMEM_LIMIT = 56 * 1024 * 1024

PROMPT_CHUNKS_PER_HALF = 2
PROMPT_FFN_ROWS = 512
SAMPLE_SEQS_PER_STEP = 8
SAMPLE_ROW_TILE = 512
SAMPLE_FFN_COLS = D_FF // 2

EVEN_SPLITS = (A_HEADS * A_DK, A_HEADS * A_DK, A_WIDTH, A_WIDTH, B_WIDTH, B_CONV_DIM, 3 * LANES)
ODD_SPLITS = (C_HEADS * C_DK, C_HEADS * C_DK, C_WIDTH, C_WIDTH)


def _params(sem):
    return pltpu.CompilerParams(dimension_semantics=sem, vmem_limit_bytes=VMEM_LIMIT)


def _resident(shape, index_map):
    return pl.BlockSpec(shape, index_map, pipeline_mode=pl.Buffered(1))


def _nt(a, b):
    return lax.dot_general(a, b, (((1,), (1,)), ((), ())), preferred_element_type=f32)


def _tn(a, b):
    return lax.dot_general(a, b, (((0,), (0,)), ((), ())), preferred_element_type=f32)


def _nn(a, b):
    return jnp.dot(a, b, preferred_element_type=f32)


def _rmsnorm(x, g):
    return x * lax.rsqrt(jnp.mean(x * x, axis=-1, keepdims=True) + EPS) * g


def _layernorm_nogain(h):
    mu = jnp.mean(h, axis=-1, keepdims=True)
    hc = h - mu
    return hc * lax.rsqrt(jnp.mean(hc * hc, axis=-1, keepdims=True) + EPS)


def _softplus(x):
    return jnp.maximum(x, 0.0) + jnp.log1p(jnp.exp(-jnp.abs(x)))


def _silu(x):
    return x * jax.nn.sigmoid(x)


def _split3(x):
    hi = x.astype(bf16)
    r1 = x - hi.astype(f32)
    mid = r1.astype(bf16)
    lo = (r1 - mid.astype(f32)).astype(bf16)
    return hi, mid, lo


def _cumsum_rows(tril, x):
    hi, mid, lo = _split3(x)
    return _nn(tril, hi) + _nn(tril, mid) + _nn(tril, lo)


def _split_views(ref, splits):
    views, off = [], 0
    for n in splits:
        views.append(ref.at[:, off:off + n])
        off += n
    return views


def _project(h, w_ref, p_ref, splits):
    off = 0
    for n in splits:
        p_ref[:, off:off + n] = _nn(h, w_ref[:, off:off + n])
        off += n


MXU_COLS = 256


def _proj_pieces(h_ref, w_ref, dst_ref):
    def piece(c0, n):
        def run():
            dst_ref[:, c0:c0 + n] = _nn(h_ref[...], w_ref[:, c0:c0 + n])
        return run
    total = w_ref.shape[1]
    return [piece(c0, min(MXU_COLS, total - c0)) for c0 in range(0, total, MXU_COLS)]


def _out_pieces(x_ref, y_ref, w_ref, o_ref, rows):
    def piece(c0, n):
        def run():
            o_ref[rows, c0:c0 + n] = x_ref[rows, c0:c0 + n] + _nn(y_ref[...], w_ref[:, c0:c0 + n])
        return run
    total = w_ref.shape[1]
    return [piece(c0, min(MXU_COLS, total - c0)) for c0 in range(0, total, MXU_COLS)]


def _spread(work, n_slots):
    return [work[i * len(work) // n_slots:(i + 1) * len(work) // n_slots] for i in range(n_slots)]


def _norm_proj_body(x_ref, g_ref, w_ref, *o_refs, splits):
    h = _rmsnorm(x_ref[...], g_ref[...]).astype(bf16)
    off = 0
    for o_ref, n in zip(o_refs, splits):
        o_ref[...] = _nn(h, w_ref[:, off:off + n])
        off += n


def _norm_proj(x, g, w, splits, tm):
    rows, d = x.shape
    return pl.pallas_call(
        functools.partial(_norm_proj_body, splits=splits),
        grid=(rows // tm,),
        in_specs=[pl.BlockSpec((tm, d), lambda i: (i, 0)),
                  pl.BlockSpec((1, d), lambda i: (0, 0)),
                  _resident(w.shape, lambda i: (0, 0))],
        out_specs=[pl.BlockSpec((tm, n), lambda i: (i, 0)) for n in splits],
        out_shape=[jax.ShapeDtypeStruct((rows, n), f32) for n in splits],
        compiler_params=_params(("arbitrary",)),
        name="norm_proj",
    )(x, g, w)


def _proj_res_body(x_ref, y_ref, w_ref, o_ref):
    o_ref[...] = x_ref[...] + _nn(y_ref[...].astype(bf16), w_ref[...])


def _proj_res(x, y, w, tm):
    rows, d = x.shape
    k = y.shape[1]
    return pl.pallas_call(
        _proj_res_body,
        grid=(rows // tm,),
        in_specs=[pl.BlockSpec((tm, d), lambda i: (i, 0)),
                  pl.BlockSpec((tm, k), lambda i: (i, 0)),
                  _resident(w.shape, lambda i: (0, 0))],
        out_specs=pl.BlockSpec((tm, d), lambda i: (i, 0)),
        out_shape=jax.ShapeDtypeStruct((rows, d), f32),
        compiler_params=_params(("arbitrary",)),
        name="proj_res",
    )(x, y, w)


def _conv3(bias, prev2, prev1, cur, w_ref):
    y = bias + prev2 * w_ref[0:1, :]
    y = y + prev1 * w_ref[1:2, :]
    return y + cur * w_ref[2:3, :]


def _ffn_body(*refs, tt, final):
    it = iter(refs)
    x_ref, g_ref, wup_ref, cw_ref, cb_ref, wdn_ref = (next(it) for _ in range(6))
    gf_ref = next(it) if final else None
    o_ref, tail_ref = next(it), next(it)
    scr, carry = next(it), next(it)

    @pl.when(pl.program_id(1) == 0)
    def _():
        carry[...] = jnp.zeros(carry.shape, f32)

    x = x_ref[...]
    h = _rmsnorm(x, g_ref[...]).astype(bf16)
    conv = []
    for part in range(2):
        cols = slice(part * D_FF, (part + 1) * D_FF)
        u = _nn(h, wup_ref[:, cols])
        scr[0:SUBLANES, :] = carry[:, cols]
        tail_ref[:, cols] = u[tt - SUBLANES:tt, :]
        carry[:, cols] = u[tt - SUBLANES:tt, :]
        scr[SUBLANES:SUBLANES + tt, :] = u
        conv.append(_conv3(cb_ref[:, cols], scr[SUBLANES - 2:SUBLANES - 2 + tt, :],
                           scr[SUBLANES - 1:SUBLANES - 1 + tt, :], u, cw_ref.at[:, cols]))
    act = (_silu(conv[0]) * conv[1]).astype(bf16)
    out = x + _nn(act, wdn_ref[...])
    o_ref[...] = _rmsnorm(out, gf_ref[...]) if final else out


def _ffn(x, layer, g, wup, conv_w, conv_b, wdn, g_final, *, seq_rows, tt):
    rows, d = x.shape
    final = g_final is not None
    n_seq = rows // seq_rows
    n_inner = seq_rows // tt
    rmap = lambda s, t: (s * n_inner + t, 0)
    cmap = lambda s, t: (0, 0)
    lmap = lambda s, t: (layer, 0, 0)
    in_specs = [pl.BlockSpec((tt, d), rmap), pl.BlockSpec((1, d), cmap), _resident((None,) + wup.shape[1:], lmap),
                pl.BlockSpec(conv_w.shape, cmap), pl.BlockSpec(conv_b.shape, cmap),
                _resident((None,) + wdn.shape[1:], lmap)]
    args = [x, g, wup, conv_w, conv_b, wdn]
    out_specs = [pl.BlockSpec((tt, d), rmap), pl.BlockSpec((SUBLANES, 2 * D_FF), lambda s, t: (s, 0))]
    out_shape = [jax.ShapeDtypeStruct((rows, d), f32), jax.ShapeDtypeStruct((n_seq * SUBLANES, 2 * D_FF), f32)]
    if final:
        in_specs.append(pl.BlockSpec((1, d), cmap))
        args.append(g_final)
    return pl.pallas_call(
        functools.partial(_ffn_body, tt=tt, final=final),
        grid=(n_seq, n_inner), in_specs=in_specs, out_specs=out_specs, out_shape=out_shape,
        scratch_shapes=[pltpu.VMEM((SUBLANES + tt, D_FF), f32), pltpu.VMEM((SUBLANES, 2 * D_FF), f32)],
        compiler_params=_params(("arbitrary", "arbitrary")),
        name="ffn_prompt",
    )(*args)


def _ffn_sample_body(*refs, nb, nj, final):
    it = iter(refs)
    x_ref, hist_ref, g_ref, wup_ref, cw_ref, cb_ref, wdn_ref = (next(it) for _ in range(7))
    gf_ref = next(it) if final else None
    o_ref, tail_ref = next(it), next(it)
    h_scr, convg_scr, acc_scr = next(it), next(it), next(it)
    c = pl.program_id(0)
    tt = x_ref.shape[0]

    @pl.when(c == 0)
    def _():
        h_scr[...] = _rmsnorm(x_ref[...], g_ref[...]).astype(bf16)
        acc_scr[...] = jnp.zeros(acc_scr.shape, f32)

    u = _nn(h_scr[...], wup_ref[...])
    ext = jnp.concatenate([hist_ref[:, k, :] for k in range(FFN_CONV - 1)] + [u], axis=0)
    for k in range(FFN_CONV - 1):
        tail_ref[:, k, :] = ext[tt + k * nb:tt + (k + 1) * nb, :]
    y = _conv3(cb_ref[...], ext[0:tt, :], ext[nb:nb + tt, :], u, cw_ref)

    @pl.when(c < nj)
    def _():
        convg_scr[c] = y

    @pl.when(c >= nj)
    def _():
        act = (_silu(convg_scr[c - nj]) * y).astype(bf16)
        acc_scr[...] += _nn(act, wdn_ref[...])

    @pl.when(c == 2 * nj - 1)
    def _():
        out = x_ref[...] + acc_scr[...]
        o_ref[...] = _rmsnorm(out, gf_ref[...]) if final else out


def _ffn_sample(x, hist, layer, g, wup, conv_w, conv_b, wdn, g_final, *, cw):
    rows, d = x.shape
    nb = hist.shape[1]
    nj = D_FF // cw
    final = g_final is not None
    fixed = lambda c: (0, 0)
    colblk = lambda c: (0, c)
    in_specs = [pl.BlockSpec((rows, d), fixed),
                pl.BlockSpec((None, nb, FFN_CONV - 1, cw), lambda c: (layer, 0, 0, c)),
                pl.BlockSpec((1, d), fixed), pl.BlockSpec((None, d, cw), lambda c: (layer, 0, c)),
                pl.BlockSpec((FFN_CONV, cw), colblk), pl.BlockSpec((1, cw), colblk),
                pl.BlockSpec((None, cw, d), lambda c: (layer, jnp.maximum(c - nj, 0), 0))]
    args = [x, hist, g, wup, conv_w, conv_b, wdn]
    out_specs = [pl.BlockSpec((rows, d), fixed), pl.BlockSpec((nb, FFN_CONV - 1, cw), lambda c: (0, 0, c))]
    out_shape = [jax.ShapeDtypeStruct((rows, d), f32), jax.ShapeDtypeStruct((nb, FFN_CONV - 1, 2 * D_FF), f32)]
    if final:
        in_specs.append(pl.BlockSpec((1, d), fixed))
        args.append(g_final)
    return pl.pallas_call(
        functools.partial(_ffn_sample_body, nb=nb, nj=nj, final=final),
        grid=(2 * nj,), in_specs=in_specs, out_specs=out_specs, out_shape=out_shape,
        scratch_shapes=[pltpu.VMEM((rows, d), bf16), pltpu.VMEM((nj, rows, cw), f32), pltpu.VMEM((rows, d), f32)],
        compiler_params=_params(("arbitrary",)),
        name="ffn_sample",
    )(*args)


EVEN_SLOTS = A_HEADS + B_HEADS // 2


def _even_chunk(r0, slot, L, n_pad, n_seg, proj_refs, hist_ref, const_refs, y_ref, state_in_refs, state_refs,
                tail_ref, conv_scr, yb_scr, side=None):
    q_ref, k_ref, v_ref, og_ref, z_ref, xbc_ref, gt_ref = proj_refs
    gb_ref, anorm_ref, convw_ref, convb_ref, alog_ref, dx_ref, bnorm_ref = const_refs
    C_in, n_in, m_in, S_in = state_in_refs
    C_ref, n_ref, m_ref, S_ref = state_refs
    rows = pl.ds(r0, L)
    seg = L // n_seg
    seg_rows = [slice(i * seg, (i + 1) * seg) for i in range(n_seg)]
    seg_last = [slice((i + 1) * seg - 1, (i + 1) * seg) for i in range(n_seg)]

    def per_row(vals):
        if n_seg == 1:
            return vals[0]
        return jnp.concatenate([jnp.broadcast_to(v, (seg, v.shape[1])) for v in vals], axis=0)

    ri = lax.broadcasted_iota(jnp.int32, (L, L), 0)
    ci = lax.broadcasted_iota(jnp.int32, (L, L), 1)
    causal = ri >= ci
    if n_seg > 1:
        causal = causal & ((ri // seg) == (ci // seg))
    tril = causal.astype(bf16)
    valid = None
    if n_pad:
        valid = (lax.broadcasted_iota(jnp.int32, (L, 1), 0) % seg) >= n_pad

    gates = gt_ref[rows, :]
    li = gates[:, 0:LANES] + gb_ref[0:1, :]
    fpre = gates[:, LANES:2 * LANES] + gb_ref[1:2, :]
    lf = -_softplus(-fpre)
    dt = _softplus(gates[:, 2 * LANES:3 * LANES] + gb_ref[2:3, :])
    if n_pad:
        li = jnp.where(valid, li, NEG_BIG)
        lf = jnp.where(valid, lf, 0.0)
        dt = jnp.where(valid, dt, 0.0)
    a = dt * (-jnp.exp(alog_ref[...]))
    cums = _cumsum_rows(tril, jnp.concatenate([lf, a], axis=1))
    bcum = cums[:, 0:LANES]
    acum = cums[:, LANES:2 * LANES]
    liT, bT, aT, dtT = li.T, bcum.T, acum.T, dt.T

    m_old = [m_in[slot + i] for i in range(n_seg)]
    b_lasts = [bcum[r, :] for r in seg_last]
    m_rows, b_last = per_row(m_old), per_row(b_lasts)
    log_g = b_last - bcum + li
    m_news = [jnp.maximum(b_lasts[i] + m_old[i], jnp.max(log_g[seg_rows[i], :], axis=0, keepdims=True))
              for i in range(n_seg)]
    m_new = per_row(m_news)
    gfac = jnp.exp(log_g - m_new)
    cdecay = jnp.exp(b_last + m_rows - m_new)
    for i in range(n_seg):
        m_ref[slot + i] = m_news[i]
    for h in range(A_HEADS):
        for run in (side[h] if side else ()):
            run()
        bcol, brow, lirow = bcum[:, h:h + 1], bT[h:h + 1, :], liT[h:h + 1, :]
        m_h = m_rows[:, h:h + 1]
        logw = jnp.where(causal, bcol - brow + lirow, -jnp.inf)
        log_prev = bcol + m_h
        m_t = jnp.maximum(log_prev, jnp.max(logw, axis=-1, keepdims=True))
        w_in = jnp.exp(logw - m_t)
        w_prev = jnp.exp(log_prev - m_t)
        qh = q_ref[rows, h * A_DK:(h + 1) * A_DK]
        kh = k_ref[rows, h * A_DK:(h + 1) * A_DK] * (A_DK ** -0.5)
        vh = v_ref[rows, h * A_DV:(h + 1) * A_DV]
        vb = vh.astype(bf16)
        qb = qh.astype(bf16)
        s = _nt(qb, kh.astype(bf16)) * w_in
        kg = kh * gfac[:, h:h + 1]
        Cs = [C_in[slot + i, h] for i in range(n_seg)]
        ns = [n_in[slot + i, h:h + 1, :] for i in range(n_seg)]
        if n_seg == 1:
            inter = _nn(qb, Cs[0].astype(bf16))
            dec_h = cdecay[:, h:h + 1]
            C_ref[slot, h] = Cs[0] * dec_h + _tn(kg.astype(bf16), vb)
            n_ref[slot, h:h + 1, :] = ns[0] * dec_h + jnp.sum(kg, axis=0, keepdims=True)
        else:
            inter = jnp.concatenate([_nn(qh[seg_rows[i], :].astype(bf16), Cs[i].astype(bf16))
                                     for i in range(n_seg)], axis=0)
            for i in range(n_seg):
                rs = seg_rows[i]
                dec_h = cdecay[seg_last[i], h:h + 1]
                C_ref[slot + i, h] = Cs[i] * dec_h + _tn(kg[rs, :].astype(bf16), vh[rs, :].astype(bf16))
                n_ref[slot + i, h:h + 1, :] = ns[i] * dec_h + jnp.sum(kg[rs, :], axis=0, keepdims=True)
        num = _nn(s.astype(bf16), vb) + inter * w_prev
        den = jnp.sum(s, axis=-1, keepdims=True) + jnp.sum(qh * per_row(ns), axis=-1, keepdims=True) * w_prev
        hout = num / jnp.maximum(jnp.abs(den), jnp.exp(-m_t))
        ya = _layernorm_nogain(hout) * anorm_ref[:, h * A_DV:(h + 1) * A_DV]
        ya = ya * jax.nn.sigmoid(og_ref[rows, h * A_DV:(h + 1) * A_DV])
        y_ref[rows, h * A_DV:(h + 1) * A_DV] = ya.astype(y_ref.dtype)

    xraw = xbc_ref[rows, :]
    if hist_ref is not None:
        seg_row = lax.broadcasted_iota(jnp.int32, (L, 1), 0) % seg
        is_hist = (seg_row >= n_pad - (B_CONV - 1)) & jnp.logical_not(valid)
        xraw = jnp.where(is_hist, hist_ref[rows, :], xraw)
    conv_scr[SUBLANES:SUBLANES + L, :] = xraw
    xc = convb_ref[...] + conv_scr[SUBLANES - 3:SUBLANES - 3 + L, :] * convw_ref[0:1, :]
    xc = xc + conv_scr[SUBLANES - 2:SUBLANES - 2 + L, :] * convw_ref[1:2, :]
    xc = xc + conv_scr[SUBLANES - 1:SUBLANES - 1 + L, :] * convw_ref[2:3, :]
    xc = xc + xraw * convw_ref[3:4, :]
    if n_seg == 1:
        new_tail = conv_scr[L:L + SUBLANES, :]
        tail_ref[...] = new_tail
        conv_scr[0:SUBLANES, :] = new_tail
    else:
        tail_ref[...] = xraw
    xc = _silu(xc)

    a_last = per_row([acum[r, :] for r in seg_last])
    wtile = jnp.exp(a_last - acum) * dt
    expa = jnp.exp(acum)

    def head_matrix(cb, h):
        acol, arow, dtrow = acum[:, h:h + 1], aT[h:h + 1, :], dtT[h:h + 1, :]
        decay = jnp.where(causal, jnp.exp(jnp.where(causal, acol - arow, 0.0)), 0.0)
        return (cb * decay * dtrow).astype(bf16)

    def group_bc(g):
        Bg = xc[:, B_WIDTH + g * B_STATE:B_WIDTH + (g + 1) * B_STATE].astype(bf16)
        c0 = B_WIDTH + B_GROUPS * B_STATE + g * B_STATE
        Cg = xc[:, c0:c0 + B_STATE].astype(bf16)
        return Bg, Cg, _nt(Cg, Bg)

    low_half = lax.broadcasted_iota(jnp.int32, (L, LANES), 1) < B_HEADDIM
    first_head_rows = lax.broadcasted_iota(jnp.int32, (2 * B_HEADDIM, 1), 0) < B_HEADDIM

    def pair_lanes(t, j):
        return jnp.where(low_half, jnp.broadcast_to(t[:, 2 * j:2 * j + 1], (L, LANES)),
                         jnp.broadcast_to(t[:, 2 * j + 1:2 * j + 2], (L, LANES)))

    pairs_per_group = B_HEADS_PER_GROUP // 2
    for g in range(B_GROUPS):
        Bg, Cg, cb = group_bc(g)
        gs = slice(g * B_GROUP_WIDTH, (g + 1) * B_GROUP_WIDTH)
        if n_seg == 1:
            ST = S_in[slot, :, gs]
            inter_g = _nn(Cg, ST.astype(bf16))
        else:
            Bf = xc[:, B_WIDTH + g * B_STATE:B_WIDTH + (g + 1) * B_STATE]
            c0 = B_WIDTH + B_GROUPS * B_STATE + g * B_STATE
            Cf = xc[:, c0:c0 + B_STATE]
        xw, decs = [], []
        for jp in range(pairs_per_group):
            j = g * pairs_per_group + jp
            for run in (side[A_HEADS + j] if side else ()):
                run()
            ps = slice(j * LANES, (j + 1) * LANES)
            e_pair, w_pair = pair_lanes(expa, j), pair_lanes(wtile, j)
            mcat = jnp.concatenate([head_matrix(cb, 2 * j), head_matrix(cb, 2 * j + 1)], axis=1)
            xp = xc[:, ps]
            xbd = jnp.concatenate([jnp.where(low_half, xp, 0.0).astype(bf16),
                                   jnp.where(low_half, 0.0, xp).astype(bf16)], axis=0)
            if n_seg == 1:
                inter = inter_g[:, jp * LANES:(jp + 1) * LANES]
                xw.append((xp * w_pair).astype(bf16))
                decs.append(e_pair[L - 1:L, :])
            else:
                xwf = xp * w_pair
                parts = []
                for i in range(n_seg):
                    rs = seg_rows[i]
                    Sp = S_in[slot + i, j]
                    parts.append(_nt(Cf[rs, :].astype(bf16), Sp.astype(bf16)))
                    e_last = e_pair[seg_last[i], :]
                    dec_col = jnp.where(first_head_rows, e_last[:, 0:1], e_last[:, B_HEADDIM:B_HEADDIM + 1])
                    S_ref[slot + i, j] = Sp * dec_col + _tn(xwf[rs, :].astype(bf16), Bf[rs, :].astype(bf16))
                inter = jnp.concatenate(parts, axis=0)
            yb_scr[:, ps] = (_nn(mcat, xbd) + inter * e_pair) + dx_ref[:, ps] * xp
        if n_seg == 1:
            S_ref[slot, :, gs] = ST * jnp.concatenate(decs, axis=1) + _tn(Bg, jnp.concatenate(xw, axis=1))
    yb = yb_scr[...] * _silu(z_ref[rows, :])
    for g in range(B_GROUPS):
        gs = slice(g * B_GROUP_WIDTH, (g + 1) * B_GROUP_WIDTH)
        yg = yb[:, gs]
        yg = yg * lax.rsqrt(jnp.mean(yg * yg, axis=-1, keepdims=True) + EPS)
        y_ref[rows, A_WIDTH + g * B_GROUP_WIDTH:A_WIDTH + (g + 1) * B_GROUP_WIDTH] = (
            yg * bnorm_ref[:, gs]).astype(y_ref.dtype)


N_EVEN_CONSTS = 7


def _even_prompt_body(*refs, L, n_sub):
    it = iter(refs)
    x_ref, xnext_ref, gmix_ref, win_ref = next(it), next(it), next(it), next(it)
    const_refs = tuple(next(it) for _ in range(N_EVEN_CONSTS))
    wout_ref = next(it)
    o_ref = next(it)
    state_refs = tuple(next(it) for _ in range(4))
    tail_ref = next(it)
    p_scr, y_scr = (next(it), next(it)), next(it)
    h_scr, conv_scr, yb_scr = (next(it) for _ in range(3))
    th = L * n_sub
    gmix = gmix_ref[...]

    @pl.when((pl.program_id(0) == 0) & (pl.program_id(1) == 0))
    def _():
        _project(_rmsnorm(x_ref[0:th, :], gmix).astype(bf16), win_ref, p_scr[0], EVEN_SPLITS)

    @pl.when(pl.program_id(1) == 0)
    def _():
        conv_scr[0:SUBLANES, :] = jnp.zeros((SUBLANES, B_CONV_DIM), f32)
        for r in state_refs:
            r[...] = jnp.zeros(r.shape, f32)

    for half in range(2):
        ahead = xnext_ref[...] if half else x_ref[th:2 * th, :]
        h_scr[...] = _rmsnorm(ahead, gmix).astype(bf16)
        side = _spread(_proj_pieces(h_scr, win_ref, p_scr[1 - half]), n_sub * EVEN_SLOTS)
        proj_refs = _split_views(p_scr[half], EVEN_SPLITS)
        y_half = y_scr.at[half * th:(half + 1) * th, :]
        for c in range(n_sub):
            _even_chunk(c * L, 0, L, 0, 1, proj_refs, None, const_refs, y_half, state_refs, state_refs,
                        tail_ref, conv_scr, yb_scr, side[c * EVEN_SLOTS:(c + 1) * EVEN_SLOTS])
    for run in _out_pieces(x_ref, y_scr, wout_ref, o_ref, slice(0, 2 * th)):
        run()


def _even_prompt(x, gmix, win, consts, wout, *, seq_rows, L, n_sub):
    rows, d = x.shape
    n_seq = rows // seq_rows
    th = L * n_sub
    tt = 2 * th
    n_inner = seq_rows // tt
    n_half = rows // th
    rmap = lambda s, t: (s * n_inner + t, 0)
    nextmap = lambda s, t: (jnp.minimum(2 * (s * n_inner + t) + 2, n_half - 1), 0)
    cmap = lambda s, t: (0, 0)
    smap4 = lambda s, t: (s, 0, 0, 0)
    smap3 = lambda s, t: (s, 0, 0)
    st_specs = [pl.BlockSpec((1, A_HEADS, A_DK, A_DV), smap4), pl.BlockSpec((1, SUBLANES, A_DK), smap3),
                pl.BlockSpec((1, 1, LANES), smap3), pl.BlockSpec((1, B_STATE, B_WIDTH), smap3)]
    st_shapes = [jax.ShapeDtypeStruct((n_seq, A_HEADS, A_DK, A_DV), f32),
                 jax.ShapeDtypeStruct((n_seq, SUBLANES, A_DK), f32),
                 jax.ShapeDtypeStruct((n_seq, 1, LANES), f32),
                 jax.ShapeDtypeStruct((n_seq, B_STATE, B_WIDTH), f32)]
    in_specs = ([pl.BlockSpec((tt, d), rmap), pl.BlockSpec((th, d), nextmap), pl.BlockSpec((1, d), cmap),
                 _resident(win.shape, cmap)]
                + [pl.BlockSpec(c.shape, cmap) for c in consts] + [_resident(wout.shape, cmap)])
    p_shape = pltpu.VMEM((th, sum(EVEN_SPLITS)), f32)
    y_shape = pltpu.VMEM((tt, A_WIDTH + B_WIDTH), bf16)
    return pl.pallas_call(
        functools.partial(_even_prompt_body, L=L, n_sub=n_sub),
        grid=(n_seq, n_inner), in_specs=in_specs,
        out_specs=[pl.BlockSpec((tt, d), rmap)] + st_specs + [pl.BlockSpec((SUBLANES, B_CONV_DIM), lambda s, t: (s, 0))],
        out_shape=[jax.ShapeDtypeStruct((rows, d), f32)] + st_shapes
        + [jax.ShapeDtypeStruct((n_seq * SUBLANES, B_CONV_DIM), f32)],
        scratch_shapes=[p_shape, p_shape, y_shape, pltpu.VMEM((th, d), bf16),
                        pltpu.VMEM((SUBLANES + L, B_CONV_DIM), f32), pltpu.VMEM((L, B_WIDTH), f32)],
        compiler_params=_params(("arbitrary", "arbitrary")),
        name="even_prompt",
    )(x, x, gmix, win, *consts, wout)


def _even_sample_body(*refs, L, n_seg, n_pad):
    it = iter(refs)
    proj_refs = tuple(next(it) for _ in range(7))
    hist_ref = next(it)
    state_in_refs = tuple(next(it) for _ in range(4))
    const_refs = tuple(next(it) for _ in range(N_EVEN_CONSTS))
    y_ref = next(it)
    state_refs = tuple(next(it) for _ in range(4))
    tail_ref = next(it)
    conv_scr, yb_scr = next(it), next(it)

    conv_scr[0:SUBLANES, :] = jnp.zeros((SUBLANES, B_CONV_DIM), f32)
    n_ref = state_refs[1]
    n_ref[:, A_HEADS:, :] = jnp.zeros((n_seg, SUBLANES - A_HEADS, A_DK), f32)
    _even_chunk(0, 0, L, n_pad, n_seg, proj_refs, hist_ref, const_refs, y_ref, state_in_refs, state_refs,
                tail_ref, conv_scr, yb_scr)


def _even_sample(proj, state, consts, *, n_seg, n_pad):
    rows = proj[0].shape[0]
    L = SEQ_PAD * n_seg
    rmap = lambda o: (o, 0)
    cmap = lambda o: (0, 0)
    smap4 = lambda o: (o, 0, 0, 0)
    smap3 = lambda o: (o, 0, 0)
    st_specs = [pl.BlockSpec((n_seg, A_HEADS, A_DK, A_DV), smap4), pl.BlockSpec((n_seg, SUBLANES, A_DK), smap3),
                pl.BlockSpec((n_seg, 1, LANES), smap3),
                pl.BlockSpec((n_seg, B_HEADS // 2, 2 * B_HEADDIM, B_STATE), smap4)]
    st_shapes = [jax.ShapeDtypeStruct(s.shape, f32) for s in state[1:]]
    in_specs = ([pl.BlockSpec((L, a.shape[1]), rmap) for a in proj] + [pl.BlockSpec((L, B_CONV_DIM), rmap)]
                + st_specs + [pl.BlockSpec(c.shape, cmap) for c in consts])
    return pl.pallas_call(
        functools.partial(_even_sample_body, L=L, n_seg=n_seg, n_pad=n_pad),
        grid=(rows // L,), in_specs=in_specs,
        out_specs=[pl.BlockSpec((L, A_WIDTH + B_WIDTH), rmap)] + st_specs + [pl.BlockSpec((L, B_CONV_DIM), rmap)],
        out_shape=[jax.ShapeDtypeStruct((rows, A_WIDTH + B_WIDTH), f32)] + st_shapes
        + [jax.ShapeDtypeStruct((rows, B_CONV_DIM), f32)],
        scratch_shapes=[pltpu.VMEM((SUBLANES + L, B_CONV_DIM), f32), pltpu.VMEM((L, B_WIDTH), f32)],
        compiler_params=_params(("arbitrary",)),
        name="even_sample",
    )(*proj, *state, *consts)


def _odd_chunk(r0, rope_r0, slot, L, n_seg, proj_refs, const_refs, y_ref, S_in, S_ref, side=None):
    q_ref, k_ref, v_ref, g_ref = proj_refs
    cos_ref, sin_ref, intra_ref, cross_ref, into_ref, cdec_ref, norm_ref = const_refs
    rows = pl.ds(r0, L)
    seg = L // n_seg
    cosf = cos_ref[pl.ds(rope_r0, L), :]
    sinf = sin_ref[pl.ds(rope_r0, L), :]
    cross = cross_ref[...]
    into = into_ref[...]
    cdec = cdec_ref[...]
    for h in range(C_HEADS):
        for run in (side[h] if side else ()):
            run()
        qh = q_ref[rows, h * C_DK:(h + 1) * C_DK]
        kh = k_ref[rows, h * C_DK:(h + 1) * C_DK]
        qh = qh * cosf + pltpu.roll(qh, C_DK // 2, 1) * sinf
        kh = (kh * cosf + pltpu.roll(kh, C_DK // 2, 1) * sinf) * (C_DK ** -0.5)
        vh = v_ref[rows, h * C_DV:(h + 1) * C_DV]
        vb = vh.astype(bf16)
        qb = qh.astype(bf16)
        s = _nt(qb, kh.astype(bf16)) * intra_ref[h]
        kd = kh * into[:, h:h + 1]
        if n_seg == 1:
            S = S_in[slot, h]
            inter = _nn(qb, S.astype(bf16))
            S_ref[slot, h] = S * cdec[:, h:h + 1] + _tn(kd.astype(bf16), vb)
        else:
            parts = []
            for i in range(n_seg):
                rs = slice(i * seg, (i + 1) * seg)
                S = S_in[slot + i, h]
                parts.append(_nn(qh[rs, :].astype(bf16), S.astype(bf16)))
                S_ref[slot + i, h] = S * cdec[:, h:h + 1] + _tn(kd[rs, :].astype(bf16), vh[rs, :].astype(bf16))
            inter = jnp.concatenate(parts, axis=0)
        o = _nn(s.astype(bf16), vb) + inter * cross[:, h:h + 1]
        y = _layernorm_nogain(o) * norm_ref[:, h * C_DV:(h + 1) * C_DV]
        y = y * _silu(g_ref[rows, h * C_DV:(h + 1) * C_DV])
        y_ref[rows, h * C_DV:(h + 1) * C_DV] = y.astype(y_ref.dtype)


N_ODD_CONSTS = 7


def _odd_prompt_body(*refs, L, n_sub):
    it = iter(refs)
    x_ref, xnext_ref, gmix_ref, win_ref = next(it), next(it), next(it), next(it)
    const_refs = tuple(next(it) for _ in range(N_ODD_CONSTS))
    wout_ref = next(it)
    o_ref, S_ref = next(it), next(it)
    p_scr, y_scr = (next(it), next(it)), next(it)
    h_scr = next(it)
    th = L * n_sub
    gmix = gmix_ref[...]

    @pl.when((pl.program_id(0) == 0) & (pl.program_id(1) == 0))
    def _():
        _project(_rmsnorm(x_ref[0:th, :], gmix).astype(bf16), win_ref, p_scr[0], ODD_SPLITS)

    @pl.when(pl.program_id(1) == 0)
    def _():
        S_ref[...] = jnp.zeros(S_ref.shape, f32)

    for half in range(2):
        ahead = xnext_ref[...] if half else x_ref[th:2 * th, :]
        h_scr[...] = _rmsnorm(ahead, gmix).astype(bf16)
        side = _spread(_proj_pieces(h_scr, win_ref, p_scr[1 - half]), n_sub * C_HEADS)
        proj_refs = _split_views(p_scr[half], ODD_SPLITS)
        y_half = y_scr.at[half * th:(half + 1) * th, :]
        for c in range(n_sub):
            _odd_chunk(c * L, half * th + c * L, 0, L, 1, proj_refs, const_refs, y_half, S_ref, S_ref,
                       side[c * C_HEADS:(c + 1) * C_HEADS])
    for run in _out_pieces(x_ref, y_scr, wout_ref, o_ref, slice(0, 2 * th)):
        run()


def _odd_prompt(x, gmix, win, cos, sin, consts, wout, *, seq_rows, L, n_sub):
    rows, d = x.shape
    n_seq = rows // seq_rows
    th = L * n_sub
    tt = 2 * th
    n_inner = seq_rows // tt
    n_half = rows // th
    rmap = lambda s, t: (s * n_inner + t, 0)
    nextmap = lambda s, t: (jnp.minimum(2 * (s * n_inner + t) + 2, n_half - 1), 0)
    cmap = lambda s, t: (0, 0)
    st_spec = pl.BlockSpec((1, C_HEADS, C_DK, C_DV), lambda s, t: (s, 0, 0, 0))
    rope_spec = pl.BlockSpec((tt, LANES), lambda s, t: (t, 0))
    in_specs = [pl.BlockSpec((tt, d), rmap), pl.BlockSpec((th, d), nextmap), pl.BlockSpec((1, d), cmap),
                _resident(win.shape, cmap), rope_spec, rope_spec]
    for c in consts:
        in_specs.append(pl.BlockSpec(c.shape, (lambda s, t: (0, 0, 0)) if c.ndim == 3 else cmap))
    in_specs.append(_resident(wout.shape, cmap))
    p_shape = pltpu.VMEM((th, sum(ODD_SPLITS)), f32)
    y_shape = pltpu.VMEM((tt, C_WIDTH), bf16)
    return pl.pallas_call(
        functools.partial(_odd_prompt_body, L=L, n_sub=n_sub),
        grid=(n_seq, n_inner), in_specs=in_specs,
        out_specs=[pl.BlockSpec((tt, d), rmap), st_spec],
        out_shape=[jax.ShapeDtypeStruct((rows, d), f32), jax.ShapeDtypeStruct((n_seq, C_HEADS, C_DK, C_DV), f32)],
        scratch_shapes=[p_shape, p_shape, y_shape, pltpu.VMEM((th, d), bf16)],
        compiler_params=_params(("arbitrary", "arbitrary")),
        name="odd_prompt",
    )(x, x, gmix, win, cos, sin, *consts, wout)


def _odd_sample_body(*refs, L, n_seg):
    it = iter(refs)
    proj_refs = tuple(next(it) for _ in range(4))
    S_in = next(it)
    const_refs = tuple(next(it) for _ in range(N_ODD_CONSTS))
    y_ref, S_ref = next(it), next(it)
    _odd_chunk(0, 0, 0, L, n_seg, proj_refs, const_refs, y_ref, S_in, S_ref)


def _odd_sample(proj, S0, cos, sin, consts, *, n_seg):
    rows = proj[0].shape[0]
    L = SEQ_PAD * n_seg
    rmap = lambda o: (o, 0)
    cmap = lambda o: (0, 0)
    st_spec = pl.BlockSpec((n_seg, C_HEADS, C_DK, C_DV), lambda o: (o, 0, 0, 0))
    in_specs = [pl.BlockSpec((L, a.shape[1]), rmap) for a in proj] + [st_spec]
    for c in (cos, sin) + tuple(consts):
        in_specs.append(pl.BlockSpec(c.shape, (lambda o: (0, 0, 0)) if c.ndim == 3 else cmap))
    return pl.pallas_call(
        functools.partial(_odd_sample_body, L=L, n_seg=n_seg),
        grid=(rows // L,), in_specs=in_specs,
        out_specs=[pl.BlockSpec((L, C_WIDTH), rmap), st_spec],
        out_shape=[jax.ShapeDtypeStruct((rows, C_WIDTH), f32), jax.ShapeDtypeStruct(S0.shape, f32)],
        compiler_params=_params(("arbitrary",)),
        name="odd_sample",
    )(*proj, S0, cos, sin, *consts)


def _pad_lanes(a, width=LANES):
    return jnp.pad(a, [(0, 0)] * (a.ndim - 1) + [(0, width - a.shape[-1])])


def _even_w_in_cols(w):
    sizes = [A_HEADS * A_DK, A_HEADS * A_DK, A_WIDTH, A_HEADS, A_HEADS, A_WIDTH, B_WIDTH, B_CONV_DIM, B_HEADS]
    q, k, v, ig, fg, og, z, xbc, dt = jnp.split(w, np.cumsum(sizes)[:-1].tolist(), axis=-1)
    return jnp.concatenate([q, k, v, og, z, xbc, _pad_lanes(ig), _pad_lanes(fg), _pad_lanes(dt)], axis=-1).astype(bf16)


def _retention_tables(seg, n_pad, n_seg=1):
    log_gamma = np.log1p(-np.exp2(-5.0 - np.arange(C_HEADS, dtype=np.float64)))
    t_real = seg - n_pad
    idx = np.arange(seg, dtype=np.float64) - n_pad
    real = idx >= 0
    diff = idx[:, None] - idx[None, :]
    intra = np.where((diff >= 0) & real[None, :], np.exp(log_gamma[:, None, None] * np.maximum(diff, 0.0)), 0.0)
    intra = np.stack([np.kron(np.eye(n_seg), intra[h]) for h in range(C_HEADS)])
    cross = np.where(real[:, None], np.exp(log_gamma[None, :] * (idx[:, None] + 1.0)), 0.0)
    into = np.where(real[:, None], np.exp(log_gamma[None, :] * (t_real - 1.0 - idx[:, None])), 0.0)
    cdec = np.exp(log_gamma * t_real)[None, :]
    lanes = lambda a: np.pad(a, ((0, 0), (0, LANES - a.shape[1]))).astype(np.float32)
    return intra.astype(np.float32), lanes(np.tile(cross, (n_seg, 1))), lanes(np.tile(into, (n_seg, 1))), lanes(cdec)


def _rope_tables(pos):
    half = C_DK // 2
    inv = ROPE_BASE ** (-np.arange(half, dtype=np.float64) / half)
    ang = np.asarray(pos, dtype=np.float64)[:, None] * inv[None, :]
    cos, sin = np.cos(ang), np.sin(ang)
    return (np.concatenate([cos, cos], axis=-1).astype(np.float32),
            np.concatenate([-sin, sin], axis=-1).astype(np.float32))


def kernel(x_prompt, x_sample, state_mlstm_C, state_mlstm_n, state_mlstm_m, state_ssd_conv, state_ssd_h, state_ret_S, state_ffn_conv, norm_mix_g, norm_ffn_g, norm_final_g, even_w_in, mlstm_igate_b, mlstm_fgate_b, mlstm_norm_g, ssd_conv_w, ssd_conv_b, ssd_dt_bias, ssd_A_log, ssd_D, ssd_norm_g, even_w_out, odd_w_in, ret_norm_g, odd_w_out, ffn_w_up, ffn_conv_w, ffn_conv_b, ffn_w_down):
    bsz, seq, d = x_prompt.shape
    dbsz, dseq, _ = x_sample.shape
    n_pad = SEQ_PAD - dseq
    assert norm_mix_g.shape[0] == 2 and d == D_MODEL and B_CONV - 1 <= n_pad < SEQ_PAD
    assert seq % (2 * PROMPT_CHUNKS_PER_HALF * CHUNK) == 0 and seq % PROMPT_FFN_ROWS == 0
    assert dbsz % SAMPLE_SEQS_PER_STEP == 0 and (dbsz * SEQ_PAD) % SAMPLE_ROW_TILE == 0

    row = lambda a: a.reshape(1, -1)
    gate_bias = jnp.concatenate([_pad_lanes(row(mlstm_igate_b[0])), _pad_lanes(row(mlstm_fgate_b[0])),
                                 _pad_lanes(row(ssd_dt_bias[0]))], axis=0)
    even_consts = (gate_bias, row(mlstm_norm_g[0]), ssd_conv_w[0], row(ssd_conv_b[0]), _pad_lanes(row(ssd_A_log[0])),
                   row(jnp.repeat(ssd_D[0], B_HEADDIM)), row(ssd_norm_g[0]))
    g_mix0, g_mix1 = row(norm_mix_g[0]), row(norm_mix_g[1])
    g_ffn0, g_ffn1, g_final = row(norm_ffn_g[0]), row(norm_ffn_g[1]), row(norm_final_g)
    w_even_in, w_even_out = _even_w_in_cols(even_w_in[0]), even_w_out[0].astype(bf16)
    w_odd_in, w_odd_out = odd_w_in[0].astype(bf16), odd_w_out[0].astype(bf16)
    w_up, w_dn = ffn_w_up.astype(bf16), ffn_w_down.astype(bf16)
    ffn_cb = ffn_conv_b[:, None, :]
    ret_norm = row(ret_norm_g[0])

    cos_p, sin_p = _rope_tables(np.arange(seq))
    xp = x_prompt.reshape(bsz * seq, d)
    xp, pC, pn, pm, pST, p_conv = _even_prompt(xp, g_mix0, w_even_in, even_consts, w_even_out,
                                               seq_rows=seq, L=CHUNK, n_sub=PROMPT_CHUNKS_PER_HALF)
    xp, p_f0 = _ffn(xp, 0, g_ffn0, w_up, ffn_conv_w[0], ffn_cb[0], w_dn, None, seq_rows=seq, tt=PROMPT_FFN_ROWS)
    xp, pSr = _odd_prompt(xp, g_mix1, w_odd_in, cos_p, sin_p, _retention_tables(CHUNK, 0) + (ret_norm,), w_odd_out,
                          seq_rows=seq, L=CHUNK, n_sub=PROMPT_CHUNKS_PER_HALF)
    yp, p_f1 = _ffn(xp, 1, g_ffn1, w_up, ffn_conv_w[1], ffn_cb[1], w_dn, g_final,
                       seq_rows=seq, tt=PROMPT_FFN_ROWS)
    pS = pST.reshape(bsz, B_STATE, B_HEADS, B_HEADDIM).transpose(0, 2, 3, 1)

    xs = jnp.pad(x_sample, ((0, 0), (n_pad, 0), (0, 0))).reshape(dbsz * SEQ_PAD, d)
    n_seg = SAMPLE_SEQS_PER_STEP
    cos_s, sin_s = _rope_tables(np.tile(PAST_LEN + np.arange(SEQ_PAD) - n_pad, n_seg))
    hist = lambda a: jnp.pad(a, ((0, 0), (n_pad - a.shape[1], dseq), (0, 0))).reshape(dbsz * SEQ_PAD, a.shape[-1])
    state = (hist(state_ssd_conv[0]), state_mlstm_C[0],
             jnp.pad(state_mlstm_n[0], ((0, 0), (0, SUBLANES - A_HEADS), (0, 0))),
             _pad_lanes(state_mlstm_m[0])[:, None, :],
             state_ssd_h[0].reshape(dbsz, B_HEADS // 2, 2 * B_HEADDIM, B_STATE))
    to_tm = lambda a: a.reshape(dbsz, SEQ_PAD, d)[:, n_pad:, :].transpose(1, 0, 2).reshape(dseq * dbsz, d)
    from_tm = lambda a: jnp.pad(a.reshape(dseq, dbsz, d).transpose(1, 0, 2),
                                ((0, 0), (n_pad, 0), (0, 0))).reshape(dbsz * SEQ_PAD, d)
    tm = SAMPLE_ROW_TILE
    proj = _norm_proj(xs, g_mix0, w_even_in, EVEN_SPLITS, tm)
    y, sC, sn, sm, sS, s_conv = _even_sample(proj, state, even_consts, n_seg=n_seg, n_pad=n_pad)
    sS = sS.reshape(dbsz, B_HEADS, B_HEADDIM, B_STATE)
    xs = _proj_res(xs, y, w_even_out, tm)
    xt, s_f0 = _ffn_sample(to_tm(xs), state_ffn_conv, 0, g_ffn0, w_up, ffn_conv_w[0], ffn_cb[0], w_dn, None,
                           cw=SAMPLE_FFN_COLS)
    xs = from_tm(xt)
    proj = _norm_proj(xs, g_mix1, w_odd_in, ODD_SPLITS, tm)
    y, sSr = _odd_sample(proj, state_ret_S[0], cos_s, sin_s,
                         _retention_tables(SEQ_PAD, n_pad, n_seg) + (ret_norm,), n_seg=n_seg)
    xs = _proj_res(xs, y, w_odd_out, tm)
    yt, s_f1 = _ffn_sample(to_tm(xs), state_ffn_conv, 1, g_ffn1, w_up, ffn_conv_w[1], ffn_cb[1], w_dn,
                              g_final, cw=SAMPLE_FFN_COLS)

    def conv_rows(tail, bs):
        return tail.reshape(bs, SUBLANES, tail.shape[-1])[:, SUBLANES - (B_CONV - 1):, :]

    def mlstm_states(C, n, m):
        return C[None], n[None, :, :A_HEADS, :], m[None, :, 0, :A_HEADS]

    p_ff = jnp.stack([p_f0, p_f1]).reshape(2, bsz, SUBLANES, 2 * D_FF)[:, :, SUBLANES - (FFN_CONV - 1):, :]
    s_ff = jnp.stack([s_f0, s_f1])
    p_states = mlstm_states(pC, pn, pm) + (conv_rows(p_conv, bsz)[None], pS[None], pSr[None], p_ff)
    s_states = mlstm_states(sC, sn, sm) + (conv_rows(s_conv, dbsz)[None], sS[None], sSr[None], s_ff)
    y_prompt = yp.reshape(bsz, seq, d)
    y_sample = yt.reshape(dseq, dbsz, d).transpose(1, 0, 2)
    return (y_prompt, y_sample) + p_states + s_states
```
